```python
import math
import jax, jax.numpy as jnp
from jax import lax
import numpy as np

D_MODEL = 1024
BATCH = 8
SEQ = 8192
DEPTH = 4

CHUNK = 64
Q_BLOCK = 128
RET_HEADS = 4
RET_QK_DIM = 128
RET_V_DIM = 256
DIFF_HEADS = 8
DIFF_HEAD_DIM = 64
DIFF_V_DIM = 2 * DIFF_HEAD_DIM
D_FF = 2816
N_SUB = 3
NORM_EPS = 1e-6

RET_QK = RET_HEADS * RET_QK_DIM
RET_V = RET_HEADS * RET_V_DIM
DIFF_QK = DIFF_HEADS * 2 * DIFF_HEAD_DIM
DIFF_V = DIFF_HEADS * DIFF_V_DIM
SPLITS = (RET_QK, RET_QK, RET_V, RET_V, DIFF_QK, DIFF_QK, DIFF_V, D_MODEL, D_MODEL)
IN_WIDTH = sum(SPLITS)
SPLIT_IDX = [int(i) for i in np.cumsum(SPLITS)[:-1]]

kernel_name = "hybrid_retention_diffattn_macaron_adaln"


def rmsnorm(x, w):
    xf = x.astype(jnp.float32)
    y = xf * lax.rsqrt(jnp.mean(xf * xf, axis=-1, keepdims=True) + NORM_EPS)
    return (y * w.astype(jnp.float32)).astype(x.dtype)


def modulate(h, shift, scale):
    return h * (1 + scale[:, None, :]) + shift[:, None, :]


def swiglu(h, w_up, w_down):
    a, b = jnp.split(h @ w_up, 2, axis=-1)
    return (jax.nn.silu(a) * b) @ w_down


def retention(q, k, v):
    B, S, H, dk = q.shape
    dv = v.shape[-1]
    n_chunks = S // CHUNK
    f32 = jnp.float32
    gamma = 1.0 - 2.0 ** (-5.0 - jnp.arange(H, dtype=f32))
    log_g = jnp.log(gamma)
    r = jnp.arange(CHUNK, dtype=f32)
    intra = jnp.exp(log_g[:, None, None] * jnp.abs(r[:, None] - r[None, :]))
    q_dec = jnp.exp(log_g[:, None] * r[None, :])
    k_dec = jnp.exp(log_g[:, None] * (CHUNK - r)[None, :])
    chunk_dec = jnp.exp(log_g * CHUNK)

    def to_chunks(t):
        return t.astype(f32).reshape(B, n_chunks, CHUNK, H, t.shape[-1]).transpose(1, 0, 3, 2, 4)

    qc, kc, vc = to_chunks(q), to_chunks(k) * (dk ** -0.5), to_chunks(v)

    def step(state, inp):
        qi, ki, vi = inp
        scores = jnp.einsum('bhqd,bhkd->bhqk', qi, ki) * intra
        y = jnp.einsum('bhqk,bhkv->bhqv', scores, vi)
        y = y + jnp.einsum('bhqd,bhdv->bhqv', qi * q_dec[:, :, None], state)
        state = state * chunk_dec[:, None, None] + jnp.einsum('bhkd,bhkv->bhdv', ki * k_dec[:, :, None], vi)
        return state, y

    state0 = jnp.zeros((B, H, dk, dv), f32)
    _, y = lax.scan(step, state0, (qc, kc, vc))
    return y.transpose(1, 0, 3, 2, 4).reshape(B, S, H, dv)


def alibi_slopes(n_heads):
    return 2.0 ** (-8.0 * jnp.arange(1, n_heads + 1, dtype=jnp.float32) / n_heads)


def diff_attention(q, k, v, lam):
    B, S, H, _, d = q.shape
    nb = S // Q_BLOCK
    f32 = jnp.float32
    qb = q.reshape(B, nb, Q_BLOCK, H, 2, d).transpose(1, 0, 3, 4, 2, 5)
    kt = k.transpose(0, 2, 3, 1, 4)
    vt = v.transpose(0, 2, 1, 3)
    slopes = alibi_slopes(H)
    key_pos = jnp.arange(S)
    scale = d ** -0.5

    def block(inp):
        qi, bi = inp
        q_pos = bi * Q_BLOCK + jnp.arange(Q_BLOCK)
        allowed = (key_pos[None, :] // CHUNK) <= (q_pos[:, None] // CHUNK)
        dist = jnp.abs(q_pos[:, None] - key_pos[None, :]).astype(f32)
        bias = jnp.where(allowed[None], -slopes[:, None, None] * dist[None], -jnp.inf)
        s = jnp.einsum('bhiqd,bhikd->bhiqk', qi, kt).astype(f32) * scale + bias[None, :, None]
        p = jax.nn.softmax(s, axis=-1)
        a = p[:, :, 0] - lam * p[:, :, 1]
        return jnp.einsum('bhqk,bhkv->bhqv', a.astype(vt.dtype), vt)

    out = lax.map(block, (qb, jnp.arange(nb)))
    return out.transpose(1, 0, 3, 2, 4).reshape(B, S, H, 2 * d)


def hybrid_mixer(h, w_in, ret_gn, lam_q1, lam_k1, lam_q2, lam_k2, subln, w_ret_branch, w_diff_branch, w_out, layer_idx):
    B, S, _ = h.shape
    f32 = jnp.float32
    proj = h @ w_in
    rq, rk, rv, rg, dq, dk, dv, g_ret, g_diff = jnp.split(proj, SPLIT_IDX, axis=-1)

    y_ret = retention(rq.reshape(B, S, RET_HEADS, RET_QK_DIM),
                      rk.reshape(B, S, RET_HEADS, RET_QK_DIM),
                      rv.reshape(B, S, RET_HEADS, RET_V_DIM))
    y_ret = rmsnorm(y_ret, ret_gn.reshape(RET_HEADS, RET_V_DIM)).reshape(B, S, RET_V).astype(h.dtype)
    y_ret = y_ret * jax.nn.silu(rg)

    lam_init = 0.8 - 0.6 * math.exp(-0.3 * layer_idx)
    lam = (jnp.exp(jnp.sum(lam_q1.astype(f32) * lam_k1.astype(f32)))
           - jnp.exp(jnp.sum(lam_q2.astype(f32) * lam_k2.astype(f32))) + lam_init)
    y_diff = diff_attention(dq.reshape(B, S, DIFF_HEADS, 2, DIFF_HEAD_DIM),
                            dk.reshape(B, S, DIFF_HEADS, 2, DIFF_HEAD_DIM),
                            dv.reshape(B, S, DIFF_HEADS, DIFF_V_DIM), lam)
    y_diff = (rmsnorm(y_diff, subln) * (1.0 - lam_init)).reshape(B, S, DIFF_V)

    merged = (jax.nn.sigmoid(g_ret) * (y_ret @ w_ret_branch)
              + jax.nn.sigmoid(g_diff) * (y_diff @ w_diff_branch))
    return merged @ w_out


def setup_inputs(seed: int = 0) -> dict:
    key = jax.random.key(seed)
    ks = jax.random.split(key, 20)
    nrm = jax.random.normal
    f32 = jnp.float32
    return {
        "x": nrm(ks[0], (BATCH, SEQ, D_MODEL), f32),
        "c": nrm(ks[1], (BATCH, D_MODEL), f32),
        "w_ada": nrm(ks[2], (DEPTH, D_MODEL, N_SUB * 3 * D_MODEL), f32) * (0.5 * D_MODEL ** -0.5),
        "b_ada": nrm(ks[3], (DEPTH, N_SUB * 3 * D_MODEL), f32) * 0.02,
        "norm_w": 1.0 + 0.02 * nrm(ks[4], (DEPTH, N_SUB, D_MODEL), f32),
        "w_ffn_up": nrm(ks[5], (DEPTH, 2, D_MODEL, 2 * D_FF), f32) * D_MODEL ** -0.5,
        "w_ffn_down": nrm(ks[6], (DEPTH, 2, D_FF, D_MODEL), f32) * D_FF ** -0.5,
        "w_in": nrm(ks[7], (DEPTH, D_MODEL, IN_WIDTH), f32) * D_MODEL ** -0.5,
        "ret_gn": 1.0 + 0.02 * nrm(ks[8], (DEPTH, RET_V), f32),
        "lambda_q1": 0.1 * nrm(ks[9], (DEPTH, DIFF_HEAD_DIM), f32),
        "lambda_k1": 0.1 * nrm(ks[10], (DEPTH, DIFF_HEAD_DIM), f32),
        "lambda_q2": 0.1 * nrm(ks[11], (DEPTH, DIFF_HEAD_DIM), f32),
        "lambda_k2": 0.1 * nrm(ks[12], (DEPTH, DIFF_HEAD_DIM), f32),
        "diff_subln": 1.0 + 0.02 * nrm(ks[13], (DEPTH, DIFF_V_DIM), f32),
        "w_ret_branch": nrm(ks[14], (DEPTH, RET_V, D_MODEL), f32) * RET_V ** -0.5,
        "w_diff_branch": nrm(ks[15], (DEPTH, DIFF_V, D_MODEL), f32) * DIFF_V ** -0.5,
        "w_out": nrm(ks[16], (DEPTH, D_MODEL, D_MODEL), f32) * D_MODEL ** -0.5,
        "final_norm": 1.0 + 0.02 * nrm(ks[17], (D_MODEL,), f32),
    }


def reference(x, c, w_ada, b_ada, norm_w, w_ffn_up, w_ffn_down, w_in, ret_gn, lambda_q1, lambda_k1, lambda_q2, lambda_k2, diff_subln, w_ret_branch, w_diff_branch, w_out, final_norm):
    B = x.shape[0]
    cond = jax.nn.silu(c)
    for l in range(DEPTH):
        mod = (cond @ w_ada[l] + b_ada[l]).reshape(B, N_SUB, 3, D_MODEL)
        shift, scale, gate = mod[:, :, 0], mod[:, :, 1], mod[:, :, 2]
        h = modulate(rmsnorm(x, norm_w[l, 0]), shift[:, 0], scale[:, 0])
        x = x + 0.5 * gate[:, 0, None, :] * swiglu(h, w_ffn_up[l, 0], w_ffn_down[l, 0])
        h = modulate(rmsnorm(x, norm_w[l, 1]), shift[:, 1], scale[:, 1])
        x = x + gate[:, 1, None, :] * hybrid_mixer(h, w_in[l], ret_gn[l], lambda_q1[l], lambda_k1[l],
                                                    lambda_q2[l], lambda_k2[l], diff_subln[l],
                                                    w_ret_branch[l], w_diff_branch[l], w_out[l], l)
        h = modulate(rmsnorm(x, norm_w[l, 2]), shift[:, 2], scale[:, 2])
        x = x + 0.5 * gate[:, 2, None, :] * swiglu(h, w_ffn_up[l, 1], w_ffn_down[l, 1])
    return rmsnorm(x, final_norm)
```

```python
import functools
import math

import jax
import jax.numpy as jnp
from jax import lax
from jax.experimental import pallas as pl
from jax.experimental.pallas import tpu as pltpu

DEPTH = 4
D_MODEL = 1024
CHUNK = 64
CHUNK_SHIFT = 6
RET_HEADS = 4
RET_QK_DIM = 128
RET_V_DIM = 256
DIFF_HEADS = 8
DIFF_HEAD_DIM = 64
DIFF_V_DIM = 2 * DIFF_HEAD_DIM
D_FF = 2816
N_SUB = 3
NORM_EPS = 1e-6
IN_WIDTH = 8 * D_MODEL

F32 = jnp.float32
BF16 = jnp.bfloat16

V7X_VMEM_LIMIT_BYTES = 56 * 1024 * 1024

FFN_TM = 512
FFN_FC = 256
RET_L = 256
ATT_TQ = 256
ATT_TK = 256
NEG_BIG = -1e30


def _resident(shape):
    nd = len(shape)
    return pl.BlockSpec(shape, lambda *_: (0,) * nd, pipeline_mode=pl.Buffered(1))


def _params(*sem):
    return pltpu.CompilerParams(dimension_semantics=sem, vmem_limit_bytes=V7X_VMEM_LIMIT_BYTES)


def _adaln_kernel(c_ref, w_ref, b_ref, nw_ref, o_ref):
    j = pl.program_id(1)
    c = c_ref[...]
    cond = c / (1.0 + jnp.exp(-c))
    r = jnp.dot(cond, w_ref[0], preferred_element_type=F32,
                precision=lax.Precision.HIGHEST) + b_ref[0]
    kind = j % 3
    sub = j // 3
    r = jnp.where(kind == 1, (1.0 + r) * nw_ref[0, 0], r)
    r = jnp.where(jnp.logical_and(kind == 2, sub != 1), 0.5 * r, r)
    o_ref[0] = r


def _adaln(c, w_ada, b_ada, norm_w):
    B = c.shape[0]
    n_tiles = N_SUB * 3
    return pl.pallas_call(
        _adaln_kernel,
        grid=(DEPTH, n_tiles),
        in_specs=[
            pl.BlockSpec((B, D_MODEL), lambda l, j: (0, 0)),
            pl.BlockSpec((1, D_MODEL, D_MODEL), lambda l, j: (l, 0, j)),
            pl.BlockSpec((1, 1, D_MODEL), lambda l, j: (l, 0, j)),
            pl.BlockSpec((1, 1, 1, D_MODEL), lambda l, j: (l, j // 3, 0, 0)),
        ],
        out_specs=pl.BlockSpec((1, B, D_MODEL), lambda l, j: (l, 0, j)),
        out_shape=jax.ShapeDtypeStruct((DEPTH, B, n_tiles * D_MODEL), F32),
        compiler_params=_params("arbitrary", "arbitrary"),
        name="adaln_mod",
    )(c, w_ada, b_ada.reshape(DEPTH, 1, n_tiles * D_MODEL), norm_w.reshape(DEPTH, N_SUB, 1, D_MODEL))


def _modulated_norm(x, mod):
    ms = jnp.mean(x * x, axis=-1, keepdims=True)
    return x * lax.rsqrt(ms + NORM_EPS) * mod[1:2] + mod[0:1]


def _ffn_kernel(x_ref, mod_ref, wup_ref, wdn_ref, *rest, n_chunks, final):
    if final:
        fw_ref, o_ref = rest
    else:
        (o_ref,) = rest
    x = x_ref[0]
    mod = mod_ref[0]
    h = _modulated_norm(x, mod).astype(BF16)
    acc = jnp.zeros(x.shape, F32)
    for c in range(n_chunks):
        u = jnp.dot(h, wup_ref[c], preferred_element_type=F32)
        a = u[:, :FFN_FC]
        b = u[:, FFN_FC:]
        act = (0.5 * a) * (1.0 + jnp.tanh(0.5 * a)) * b
        acc = acc + jnp.dot(act.astype(BF16), wdn_ref[c], preferred_element_type=F32)
    y = x + mod[2:3] * acc
    if final:
        ms = jnp.mean(y * y, axis=-1, keepdims=True)
        y = y * lax.rsqrt(ms + NORM_EPS) * fw_ref[...]
    o_ref[0] = y


def _ffn(x, mod, wup, wdn, final_w=None):
    B, S, D = x.shape
    n_chunks = wup.shape[0]
    final = final_w is not None
    in_specs = [
        pl.BlockSpec((1, FFN_TM, D), lambda b, t: (b, t, 0)),
        pl.BlockSpec((1, 3, D), lambda b, t: (b, 0, 0)),
        _resident(wup.shape),
        _resident(wdn.shape),
    ]
    args = [x, mod, wup, wdn]
    if final:
        in_specs.append(_resident((1, D)))
        args.append(final_w.reshape(1, D))
    return pl.pallas_call(
        functools.partial(_ffn_kernel, n_chunks=n_chunks, final=final),
        grid=(B, S // FFN_TM),
        in_specs=in_specs,
        out_specs=pl.BlockSpec((1, FFN_TM, D), lambda b, t: (b, t, 0)),
        out_shape=jax.ShapeDtypeStruct(x.shape, F32),
        compiler_params=_params("parallel", "parallel"),
        name="ffn_final" if final else "ffn",
    )(*args)


def _inproj_kernel(x_ref, mod_ref, w_ref, rq_ref, rk_ref, rv_ref, rg_ref, dq_ref, dk_ref, dv_ref,
                   gr_ref, gd_ref):
    x = x_ref[0]
    h = _modulated_norm(x, mod_ref[0]).astype(BF16)

    def proj(c):
        return jnp.dot(h, w_ref[:, c * D_MODEL:(c + 1) * D_MODEL], preferred_element_type=F32)

    u = proj(0).astype(BF16)
    for hh in range(RET_HEADS):
        rq_ref[0, hh] = u[:, hh * RET_QK_DIM:(hh + 1) * RET_QK_DIM]
        rk_ref[0, hh] = u[:, (RET_HEADS + hh) * RET_QK_DIM:(RET_HEADS + hh + 1) * RET_QK_DIM]
    u = proj(1).astype(BF16)
    for hh in range(RET_HEADS):
        rv_ref[0, hh] = u[:, hh * RET_V_DIM:(hh + 1) * RET_V_DIM]
    rg_ref[0] = proj(2).astype(BF16)
    u = (proj(3) * (DIFF_HEAD_DIM ** -0.5)).astype(BF16)
    for hh in range(DIFF_HEADS):
        dq_ref[0, hh] = u[:, hh * DIFF_V_DIM:(hh + 1) * DIFF_V_DIM]
    u = proj(4).astype(BF16)
    for hh in range(DIFF_HEADS):
        dk_ref[0, hh] = u[:, hh * DIFF_V_DIM:(hh + 1) * DIFF_V_DIM]
    u = proj(5).astype(BF16)
    for hh in range(DIFF_HEADS):
        dv_ref[0, hh] = u[:, hh * DIFF_V_DIM:(hh + 1) * DIFF_V_DIM]
    gr_ref[0] = proj(6).astype(BF16)
    gd_ref[0] = proj(7).astype(BF16)


def _inproj(x, mod, w_in):
    B, S, D = x.shape
    TM = FFN_TM
    head_spec = lambda nh, w: pl.BlockSpec((1, nh, TM, w), lambda b, t: (b, 0, t, 0))
    tok_spec = pl.BlockSpec((1, TM, D), lambda b, t: (b, t, 0))
    sds = jax.ShapeDtypeStruct
    return pl.pallas_call(
        _inproj_kernel,
        grid=(B, S // TM),
        in_specs=[tok_spec, pl.BlockSpec((1, 3, D), lambda b, t: (b, 0, 0)), _resident(w_in.shape)],
        out_specs=[
            head_spec(RET_HEADS, RET_QK_DIM), head_spec(RET_HEADS, RET_QK_DIM),
            head_spec(RET_HEADS, RET_V_DIM), tok_spec,
            head_spec(DIFF_HEADS, DIFF_V_DIM), head_spec(DIFF_HEADS, DIFF_V_DIM),
            head_spec(DIFF_HEADS, DIFF_V_DIM), tok_spec, tok_spec,
        ],
        out_shape=[
            sds((B, RET_HEADS, S, RET_QK_DIM), BF16), sds((B, RET_HEADS, S, RET_QK_DIM), BF16),
            sds((B, RET_HEADS, S, RET_V_DIM), BF16), sds((B, S, D), BF16),
            sds((B, DIFF_HEADS, S, DIFF_V_DIM), BF16), sds((B, DIFF_HEADS, S, DIFF_V_DIM), BF16),
            sds((B, DIFF_HEADS, S, DIFF_V_DIM), BF16), sds((B, S, D), BF16), sds((B, S, D), BF16),
        ],
        compiler_params=_params("parallel", "parallel"),
        name="mixer_inproj",
    )(x, mod, w_in)


def _retention_consts():
    L = RET_L
    gamma = 1.0 - 2.0 ** (-5.0 - jnp.arange(RET_HEADS, dtype=F32))
    log_g = jnp.log(gamma)
    r = jnp.arange(L, dtype=F32)
    scale = RET_QK_DIM ** -0.5
    allowed = (jnp.arange(L)[None, :] // CHUNK) <= (jnp.arange(L)[:, None] // CHUNK)
    intra = jnp.exp(log_g[:, None, None] * jnp.abs(r[:, None] - r[None, :]))
    intra = jnp.where(allowed[None], intra, 0.0) * scale
    q_dec = jnp.exp(log_g[:, None] * r[None, :]) * scale
    k_dec = jnp.exp(log_g[:, None] * (L - r)[None, :])
    blk_dec = jnp.exp(log_g * L)
    q_dec = jnp.broadcast_to(q_dec[:, :, None], (RET_HEADS, L, RET_QK_DIM))
    k_dec = jnp.broadcast_to(k_dec[:, :, None], (RET_HEADS, L, RET_QK_DIM))
    blk_dec = jnp.broadcast_to(blk_dec[:, None, None], (RET_HEADS, 1, RET_V_DIM))
    return intra, q_dec, k_dec, blk_dec


def _retention_kernel(q_ref, k_ref, v_ref, g_ref, w_ref, qd_ref, kd_ref, bd_ref, gn_ref, o_ref, state_ref):
    t = pl.program_id(2)

    @pl.when(t == 0)
    def _():
        state_ref[...] = jnp.zeros_like(state_ref)

    q = q_ref[0, 0]
    k = k_ref[0, 0]
    v = v_ref[0, 0]
    s = lax.dot_general(q, k, (((1,), (1,)), ((), ())), preferred_element_type=F32)
    p = (s * w_ref[0]).astype(BF16)
    y = jnp.dot(p, v, preferred_element_type=F32)
    state = state_ref[...]
    qd = (q.astype(F32) * qd_ref[0]).astype(BF16)
    y = y + jnp.dot(qd, state.astype(BF16), preferred_element_type=F32)
    kd_t = (k.astype(F32) * kd_ref[0]).T.astype(BF16)
    state_ref[...] = state * bd_ref[0] + jnp.dot(kd_t, v, preferred_element_type=F32)
    ms = jnp.mean(y * y, axis=-1, keepdims=True)
    y = y * lax.rsqrt(ms + NORM_EPS) * gn_ref[0]
    g = g_ref[0].astype(F32)
    o_ref[0] = (y * (0.5 * g) * (1.0 + jnp.tanh(0.5 * g))).astype(BF16)


def _retention(rq, rk, rv, rg, ret_gn_l):
    B, H, S, dk = rq.shape
    dv = rv.shape[-1]
    L = RET_L
    intra, q_dec, k_dec, blk_dec = _retention_consts()
    blk = lambda w: pl.BlockSpec((1, 1, L, w), lambda b, h, t: (b, h, t, 0))
    per_head = lambda shape: pl.BlockSpec((1,) + shape, lambda b, h, t: (h, 0, 0))
    return pl.pallas_call(
        _retention_kernel,
        grid=(B, H, S // L),
        in_specs=[
            blk(dk), blk(dk), blk(dv),
            pl.BlockSpec((1, L, dv), lambda b, h, t: (b, t, h)),
            per_head((L, L)), per_head((L, dk)), per_head((L, dk)), per_head((1, dv)), per_head((1, dv)),
        ],
        out_specs=pl.BlockSpec((1, L, dv), lambda b, h, t: (b, t, h)),
        out_shape=jax.ShapeDtypeStruct((B, S, H * dv), BF16),
        scratch_shapes=[pltpu.VMEM((dk, dv), F32)],
        compiler_params=_params("parallel", "parallel", "arbitrary"),
        name="retention",
    )(rq, rk, rv, rg, intra, q_dec, k_dec, blk_dec, ret_gn_l.reshape(H, 1, dv))


def _diffattn_kernel(slope_ref, q_ref, k_ref, v_ref, lam_ref, sub_ref, o_ref, *, lam_init):
    h = pl.program_id(1)
    i = pl.program_id(2)
    TQ, TK = ATT_TQ, ATT_TK
    slope = slope_ref[h]
    q = q_ref[0, 0]
    lane = lax.broadcasted_iota(jnp.int32, q.shape, 1)
    zero = jnp.zeros_like(q)
    qbd = jnp.concatenate([jnp.where(lane < DIFF_HEAD_DIM, q, zero),
                           jnp.where(lane >= DIFF_HEAD_DIM, q, zero)], axis=0)
    row = lax.broadcasted_iota(jnp.int32, (2 * TQ, TK), 0)
    q_pos = i * TQ + jnp.where(row >= TQ, row - TQ, row)
    col = lax.broadcasted_iota(jnp.int32, (2 * TQ, TK), 1)

    def body(j, carry):
        m, l, acc = carry
        k = k_ref[0, 0, pl.ds(pl.multiple_of(j * TK, TK), TK), :]
        v = v_ref[0, 0, pl.ds(pl.multiple_of(j * TK, TK), TK), :]
        s = lax.dot_general(qbd, k, (((1,), (1,)), ((), ())), preferred_element_type=F32)
        k_pos = j * TK + col
        allowed = jnp.right_shift(k_pos, CHUNK_SHIFT) <= jnp.right_shift(q_pos, CHUNK_SHIFT)
        dist = jnp.abs(q_pos - k_pos).astype(F32)
        s = jnp.where(allowed, s - slope * dist, NEG_BIG)
        m_new = jnp.maximum(m, jnp.max(s, axis=-1, keepdims=True))
        alpha = jnp.exp(m - m_new)
        p = jnp.exp(s - m_new)
        l = alpha * l + jnp.sum(p, axis=-1, keepdims=True)
        acc = alpha * acc + jnp.dot(p.astype(BF16), v, preferred_element_type=F32)
        return m_new, l, acc

    m0 = jnp.full((2 * TQ, 1), NEG_BIG, F32)
    l0 = jnp.zeros((2 * TQ, 1), F32)
    acc0 = jnp.zeros((2 * TQ, DIFF_V_DIM), F32)
    m, l, acc = lax.fori_loop(0, i + 1, body, (m0, l0, acc0))
    o = acc / l
    lp = lam_ref[...]
    lam = (jnp.exp(jnp.sum(lp[0:1] * lp[1:2], axis=-1, keepdims=True))
           - jnp.exp(jnp.sum(lp[2:3] * lp[3:4], axis=-1, keepdims=True)) + lam_init)
    y = o[:TQ] - lam * o[TQ:]
    ms = jnp.mean(y * y, axis=-1, keepdims=True)
    y = y * lax.rsqrt(ms + NORM_EPS) * sub_ref[...] * (1.0 - lam_init)
    o_ref[0] = y.astype(BF16)


def _diffattn(dq, dk, dv, lam_params, subln, layer_idx):
    B, H, S, w = dq.shape
    lam_init = 0.8 - 0.6 * math.exp(-0.3 * layer_idx)
    slopes = 2.0 ** (-8.0 * jnp.arange(1, H + 1, dtype=F32) / H)
    grid_spec = pltpu.PrefetchScalarGridSpec(
        num_scalar_prefetch=1,
        grid=(B, H, S // ATT_TQ),
        in_specs=[
            pl.BlockSpec((1, 1, ATT_TQ, w), lambda b, h, i, sl: (b, h, i, 0)),
            pl.BlockSpec((1, 1, S, w), lambda b, h, i, sl: (b, h, 0, 0)),
            pl.BlockSpec((1, 1, S, w), lambda b, h, i, sl: (b, h, 0, 0)),
            pl.BlockSpec((4, DIFF_HEAD_DIM), lambda b, h, i, sl: (0, 0)),
            pl.BlockSpec((1, w), lambda b, h, i, sl: (0, 0)),
        ],
        out_specs=pl.BlockSpec((1, ATT_TQ, w), lambda b, h, i, sl: (b, i, h)),
    )
    return pl.pallas_call(
        functools.partial(_diffattn_kernel, lam_init=lam_init),
        grid_spec=grid_spec,
        out_shape=jax.ShapeDtypeStruct((B, S, H * w), BF16),
        compiler_params=_params("parallel", "parallel", "parallel"),
        name="diff_attention",
    )(slopes, dq, dk, dv, lam_params, subln.reshape(1, w))


def _merge_kernel(x_ref, mod_ref, yr_ref, yd_ref, gr_ref, gd_ref, wr_ref, wd_ref, wo_ref, o_ref):
    def sigmoid(g):
        return 0.5 * (1.0 + jnp.tanh(0.5 * g))

    br = jnp.dot(yr_ref[0], wr_ref[...], preferred_element_type=F32)
    bd = jnp.dot(yd_ref[0], wd_ref[...], preferred_element_type=F32)
    merged = sigmoid(gr_ref[0].astype(F32)) * br + sigmoid(gd_ref[0].astype(F32)) * bd
    out = jnp.dot(merged.astype(BF16), wo_ref[...], preferred_element_type=F32)
    o_ref[0] = x_ref[0] + mod_ref[0][2:3] * out


def _merge(x, mod, y_ret, y_diff, g_ret, g_diff, w_rb, w_db, w_o):
    B, S, D = x.shape
    tok = pl.BlockSpec((1, FFN_TM, D), lambda b, t: (b, t, 0))
    return pl.pallas_call(
        _merge_kernel,
        grid=(B, S // FFN_TM),
        in_specs=[tok, pl.BlockSpec((1, 3, D), lambda b, t: (b, 0, 0)), tok, tok, tok, tok,
                  _resident(w_rb.shape), _resident(w_db.shape), _resident(w_o.shape)],
        out_specs=tok,
        out_shape=jax.ShapeDtypeStruct(x.shape, F32),
        compiler_params=_params("parallel", "parallel"),
        name="mixer_merge",
    )(x, mod, y_ret, y_diff, g_ret, g_diff, w_rb, w_db, w_o)


def _prep_ffn_weights(w_up, w_down):
    n = D_FF // FFN_FC
    up = w_up.astype(BF16).reshape(D_MODEL, 2, n, FFN_FC).transpose(2, 0, 1, 3).reshape(n, D_MODEL, 2 * FFN_FC)
    down = w_down.astype(BF16).reshape(n, FFN_FC, D_MODEL)
    return up, down


def kernel(x, c, w_ada, b_ada, norm_w, w_ffn_up, w_ffn_down, w_in, ret_gn, lambda_q1, lambda_k1, lambda_q2,
           lambda_k2, diff_subln, w_ret_branch, w_diff_branch, w_out, final_norm):
    B, S, D = x.shape
    assert D == D_MODEL and S % max(FFN_TM, RET_L, ATT_TQ) == 0 and ATT_TQ == ATT_TK
    mod_all = _adaln(c, w_ada, b_ada, norm_w).reshape(DEPTH, B, N_SUB, 3, D)
    for l in range(DEPTH):
        mod = [mod_all[l, :, s] for s in range(N_SUB)]
        up, down = _prep_ffn_weights(w_ffn_up[l, 0], w_ffn_down[l, 0])
        x = _ffn(x, mod[0], up, down)
        rq, rk, rv, rg, dq, dk, dv, g_ret, g_diff = _inproj(x, mod[1], w_in[l].astype(BF16))
        y_ret = _retention(rq, rk, rv, rg, ret_gn[l])
        lam_params = jnp.stack([lambda_q1[l], lambda_k1[l], lambda_q2[l], lambda_k2[l]])
        y_diff = _diffattn(dq, dk, dv, lam_params, diff_subln[l], l)
        x = _merge(x, mod[1], y_ret, y_diff, g_ret, g_diff, w_ret_branch[l].astype(BF16),
                   w_diff_branch[l].astype(BF16), w_out[l].astype(BF16))
        up, down = _prep_ffn_weights(w_ffn_up[l, 1], w_ffn_down[l, 1])
        x = _ffn(x, mod[2], up, down, final_w=final_norm if l == DEPTH - 1 else None)
    return x
```

```python
import functools
import math

import jax
import jax.numpy as jnp
from jax import lax
from jax.experimental import pallas as pl
from jax.experimental.pallas import tpu as pltpu

DEPTH = 4
D_MODEL = 1024
CHUNK = 64
CHUNK_SHIFT = 6
RET_HEADS = 4
RET_QK_DIM = 128
RET_V_DIM = 256
DIFF_HEADS = 8
DIFF_HEAD_DIM = 64
DIFF_V_DIM = 2 * DIFF_HEAD_DIM
D_FF = 2816
N_SUB = 3
NORM_EPS = 1e-6
IN_WIDTH = 8 * D_MODEL

F32 = jnp.float32
BF16 = jnp.bfloat16

V7X_VMEM_LIMIT_BYTES = 56 * 1024 * 1024

FFN_TM = 512
FFN_FC = 256
RET_L = 256
ATT_TQ = 512
ATT_TK = 512
ATT_SUM_ROWS = 16
NEG_BIG = -1e30
LOG2E = math.log2(math.e)


def _resident(shape):
    nd = len(shape)
    return pl.BlockSpec(shape, lambda *_: (0,) * nd, pipeline_mode=pl.Buffered(1))


def _params(*sem, flags=None):
    return pltpu.CompilerParams(dimension_semantics=sem, vmem_limit_bytes=V7X_VMEM_LIMIT_BYTES, flags=flags)


def _adaln_kernel(c_ref, w_ref, b_ref, nw_ref, o_ref):
    j = pl.program_id(1)
    c = c_ref[...]
    cond = c / (1.0 + jnp.exp(-c))
    r = jnp.dot(cond, w_ref[0], preferred_element_type=F32,
                precision=lax.Precision.HIGHEST) + b_ref[0]
    kind = j % 3
    sub = j // 3
    r = jnp.where(kind == 1, (1.0 + r) * nw_ref[0, 0], r)
    r = jnp.where(jnp.logical_and(kind == 2, sub != 1), 0.5 * r, r)
    o_ref[0] = r


def _adaln(c, w_ada, b_ada, norm_w):
    B = c.shape[0]
    n_tiles = N_SUB * 3
    return pl.pallas_call(
        _adaln_kernel,
        grid=(DEPTH, n_tiles),
        in_specs=[
            pl.BlockSpec((B, D_MODEL), lambda l, j: (0, 0)),
            pl.BlockSpec((1, D_MODEL, D_MODEL), lambda l, j: (l, 0, j)),
            pl.BlockSpec((1, 1, D_MODEL), lambda l, j: (l, 0, j)),
            pl.BlockSpec((1, 1, 1, D_MODEL), lambda l, j: (l, j // 3, 0, 0)),
        ],
        out_specs=pl.BlockSpec((1, B, D_MODEL), lambda l, j: (l, 0, j)),
        out_shape=jax.ShapeDtypeStruct((DEPTH, B, n_tiles * D_MODEL), F32),
        compiler_params=_params("arbitrary", "arbitrary"),
        name="adaln_mod",
    )(c, w_ada, b_ada.reshape(DEPTH, 1, n_tiles * D_MODEL), norm_w.reshape(DEPTH, N_SUB, 1, D_MODEL))


def _modulated_norm(x, mod):
    ms = jnp.mean(x * x, axis=-1, keepdims=True)
    return x * lax.rsqrt(ms + NORM_EPS) * mod[1:2] + mod[0:1]


def _ffn_kernel(x_ref, mod_ref, wup_ref, wdn_ref, *rest, n_chunks, final):
    if final:
        fw_ref, o_ref = rest
    else:
        (o_ref,) = rest
    x = x_ref[0]
    mod = mod_ref[0]
    h = _modulated_norm(x, mod).astype(BF16)
    acc = jnp.zeros(x.shape, F32)
    for c in range(n_chunks):
        u = jnp.dot(h, wup_ref[c], preferred_element_type=F32)
        a = u[:, :FFN_FC]
        b = u[:, FFN_FC:]
        act = (0.5 * a) * (1.0 + jnp.tanh(0.5 * a)) * b
        acc = acc + jnp.dot(act.astype(BF16), wdn_ref[c], preferred_element_type=F32)
    y = x + mod[2:3] * acc
    if final:
        ms = jnp.mean(y * y, axis=-1, keepdims=True)
        y = y * lax.rsqrt(ms + NORM_EPS) * fw_ref[...]
    o_ref[0] = y


def _ffn(x, mod, wup, wdn, final_w=None):
    B, S, D = x.shape
    n_chunks = wup.shape[0]
    final = final_w is not None
    in_specs = [
        pl.BlockSpec((1, FFN_TM, D), lambda b, t: (b, t, 0)),
        pl.BlockSpec((1, 3, D), lambda b, t: (b, 0, 0)),
        _resident(wup.shape),
        _resident(wdn.shape),
    ]
    args = [x, mod, wup, wdn]
    if final:
        in_specs.append(_resident((1, D)))
        args.append(final_w.reshape(1, D))
    return pl.pallas_call(
        functools.partial(_ffn_kernel, n_chunks=n_chunks, final=final),
        grid=(B, S // FFN_TM),
        in_specs=in_specs,
        out_specs=pl.BlockSpec((1, FFN_TM, D), lambda b, t: (b, t, 0)),
        out_shape=jax.ShapeDtypeStruct(x.shape, F32),
        compiler_params=_params("parallel", "parallel"),
        name="ffn_final" if final else "ffn",
    )(*args)


def _inproj_kernel(x_ref, mod_ref, w_ref, rq_ref, rk_ref, rv_ref, rg_ref, dq_ref, dk_ref, dv_ref,
                   gr_ref, gd_ref):
    x = x_ref[0]
    h = _modulated_norm(x, mod_ref[0]).astype(BF16)

    def proj(c):
        return jnp.dot(h, w_ref[:, c * D_MODEL:(c + 1) * D_MODEL], preferred_element_type=F32)

    u = proj(0).astype(BF16)
    for hh in range(RET_HEADS):
        rq_ref[0, hh] = u[:, hh * RET_QK_DIM:(hh + 1) * RET_QK_DIM]
        rk_ref[0, hh] = u[:, (RET_HEADS + hh) * RET_QK_DIM:(RET_HEADS + hh + 1) * RET_QK_DIM]
    u = proj(1).astype(BF16)
    for hh in range(RET_HEADS):
        rv_ref[0, hh] = u[:, hh * RET_V_DIM:(hh + 1) * RET_V_DIM]
    rg_ref[0] = proj(2).astype(BF16)
    u = (proj(3) * (DIFF_HEAD_DIM ** -0.5 * LOG2E)).astype(BF16)
    for hh in range(DIFF_HEADS):
        dq_ref[0, hh] = u[:, hh * DIFF_V_DIM:(hh + 1) * DIFF_V_DIM]
    u = proj(4).astype(BF16)
    for hh in range(DIFF_HEADS):
        dk_ref[0, hh] = u[:, hh * DIFF_V_DIM:(hh + 1) * DIFF_V_DIM]
    u = proj(5).astype(BF16)
    for hh in range(DIFF_HEADS):
        dv_ref[0, hh] = u[:, hh * DIFF_V_DIM:(hh + 1) * DIFF_V_DIM]
    gr_ref[0] = proj(6).astype(BF16)
    gd_ref[0] = proj(7).astype(BF16)


def _inproj(x, mod, w_in):
    B, S, D = x.shape
    TM = FFN_TM
    head_spec = lambda nh, w: pl.BlockSpec((1, nh, TM, w), lambda b, t: (b, 0, t, 0))
    tok_spec = pl.BlockSpec((1, TM, D), lambda b, t: (b, t, 0))
    sds = jax.ShapeDtypeStruct
    return pl.pallas_call(
        _inproj_kernel,
        grid=(B, S // TM),
        in_specs=[tok_spec, pl.BlockSpec((1, 3, D), lambda b, t: (b, 0, 0)), _resident(w_in.shape)],
        out_specs=[
            head_spec(RET_HEADS, RET_QK_DIM), head_spec(RET_HEADS, RET_QK_DIM),
            head_spec(RET_HEADS, RET_V_DIM), tok_spec,
            head_spec(DIFF_HEADS, DIFF_V_DIM), head_spec(DIFF_HEADS, DIFF_V_DIM),
            head_spec(DIFF_HEADS, DIFF_V_DIM), tok_spec, tok_spec,
        ],
        out_shape=[
            sds((B, RET_HEADS, S, RET_QK_DIM), BF16), sds((B, RET_HEADS, S, RET_QK_DIM), BF16),
            sds((B, RET_HEADS, S, RET_V_DIM), BF16), sds((B, S, D), BF16),
            sds((B, DIFF_HEADS, S, DIFF_V_DIM), BF16), sds((B, DIFF_HEADS, S, DIFF_V_DIM), BF16),
            sds((B, DIFF_HEADS, S, DIFF_V_DIM), BF16), sds((B, S, D), BF16), sds((B, S, D), BF16),
        ],
        compiler_params=_params("parallel", "parallel"),
        name="mixer_inproj",
    )(x, mod, w_in)


def _retention_consts():
    L = RET_L
    gamma = 1.0 - 2.0 ** (-5.0 - jnp.arange(RET_HEADS, dtype=F32))
    log_g = jnp.log(gamma)
    r = jnp.arange(L, dtype=F32)
    scale = RET_QK_DIM ** -0.5
    allowed = (jnp.arange(L)[None, :] // CHUNK) <= (jnp.arange(L)[:, None] // CHUNK)
    intra = jnp.exp(log_g[:, None, None] * jnp.abs(r[:, None] - r[None, :]))
    intra = jnp.where(allowed[None], intra, 0.0) * scale
    q_dec = jnp.exp(log_g[:, None] * r[None, :]) * scale
    k_dec = jnp.exp(log_g[:, None] * (L - r)[None, :])
    blk_dec = jnp.exp(log_g * L)
    q_dec = jnp.broadcast_to(q_dec[:, :, None], (RET_HEADS, L, RET_QK_DIM))
    k_dec = jnp.broadcast_to(k_dec[:, :, None], (RET_HEADS, L, RET_QK_DIM))
    blk_dec = jnp.broadcast_to(blk_dec[:, None, None], (RET_HEADS, 1, RET_V_DIM))
    return intra, q_dec, k_dec, blk_dec


def _retention_kernel(q_ref, k_ref, v_ref, g_ref, w_ref, qd_ref, kd_ref, bd_ref, gn_ref, o_ref, state_ref):
    t = pl.program_id(2)

    @pl.when(t == 0)
    def _():
        state_ref[...] = jnp.zeros_like(state_ref)

    q = q_ref[0, 0]
    k = k_ref[0, 0]
    v = v_ref[0, 0]
    s = lax.dot_general(q, k, (((1,), (1,)), ((), ())), preferred_element_type=F32)
    p = (s * w_ref[0]).astype(BF16)
    y = jnp.dot(p, v, preferred_element_type=F32)
    state = state_ref[...]
    qd = (q.astype(F32) * qd_ref[0]).astype(BF16)
    y = y + jnp.dot(qd, state.astype(BF16), preferred_element_type=F32)
    kd_t = (k.astype(F32) * kd_ref[0]).T.astype(BF16)
    state_ref[...] = state * bd_ref[0] + jnp.dot(kd_t, v, preferred_element_type=F32)
    ms = jnp.mean(y * y, axis=-1, keepdims=True)
    y = y * lax.rsqrt(ms + NORM_EPS) * gn_ref[0]
    g = g_ref[0].astype(F32)
    o_ref[0] = (y * (0.5 * g) * (1.0 + jnp.tanh(0.5 * g))).astype(BF16)


def _retention(rq, rk, rv, rg, ret_gn_l):
    B, H, S, dk = rq.shape
    dv = rv.shape[-1]
    L = RET_L
    intra, q_dec, k_dec, blk_dec = _retention_consts()
    blk = lambda w: pl.BlockSpec((1, 1, L, w), lambda b, h, t: (b, h, t, 0))
    per_head = lambda shape: pl.BlockSpec((1,) + shape, lambda b, h, t: (h, 0, 0))
    return pl.pallas_call(
        _retention_kernel,
        grid=(B, H, S // L),
        in_specs=[
            blk(dk), blk(dk), blk(dv),
            pl.BlockSpec((1, L, dv), lambda b, h, t: (b, t, h)),
            per_head((L, L)), per_head((L, dk)), per_head((L, dk)), per_head((1, dv)), per_head((1, dv)),
        ],
        out_specs=pl.BlockSpec((1, L, dv), lambda b, h, t: (b, t, h)),
        out_shape=jax.ShapeDtypeStruct((B, S, H * dv), BF16),
        scratch_shapes=[pltpu.VMEM((dk, dv), F32)],
        compiler_params=_params("parallel", "parallel", "arbitrary"),
        name="retention",
    )(rq, rk, rv, rg, intra, q_dec, k_dec, blk_dec, ret_gn_l.reshape(H, 1, dv))


def _attn_bias_tables(n_heads):
    T = ATT_TQ
    slopes = 2.0 ** (-8.0 * jnp.arange(1, n_heads + 1, dtype=F32) / n_heads)
    r = jnp.arange(T, dtype=F32)
    a, c = r[:, None], r[None, :]
    allowed = (jnp.arange(T)[:, None] // CHUNK) <= (jnp.arange(T)[None, :] // CHUNK)
    diag = slopes[:, None, None] * (c - jnp.abs(c - a))[None]
    diag = jnp.where(allowed[None], diag, NEG_BIG)
    diag = jnp.concatenate([diag, diag], axis=-1)
    off = jnp.broadcast_to((slopes[:, None] * r[None, :])[:, :, None], (n_heads, T, 128))
    return slopes * LOG2E, diag * LOG2E, off * LOG2E


def _diffattn_kernel(slope_ref, q_ref, k_ref, v_ref, db_ref, ob_ref, lam_ref, sub_ref, o_ref,
                     vt_ref, qbd_ref, s_ref, p_ref, bmax_ref, alpha_ref, m_ref, acc_ref, *, lam_init):
    h = pl.program_id(1)
    i = pl.program_id(2)
    TQ, TK = ATT_TQ, ATT_TK
    n_kv = vt_ref.shape[0]

    @pl.when(i == 0)
    def _():
        for c in range(n_kv):
            vt_ref[c, 0:DIFF_V_DIM] = v_ref[0, 0, c * TK:(c + 1) * TK, :].astype(F32).T.astype(BF16)
            ones_row = lax.broadcasted_iota(jnp.int32, (ATT_SUM_ROWS, TK), 0) == 0
            vt_ref[c, DIFF_V_DIM:DIFF_V_DIM + ATT_SUM_ROWS] = ones_row.astype(F32).astype(BF16)

    slope = slope_ref[h]
    q = q_ref[0, 0]
    lane = lax.broadcasted_iota(jnp.int32, q.shape, 1)
    zero = jnp.zeros_like(q)
    qbd_ref[0:TQ] = jnp.where(lane < DIFF_HEAD_DIM, q, zero)
    qbd_ref[TQ:2 * TQ] = jnp.where(lane >= DIFF_HEAD_DIM, q, zero)
    m_ref[...] = jnp.full(m_ref.shape, NEG_BIG, F32)
    acc_ref[...] = jnp.zeros(acc_ref.shape, F32)

    def block_of(t):
        return jnp.maximum(i - t, 0)

    def stage_scores(t, slot, diagonal):
        k = k_ref[0, 0, pl.ds(pl.multiple_of(block_of(t) * TK, TK), TK), :]
        s = lax.dot_general(k, qbd_ref[...], (((1,), (1,)), ((), ())), preferred_element_type=F32)
        if diagonal:
            u = s + db_ref[0]
        else:
            u = s + jnp.concatenate([ob_ref[0]] * (2 * TQ // 128), axis=1)
        s_ref[slot] = u
        bmax_ref[slot] = jnp.max(u, axis=0, keepdims=True)

    def stage_softmax(t, slot, diagonal):
        if diagonal:
            shift = 0.0
        else:
            shift = jnp.where(t <= i, -slope * (t * TK).astype(F32), NEG_BIG)
        m = m_ref[...]
        m_new = jnp.maximum(m, bmax_ref[slot] + shift)
        alpha = jnp.exp2(m - m_new)
        p = jnp.exp2(s_ref[slot] - (m_new - shift))
        m_ref[...] = m_new
        alpha_ref[slot] = alpha
        p_ref[slot] = p.astype(BF16)

    def stage_pv(t, slot):
        pv = jnp.dot(vt_ref[block_of(t)], p_ref[slot], preferred_element_type=F32)
        acc_ref[...] = alpha_ref[slot] * acc_ref[...] + pv

    stage_scores(0, 0, True)
    stage_softmax(0, 0, True)
    stage_scores(1, 1, False)

    def body(u, carry):
        t = 2 * u
        stage_pv(t, 0)
        stage_softmax(t + 1, 1, False)
        stage_scores(t + 2, 0, False)
        stage_pv(t + 1, 1)
        stage_softmax(t + 2, 0, False)
        stage_scores(t + 3, 1, False)
        return carry

    lax.fori_loop(0, (i + 2) // 2, body, 0)
    o = acc_ref[0:DIFF_V_DIM] / acc_ref[DIFF_V_DIM:DIFF_V_DIM + 1]
    lp = lam_ref[...]
    lam = (jnp.exp(jnp.sum(lp[0:1] * lp[1:2], axis=-1, keepdims=True))
           - jnp.exp(jnp.sum(lp[2:3] * lp[3:4], axis=-1, keepdims=True)) + lam_init)
    y = (o[:, :TQ] - lam * o[:, TQ:]).T
    ms = jnp.mean(y * y, axis=-1, keepdims=True)
    y = y * lax.rsqrt(ms + NORM_EPS) * sub_ref[...] * (1.0 - lam_init)
    o_ref[0] = y.astype(BF16)


def _diffattn(dq, dk, dv, lam_params, subln, layer_idx):
    B, H, S, w = dq.shape
    lam_init = 0.8 - 0.6 * math.exp(-0.3 * layer_idx)
    slopes, diag_bias, off_bias = _attn_bias_tables(H)
    grid_spec = pltpu.PrefetchScalarGridSpec(
        num_scalar_prefetch=1,
        grid=(B, H, S // ATT_TQ),
        in_specs=[
            pl.BlockSpec((1, 1, ATT_TQ, w), lambda b, h, i, sl: (b, h, i, 0)),
            pl.BlockSpec((1, 1, S, w), lambda b, h, i, sl: (b, h, 0, 0)),
            pl.BlockSpec((1, 1, S, w), lambda b, h, i, sl: (b, h, 0, 0)),
            pl.BlockSpec((1, ATT_TK, 2 * ATT_TQ), lambda b, h, i, sl: (h, 0, 0)),
            pl.BlockSpec((1, ATT_TK, 128), lambda b, h, i, sl: (h, 0, 0)),
            pl.BlockSpec((4, DIFF_HEAD_DIM), lambda b, h, i, sl: (0, 0)),
            pl.BlockSpec((1, w), lambda b, h, i, sl: (0, 0)),
        ],
        out_specs=pl.BlockSpec((1, ATT_TQ, w), lambda b, h, i, sl: (b, i, h)),
        scratch_shapes=[
            pltpu.VMEM((S // ATT_TK, w + ATT_SUM_ROWS, ATT_TK), BF16),
            pltpu.VMEM((2 * ATT_TQ, w), BF16),
            pltpu.VMEM((2, ATT_TK, 2 * ATT_TQ), F32),
            pltpu.VMEM((2, ATT_TK, 2 * ATT_TQ), BF16),
            pltpu.VMEM((2, 1, 2 * ATT_TQ), F32),
            pltpu.VMEM((2, 1, 2 * ATT_TQ), F32),
            pltpu.VMEM((1, 2 * ATT_TQ), F32),
            pltpu.VMEM((w + ATT_SUM_ROWS, 2 * ATT_TQ), F32),
        ],
    )
    return pl.pallas_call(
        functools.partial(_diffattn_kernel, lam_init=lam_init),
        grid_spec=grid_spec,
        out_shape=jax.ShapeDtypeStruct((B, S, H * w), BF16),
        compiler_params=_params("parallel", "parallel", "arbitrary"),
        name="diff_attention",
    )(slopes, dq, dk, dv, diag_bias, off_bias, lam_params, subln.reshape(1, w))


def _merge_kernel(x_ref, mod_ref, yr_ref, yd_ref, gr_ref, gd_ref, wr_ref, wd_ref, wo_ref, o_ref):
    def sigmoid(g):
        return 0.5 * (1.0 + jnp.tanh(0.5 * g))

    br = jnp.dot(yr_ref[0], wr_ref[...], preferred_element_type=F32)
    bd = jnp.dot(yd_ref[0], wd_ref[...], preferred_element_type=F32)
    merged = sigmoid(gr_ref[0].astype(F32)) * br + sigmoid(gd_ref[0].astype(F32)) * bd
    out = jnp.dot(merged.astype(BF16), wo_ref[...], preferred_element_type=F32)
    o_ref[0] = x_ref[0] + mod_ref[0][2:3] * out


def _merge(x, mod, y_ret, y_diff, g_ret, g_diff, w_rb, w_db, w_o):
    B, S, D = x.shape
    tok = pl.BlockSpec((1, FFN_TM, D), lambda b, t: (b, t, 0))
    return pl.pallas_call(
        _merge_kernel,
        grid=(B, S // FFN_TM),
        in_specs=[tok, pl.BlockSpec((1, 3, D), lambda b, t: (b, 0, 0)), tok, tok, tok, tok,
                  _resident(w_rb.shape), _resident(w_db.shape), _resident(w_o.shape)],
        out_specs=tok,
        out_shape=jax.ShapeDtypeStruct(x.shape, F32),
        compiler_params=_params("parallel", "parallel"),
        name="mixer_merge",
    )(x, mod, y_ret, y_diff, g_ret, g_diff, w_rb, w_db, w_o)


def _prep_ffn_weights(w_up, w_down):
    n = D_FF // FFN_FC
    up = w_up.astype(BF16).reshape(D_MODEL, 2, n, FFN_FC).transpose(2, 0, 1, 3).reshape(n, D_MODEL, 2 * FFN_FC)
    down = w_down.astype(BF16).reshape(n, FFN_FC, D_MODEL)
    return up, down


def kernel(x, c, w_ada, b_ada, norm_w, w_ffn_up, w_ffn_down, w_in, ret_gn, lambda_q1, lambda_k1, lambda_q2,
           lambda_k2, diff_subln, w_ret_branch, w_diff_branch, w_out, final_norm):
    B, S, D = x.shape
    assert D == D_MODEL and S % max(FFN_TM, RET_L, ATT_TQ) == 0 and ATT_TQ == ATT_TK
    mod_all = _adaln(c, w_ada, b_ada, norm_w).reshape(DEPTH, B, N_SUB, 3, D)
    for l in range(DEPTH):
        mod = [mod_all[l, :, s] for s in range(N_SUB)]
        up, down = _prep_ffn_weights(w_ffn_up[l, 0], w_ffn_down[l, 0])
        x = _ffn(x, mod[0], up, down)
        rq, rk, rv, rg, dq, dk, dv, g_ret, g_diff = _inproj(x, mod[1], w_in[l].astype(BF16))
        y_ret = _retention(rq, rk, rv, rg, ret_gn[l])
        lam_params = jnp.stack([lambda_q1[l], lambda_k1[l], lambda_q2[l], lambda_k2[l]])
        y_diff = _diffattn(dq, dk, dv, lam_params, diff_subln[l], l)
        x = _merge(x, mod[1], y_ret, y_diff, g_ret, g_diff, w_ret_branch[l].astype(BF16),
                   w_diff_branch[l].astype(BF16), w_out[l].astype(BF16))
        up, down = _prep_ffn_weights(w_ffn_up[l, 1], w_ffn_down[l, 1])
        x = _ffn(x, mod[2], up, down, final_w=final_norm if l == DEPTH - 1 else None)
    return x
```

```python
import functools
import math

import jax
import jax.numpy as jnp
from jax import lax
from jax.experimental import pallas as pl
from jax.experimental.pallas import tpu as pltpu

DEPTH = 4
D_MODEL = 1024
CHUNK = 64
CHUNK_SHIFT = 6
RET_HEADS = 4
RET_QK_DIM = 128
RET_V_DIM = 256
DIFF_HEADS = 8
DIFF_HEAD_DIM = 64
DIFF_V_DIM = 2 * DIFF_HEAD_DIM
D_FF = 2816
N_SUB = 3
NORM_EPS = 1e-6
IN_WIDTH = 8 * D_MODEL

F32 = jnp.float32
BF16 = jnp.bfloat16

V7X_VMEM_LIMIT_BYTES = 56 * 1024 * 1024

FFN_TM = 512
FFN_FC = 256
RET_L = 256
ATT_TQ = 512
ATT_TK = 512
ATT_SUM_ROWS = 16
ATT_EXP2_ZERO = 160.0
NEG_BIG = -1e30
LOG2E = math.log2(math.e)


def _resident(shape):
    nd = len(shape)
    return pl.BlockSpec(shape, lambda *_: (0,) * nd, pipeline_mode=pl.Buffered(1))


def _params(*sem, flags=None):
    return pltpu.CompilerParams(dimension_semantics=sem, vmem_limit_bytes=V7X_VMEM_LIMIT_BYTES, flags=flags)


def _adaln_kernel(c_ref, w_ref, b_ref, nw_ref, o_ref):
    j = pl.program_id(1)
    c = c_ref[...]
    cond = c / (1.0 + jnp.exp(-c))
    r = jnp.dot(cond, w_ref[0], preferred_element_type=F32,
                precision=lax.Precision.HIGHEST) + b_ref[0]
    kind = j % 3
    sub = j // 3
    r = jnp.where(kind == 1, (1.0 + r) * nw_ref[0, 0], r)
    r = jnp.where(jnp.logical_and(kind == 2, sub != 1), 0.5 * r, r)
    o_ref[0] = r


def _adaln(c, w_ada, b_ada, norm_w):
    B = c.shape[0]
    n_tiles = N_SUB * 3
    return pl.pallas_call(
        _adaln_kernel,
        grid=(DEPTH, n_tiles),
        in_specs=[
            pl.BlockSpec((B, D_MODEL), lambda l, j: (0, 0)),
            pl.BlockSpec((1, D_MODEL, D_MODEL), lambda l, j: (l, 0, j)),
            pl.BlockSpec((1, 1, D_MODEL), lambda l, j: (l, 0, j)),
            pl.BlockSpec((1, 1, 1, D_MODEL), lambda l, j: (l, j // 3, 0, 0)),
        ],
        out_specs=pl.BlockSpec((1, B, D_MODEL), lambda l, j: (l, 0, j)),
        out_shape=jax.ShapeDtypeStruct((DEPTH, B, n_tiles * D_MODEL), F32),
        compiler_params=_params("arbitrary", "arbitrary"),
        name="adaln_mod",
    )(c, w_ada, b_ada.reshape(DEPTH, 1, n_tiles * D_MODEL), norm_w.reshape(DEPTH, N_SUB, 1, D_MODEL))


def _modulated_norm(x, mod):
    ms = jnp.mean(x * x, axis=-1, keepdims=True)
    return x * lax.rsqrt(ms + NORM_EPS) * mod[1:2] + mod[0:1]


def _ffn_kernel(x_ref, mod_ref, wup_ref, wdn_ref, *rest, n_chunks, final):
    if final:
        fw_ref, o_ref = rest
    else:
        (o_ref,) = rest
    x = x_ref[0]
    mod = mod_ref[0]
    h = _modulated_norm(x, mod).astype(BF16)
    acc = jnp.zeros(x.shape, F32)
    for c in range(n_chunks):
        u = jnp.dot(h, wup_ref[c], preferred_element_type=F32)
        a = u[:, :FFN_FC]
        b = u[:, FFN_FC:]
        act = (0.5 * a) * (1.0 + jnp.tanh(0.5 * a)) * b
        acc = acc + jnp.dot(act.astype(BF16), wdn_ref[c], preferred_element_type=F32)
    y = x + mod[2:3] * acc
    if final:
        ms = jnp.mean(y * y, axis=-1, keepdims=True)
        y = y * lax.rsqrt(ms + NORM_EPS) * fw_ref[...]
    o_ref[0] = y


def _ffn(x, mod, wup, wdn, final_w=None):
    B, S, D = x.shape
    n_chunks = wup.shape[0]
    final = final_w is not None
    in_specs = [
        pl.BlockSpec((1, FFN_TM, D), lambda b, t: (b, t, 0)),
        pl.BlockSpec((1, 3, D), lambda b, t: (b, 0, 0)),
        _resident(wup.shape),
        _resident(wdn.shape),
    ]
    args = [x, mod, wup, wdn]
    if final:
        in_specs.append(_resident((1, D)))
        args.append(final_w.reshape(1, D))
    return pl.pallas_call(
        functools.partial(_ffn_kernel, n_chunks=n_chunks, final=final),
        grid=(B, S // FFN_TM),
        in_specs=in_specs,
        out_specs=pl.BlockSpec((1, FFN_TM, D), lambda b, t: (b, t, 0)),
        out_shape=jax.ShapeDtypeStruct(x.shape, F32),
        compiler_params=_params("parallel", "parallel"),
        name="ffn_final" if final else "ffn",
    )(*args)


def _inproj_kernel(x_ref, mod_ref, w_ref, rq_ref, rk_ref, rv_ref, rg_ref, dq_ref, dk_ref, dv_ref,
                   gr_ref, gd_ref):
    x = x_ref[0]
    h = _modulated_norm(x, mod_ref[0]).astype(BF16)

    def proj(c):
        return jnp.dot(h, w_ref[:, c * D_MODEL:(c + 1) * D_MODEL], preferred_element_type=F32)

    u = proj(0).astype(BF16)
    for hh in range(RET_HEADS):
        rq_ref[0, hh] = u[:, hh * RET_QK_DIM:(hh + 1) * RET_QK_DIM]
        rk_ref[0, hh] = u[:, (RET_HEADS + hh) * RET_QK_DIM:(RET_HEADS + hh + 1) * RET_QK_DIM]
    u = proj(1).astype(BF16)
    for hh in range(RET_HEADS):
        rv_ref[0, hh] = u[:, hh * RET_V_DIM:(hh + 1) * RET_V_DIM]
    rg_ref[0] = proj(2).astype(BF16)
    u = (proj(3) * (DIFF_HEAD_DIM ** -0.5 * LOG2E)).astype(BF16)
    for hh in range(DIFF_HEADS):
        dq_ref[0, hh] = u[:, hh * DIFF_V_DIM:(hh + 1) * DIFF_V_DIM]
    u = proj(4).astype(BF16)
    for hh in range(DIFF_HEADS):
        dk_ref[0, hh] = u[:, hh * DIFF_V_DIM:(hh + 1) * DIFF_V_DIM]
    u = proj(5).astype(BF16)
    for hh in range(DIFF_HEADS):
        dv_ref[0, hh] = u[:, hh * DIFF_V_DIM:(hh + 1) * DIFF_V_DIM]
    gr_ref[0] = proj(6).astype(BF16)
    gd_ref[0] = proj(7).astype(BF16)


def _inproj(x, mod, w_in):
    B, S, D = x.shape
    TM = FFN_TM
    head_spec = lambda nh, w: pl.BlockSpec((1, nh, TM, w), lambda b, t: (b, 0, t, 0))
    tok_spec = pl.BlockSpec((1, TM, D), lambda b, t: (b, t, 0))
    sds = jax.ShapeDtypeStruct
    return pl.pallas_call(
        _inproj_kernel,
        grid=(B, S // TM),
        in_specs=[tok_spec, pl.BlockSpec((1, 3, D), lambda b, t: (b, 0, 0)), _resident(w_in.shape)],
        out_specs=[
            head_spec(RET_HEADS, RET_QK_DIM), head_spec(RET_HEADS, RET_QK_DIM),
            head_spec(RET_HEADS, RET_V_DIM), tok_spec,
            head_spec(DIFF_HEADS, DIFF_V_DIM), head_spec(DIFF_HEADS, DIFF_V_DIM),
            head_spec(DIFF_HEADS, DIFF_V_DIM), tok_spec, tok_spec,
        ],
        out_shape=[
            sds((B, RET_HEADS, S, RET_QK_DIM), BF16), sds((B, RET_HEADS, S, RET_QK_DIM), BF16),
            sds((B, RET_HEADS, S, RET_V_DIM), BF16), sds((B, S, D), BF16),
            sds((B, DIFF_HEADS, S, DIFF_V_DIM), BF16), sds((B, DIFF_HEADS, S, DIFF_V_DIM), BF16),
            sds((B, DIFF_HEADS, S, DIFF_V_DIM), BF16), sds((B, S, D), BF16), sds((B, S, D), BF16),
        ],
        compiler_params=_params("parallel", "parallel"),
        name="mixer_inproj",
    )(x, mod, w_in)


def _retention_consts():
    L = RET_L
    gamma = 1.0 - 2.0 ** (-5.0 - jnp.arange(RET_HEADS, dtype=F32))
    log_g = jnp.log(gamma)
    r = jnp.arange(L, dtype=F32)
    scale = RET_QK_DIM ** -0.5
    allowed = (jnp.arange(L)[None, :] // CHUNK) <= (jnp.arange(L)[:, None] // CHUNK)
    intra = jnp.exp(log_g[:, None, None] * jnp.abs(r[:, None] - r[None, :]))
    intra = jnp.where(allowed[None], intra, 0.0) * scale
    q_dec = jnp.exp(log_g[:, None] * r[None, :]) * scale
    k_dec = jnp.exp(log_g[:, None] * (L - r)[None, :])
    blk_dec = jnp.exp(log_g * L)
    q_dec = jnp.broadcast_to(q_dec[:, :, None], (RET_HEADS, L, RET_QK_DIM))
    k_dec = jnp.broadcast_to(k_dec[:, :, None], (RET_HEADS, L, RET_QK_DIM))
    blk_dec = jnp.broadcast_to(blk_dec[:, None, None], (RET_HEADS, 1, RET_V_DIM))
    return intra, q_dec, k_dec, blk_dec


def _retention_kernel(q_ref, k_ref, v_ref, g_ref, w_ref, qd_ref, kd_ref, bd_ref, gn_ref, o_ref, state_ref):
    t = pl.program_id(2)

    @pl.when(t == 0)
    def _():
        state_ref[...] = jnp.zeros_like(state_ref)

    q = q_ref[0, 0]
    k = k_ref[0, 0]
    v = v_ref[0, 0]
    s = lax.dot_general(q, k, (((1,), (1,)), ((), ())), preferred_element_type=F32)
    p = (s * w_ref[0]).astype(BF16)
    y = jnp.dot(p, v, preferred_element_type=F32)
    state = state_ref[...]
    qd = (q.astype(F32) * qd_ref[0]).astype(BF16)
    y = y + jnp.dot(qd, state.astype(BF16), preferred_element_type=F32)
    kd_t = (k.astype(F32) * kd_ref[0]).T.astype(BF16)
    state_ref[...] = state * bd_ref[0] + jnp.dot(kd_t, v, preferred_element_type=F32)
    ms = jnp.mean(y * y, axis=-1, keepdims=True)
    y = y * lax.rsqrt(ms + NORM_EPS) * gn_ref[0]
    g = g_ref[0].astype(F32)
    o_ref[0] = (y * (0.5 * g) * (1.0 + jnp.tanh(0.5 * g))).astype(BF16)


def _retention(rq, rk, rv, rg, ret_gn_l):
    B, H, S, dk = rq.shape
    dv = rv.shape[-1]
    L = RET_L
    intra, q_dec, k_dec, blk_dec = _retention_consts()
    blk = lambda w: pl.BlockSpec((1, 1, L, w), lambda b, h, t: (b, h, t, 0))
    per_head = lambda shape: pl.BlockSpec((1,) + shape, lambda b, h, t: (h, 0, 0))
    return pl.pallas_call(
        _retention_kernel,
        grid=(B, H, S // L),
        in_specs=[
            blk(dk), blk(dk), blk(dv),
            pl.BlockSpec((1, L, dv), lambda b, h, t: (b, t, h)),
            per_head((L, L)), per_head((L, dk)), per_head((L, dk)), per_head((1, dv)), per_head((1, dv)),
        ],
        out_specs=pl.BlockSpec((1, L, dv), lambda b, h, t: (b, t, h)),
        out_shape=jax.ShapeDtypeStruct((B, S, H * dv), BF16),
        scratch_shapes=[pltpu.VMEM((dk, dv), F32)],
        compiler_params=_params("parallel", "parallel", "arbitrary"),
        name="retention",
    )(rq, rk, rv, rg, intra, q_dec, k_dec, blk_dec, ret_gn_l.reshape(H, 1, dv))


def _attn_bias_tables(n_heads):
    T = ATT_TQ
    slopes = 2.0 ** (-8.0 * jnp.arange(1, n_heads + 1, dtype=F32) / n_heads)
    r = jnp.arange(T, dtype=F32)
    a, c = r[:, None], r[None, :]
    allowed = (jnp.arange(T)[:, None] // CHUNK) <= (jnp.arange(T)[None, :] // CHUNK)
    diag = slopes[:, None, None] * (c - jnp.abs(c - a))[None]
    diag = jnp.where(allowed[None], diag, NEG_BIG)
    diag = jnp.concatenate([diag, diag], axis=-1)
    off = jnp.broadcast_to((slopes[:, None] * r[None, :])[:, :, None], (n_heads, T, 128))
    return slopes * LOG2E, diag * LOG2E, off * LOG2E


def _diffattn_kernel(slope_ref, q_ref, k_ref, v_ref, db_ref, ob_ref, lam_ref, sub_ref, o_ref,
                     vt_ref, qbd_ref, s_ref, p_ref, bmax_ref, alpha_ref, m_ref, knorm_ref, acc_ref, *, lam_init):
    h = pl.program_id(1)
    i = pl.program_id(2)
    TQ, TK = ATT_TQ, ATT_TK
    n_kv = vt_ref.shape[0]

    def max_sq_norms(x):
        xsq = jnp.square(x.astype(F32))
        first = lax.broadcasted_iota(jnp.int32, xsq.shape, 1) < DIFF_HEAD_DIM
        n1 = jnp.sum(jnp.where(first, xsq, 0.0), axis=1, keepdims=True)
        n2 = jnp.sum(jnp.where(first, 0.0, xsq), axis=1, keepdims=True)
        return jnp.max(n1, axis=0, keepdims=True), jnp.max(n2, axis=0, keepdims=True)

    @pl.when(i == 0)
    def _():
        k1sq = jnp.zeros((1, 1), F32)
        k2sq = jnp.zeros((1, 1), F32)
        for c in range(n_kv):
            vt_ref[c, 0:DIFF_V_DIM] = v_ref[0, 0, c * TK:(c + 1) * TK, :].astype(F32).T.astype(BF16)
            ones_row = lax.broadcasted_iota(jnp.int32, (ATT_SUM_ROWS, TK), 0) == 0
            vt_ref[c, DIFF_V_DIM:DIFF_V_DIM + ATT_SUM_ROWS] = ones_row.astype(F32).astype(BF16)
            c1, c2 = max_sq_norms(k_ref[0, 0, c * TK:(c + 1) * TK, :])
            k1sq, k2sq = jnp.maximum(k1sq, c1), jnp.maximum(k2sq, c2)
        knorm_ref[0:1] = jnp.broadcast_to(k1sq, (1, 128))
        knorm_ref[1:2] = jnp.broadcast_to(k2sq, (1, 128))

    slope = slope_ref[h]
    q = q_ref[0, 0]
    lane = lax.broadcasted_iota(jnp.int32, q.shape, 1)
    zero = jnp.zeros_like(q)
    qbd_ref[0:TQ] = jnp.where(lane < DIFF_HEAD_DIM, q, zero)
    qbd_ref[TQ:2 * TQ] = jnp.where(lane >= DIFF_HEAD_DIM, q, zero)
    m_ref[...] = jnp.full(m_ref.shape, NEG_BIG, F32)
    acc_ref[...] = jnp.zeros(acc_ref.shape, F32)

    def block_of(t):
        return jnp.maximum(i - t, 0)

    def stage_scores(t, slot, diagonal):
        k = k_ref[0, 0, pl.ds(pl.multiple_of(block_of(t) * TK, TK), TK), :]
        s = lax.dot_general(k, qbd_ref[...], (((1,), (1,)), ((), ())), preferred_element_type=F32)
        if diagonal:
            u = s + db_ref[0]
        else:
            u = s + jnp.concatenate([ob_ref[0]] * (2 * TQ // 128), axis=1)
        s_ref[slot] = u
        bmax_ref[slot] = jnp.max(u, axis=0, keepdims=True)

    def stage_softmax(t, slot, diagonal):
        if diagonal:
            shift = 0.0
        else:
            shift = jnp.where(t < n_steps, -slope * (t * TK).astype(F32), NEG_BIG)
        m = m_ref[...]
        m_new = jnp.maximum(m, bmax_ref[slot] + shift)
        alpha = jnp.exp2(m - m_new)
        p = jnp.exp2(s_ref[slot] - (m_new - shift))
        m_ref[...] = m_new
        alpha_ref[slot] = alpha
        p_ref[slot] = p.astype(BF16)

    def stage_pv(t, slot):
        pv = jnp.dot(vt_ref[block_of(t)], p_ref[slot], preferred_element_type=F32)
        acc_ref[...] = alpha_ref[slot] * acc_ref[...] + pv

    stage_scores(0, 0, True)
    stage_softmax(0, 0, True)
    stage_scores(1, 1, False)

    q1sq, q2sq = max_sq_norms(q)
    bound = jnp.sqrt(jnp.maximum(q1sq * knorm_ref[0:1, 0:1], q2sq * knorm_ref[1:2, 0:1])) * 1.001
    m_min = jnp.min(m_ref[...], axis=1, keepdims=True)
    reach = (ATT_EXP2_ZERO + bound - m_min) / (slope * TK)
    n_steps = jnp.minimum(jnp.floor(reach) + 2.0, (i + 1).astype(F32))
    n_steps = jnp.max(n_steps).astype(jnp.int32)

    def body(u, carry):
        t = 2 * u
        stage_pv(t, 0)
        stage_softmax(t + 1, 1, False)
        stage_scores(t + 2, 0, False)
        stage_pv(t + 1, 1)
        stage_softmax(t + 2, 0, False)
        stage_scores(t + 3, 1, False)
        return carry

    lax.fori_loop(0, lax.shift_right_logical(n_steps + 1, 1), body, 0)
    o = acc_ref[0:DIFF_V_DIM] / acc_ref[DIFF_V_DIM:DIFF_V_DIM + 1]
    lp = lam_ref[...]
    lam = (jnp.exp(jnp.sum(lp[0:1] * lp[1:2], axis=-1, keepdims=True))
           - jnp.exp(jnp.sum(lp[2:3] * lp[3:4], axis=-1, keepdims=True)) + lam_init)
    y = (o[:, :TQ] - lam * o[:, TQ:]).T
    ms = jnp.mean(y * y, axis=-1, keepdims=True)
    y = y * lax.rsqrt(ms + NORM_EPS) * sub_ref[...] * (1.0 - lam_init)
    o_ref[0] = y.astype(BF16)


def _diffattn(dq, dk, dv, lam_params, subln, layer_idx):
    B, H, S, w = dq.shape
    lam_init = 0.8 - 0.6 * math.exp(-0.3 * layer_idx)
    slopes, diag_bias, off_bias = _attn_bias_tables(H)
    grid_spec = pltpu.PrefetchScalarGridSpec(
        num_scalar_prefetch=1,
        grid=(B, H, S // ATT_TQ),
        in_specs=[
            pl.BlockSpec((1, 1, ATT_TQ, w), lambda b, h, i, sl: (b, h, i, 0)),
            pl.BlockSpec((1, 1, S, w), lambda b, h, i, sl: (b, h, 0, 0)),
            pl.BlockSpec((1, 1, S, w), lambda b, h, i, sl: (b, h, 0, 0)),
            pl.BlockSpec((1, ATT_TK, 2 * ATT_TQ), lambda b, h, i, sl: (h, 0, 0)),
            pl.BlockSpec((1, ATT_TK, 128), lambda b, h, i, sl: (h, 0, 0)),
            pl.BlockSpec((4, DIFF_HEAD_DIM), lambda b, h, i, sl: (0, 0)),
            pl.BlockSpec((1, w), lambda b, h, i, sl: (0, 0)),
        ],
        out_specs=pl.BlockSpec((1, ATT_TQ, w), lambda b, h, i, sl: (b, i, h)),
        scratch_shapes=[
            pltpu.VMEM((S // ATT_TK, w + ATT_SUM_ROWS, ATT_TK), BF16),
            pltpu.VMEM((2 * ATT_TQ, w), BF16),
            pltpu.VMEM((2, ATT_TK, 2 * ATT_TQ), F32),
            pltpu.VMEM((2, ATT_TK, 2 * ATT_TQ), BF16),
            pltpu.VMEM((2, 1, 2 * ATT_TQ), F32),
            pltpu.VMEM((2, 1, 2 * ATT_TQ), F32),
            pltpu.VMEM((1, 2 * ATT_TQ), F32),
            pltpu.VMEM((8, 128), F32),
            pltpu.VMEM((w + ATT_SUM_ROWS, 2 * ATT_TQ), F32),
        ],
    )
    return pl.pallas_call(
        functools.partial(_diffattn_kernel, lam_init=lam_init),
        grid_spec=grid_spec,
        out_shape=jax.ShapeDtypeStruct((B, S, H * w), BF16),
        compiler_params=_params("parallel", "parallel", "arbitrary"),
        name="diff_attention",
    )(slopes, dq, dk, dv, diag_bias, off_bias, lam_params, subln.reshape(1, w))


def _merge_kernel(x_ref, mod_ref, yr_ref, yd_ref, gr_ref, gd_ref, wr_ref, wd_ref, wo_ref, o_ref):
    def sigmoid(g):
        return 0.5 * (1.0 + jnp.tanh(0.5 * g))

    br = jnp.dot(yr_ref[0], wr_ref[...], preferred_element_type=F32)
    bd = jnp.dot(yd_ref[0], wd_ref[...], preferred_element_type=F32)
    merged = sigmoid(gr_ref[0].astype(F32)) * br + sigmoid(gd_ref[0].astype(F32)) * bd
    out = jnp.dot(merged.astype(BF16), wo_ref[...], preferred_element_type=F32)
    o_ref[0] = x_ref[0] + mod_ref[0][2:3] * out


def _merge(x, mod, y_ret, y_diff, g_ret, g_diff, w_rb, w_db, w_o):
    B, S, D = x.shape
    tok = pl.BlockSpec((1, FFN_TM, D), lambda b, t: (b, t, 0))
    return pl.pallas_call(
        _merge_kernel,
        grid=(B, S // FFN_TM),
        in_specs=[tok, pl.BlockSpec((1, 3, D), lambda b, t: (b, 0, 0)), tok, tok, tok, tok,
                  _resident(w_rb.shape), _resident(w_db.shape), _resident(w_o.shape)],
        out_specs=tok,
        out_shape=jax.ShapeDtypeStruct(x.shape, F32),
        compiler_params=_params("parallel", "parallel"),
        name="mixer_merge",
    )(x, mod, y_ret, y_diff, g_ret, g_diff, w_rb, w_db, w_o)


def _prep_ffn_weights(w_up, w_down):
    n = D_FF // FFN_FC
    up = w_up.astype(BF16).reshape(D_MODEL, 2, n, FFN_FC).transpose(2, 0, 1, 3).reshape(n, D_MODEL, 2 * FFN_FC)
    down = w_down.astype(BF16).reshape(n, FFN_FC, D_MODEL)
    return up, down


def kernel(x, c, w_ada, b_ada, norm_w, w_ffn_up, w_ffn_down, w_in, ret_gn, lambda_q1, lambda_k1, lambda_q2,
           lambda_k2, diff_subln, w_ret_branch, w_diff_branch, w_out, final_norm):
    B, S, D = x.shape
    assert D == D_MODEL and S % max(FFN_TM, RET_L, ATT_TQ) == 0 and ATT_TQ == ATT_TK
    mod_all = _adaln(c, w_ada, b_ada, norm_w).reshape(DEPTH, B, N_SUB, 3, D)
    for l in range(DEPTH):
        mod = [mod_all[l, :, s] for s in range(N_SUB)]
        up, down = _prep_ffn_weights(w_ffn_up[l, 0], w_ffn_down[l, 0])
        x = _ffn(x, mod[0], up, down)
        rq, rk, rv, rg, dq, dk, dv, g_ret, g_diff = _inproj(x, mod[1], w_in[l].astype(BF16))
        y_ret = _retention(rq, rk, rv, rg, ret_gn[l])
        lam_params = jnp.stack([lambda_q1[l], lambda_k1[l], lambda_q2[l], lambda_k2[l]])
        y_diff = _diffattn(dq, dk, dv, lam_params, diff_subln[l], l)
        x = _merge(x, mod[1], y_ret, y_diff, g_ret, g_diff, w_ret_branch[l].astype(BF16),
                   w_diff_branch[l].astype(BF16), w_out[l].astype(BF16))
        up, down = _prep_ffn_weights(w_ffn_up[l, 1], w_ffn_down[l, 1])
        x = _ffn(x, mod[2], up, down, final_w=final_norm if l == DEPTH - 1 else None)
    return x
```

```python
import functools
import math

import jax
import jax.numpy as jnp
from jax import lax
from jax.experimental import pallas as pl
from jax.experimental.pallas import tpu as pltpu

DEPTH = 4
D_MODEL = 1024
CHUNK = 64
CHUNK_SHIFT = 6
RET_HEADS = 4
RET_QK_DIM = 128
RET_V_DIM = 256
DIFF_HEADS = 8
DIFF_HEAD_DIM = 64
DIFF_V_DIM = 2 * DIFF_HEAD_DIM
D_FF = 2816
N_SUB = 3
NORM_EPS = 1e-6
IN_WIDTH = 8 * D_MODEL

F32 = jnp.float32
BF16 = jnp.bfloat16

V7X_VMEM_LIMIT_BYTES = 56 * 1024 * 1024

FFN_TM = 512
FFN_FC = 256
RET_L = 256
ATT_TQ = 512
ATT_TK = 512
ATT_EXP_ROWS = 32
ATT_TILES = 2
ATT_SUM_ROWS = 16
ATT_EXP2_ZERO = 160.0
NEG_BIG = -1e30
LOG2E = math.log2(math.e)


def _resident(shape):
    nd = len(shape)
    return pl.BlockSpec(shape, lambda *_: (0,) * nd, pipeline_mode=pl.Buffered(1))


def _params(*sem, flags=None):
    return pltpu.CompilerParams(dimension_semantics=sem, vmem_limit_bytes=V7X_VMEM_LIMIT_BYTES, flags=flags)


def _adaln_kernel(c_ref, w_ref, b_ref, nw_ref, o_ref):
    j = pl.program_id(1)
    c = c_ref[...]
    cond = c / (1.0 + jnp.exp(-c))
    r = jnp.dot(cond, w_ref[0], preferred_element_type=F32,
                precision=lax.Precision.HIGHEST) + b_ref[0]
    kind = j % 3
    sub = j // 3
    r = jnp.where(kind == 1, (1.0 + r) * nw_ref[0, 0], r)
    r = jnp.where(jnp.logical_and(kind == 2, sub != 1), 0.5 * r, r)
    o_ref[0] = r


def _adaln(c, w_ada, b_ada, norm_w):
    B = c.shape[0]
    n_tiles = N_SUB * 3
    return pl.pallas_call(
        _adaln_kernel,
        grid=(DEPTH, n_tiles),
        in_specs=[
            pl.BlockSpec((B, D_MODEL), lambda l, j: (0, 0)),
            pl.BlockSpec((1, D_MODEL, D_MODEL), lambda l, j: (l, 0, j)),
            pl.BlockSpec((1, 1, D_MODEL), lambda l, j: (l, 0, j)),
            pl.BlockSpec((1, 1, 1, D_MODEL), lambda l, j: (l, j // 3, 0, 0)),
        ],
        out_specs=pl.BlockSpec((1, B, D_MODEL), lambda l, j: (l, 0, j)),
        out_shape=jax.ShapeDtypeStruct((DEPTH, B, n_tiles * D_MODEL), F32),
        compiler_params=_params("arbitrary", "arbitrary"),
        name="adaln_mod",
    )(c, w_ada, b_ada.reshape(DEPTH, 1, n_tiles * D_MODEL), norm_w.reshape(DEPTH, N_SUB, 1, D_MODEL))


def _modulated_norm(x, mod):
    ms = jnp.mean(x * x, axis=-1, keepdims=True)
    return x * lax.rsqrt(ms + NORM_EPS) * mod[1:2] + mod[0:1]


def _ffn_kernel(x_ref, mod_ref, wup_ref, wdn_ref, *rest, n_chunks, final):
    if final:
        fw_ref, o_ref = rest
    else:
        (o_ref,) = rest
    x = x_ref[0]
    mod = mod_ref[0]
    h = _modulated_norm(x, mod).astype(BF16)
    acc = jnp.zeros(x.shape, F32)
    for c in range(n_chunks):
        u = jnp.dot(h, wup_ref[c], preferred_element_type=F32)
        a = u[:, :FFN_FC]
        b = u[:, FFN_FC:]
        act = (0.5 * a) * (1.0 + jnp.tanh(0.5 * a)) * b
        acc = acc + jnp.dot(act.astype(BF16), wdn_ref[c], preferred_element_type=F32)
    y = x + mod[2:3] * acc
    if final:
        ms = jnp.mean(y * y, axis=-1, keepdims=True)
        y = y * lax.rsqrt(ms + NORM_EPS) * fw_ref[...]
    o_ref[0] = y


def _ffn(x, mod, wup, wdn, final_w=None):
    B, S, D = x.shape
    n_chunks = wup.shape[0]
    final = final_w is not None
    in_specs = [
        pl.BlockSpec((1, FFN_TM, D), lambda b, t: (b, t, 0)),
        pl.BlockSpec((1, 3, D), lambda b, t: (b, 0, 0)),
        _resident(wup.shape),
        _resident(wdn.shape),
    ]
    args = [x, mod, wup, wdn]
    if final:
        in_specs.append(_resident((1, D)))
        args.append(final_w.reshape(1, D))
    return pl.pallas_call(
        functools.partial(_ffn_kernel, n_chunks=n_chunks, final=final),
        grid=(B, S // FFN_TM),
        in_specs=in_specs,
        out_specs=pl.BlockSpec((1, FFN_TM, D), lambda b, t: (b, t, 0)),
        out_shape=jax.ShapeDtypeStruct(x.shape, F32),
        compiler_params=_params("parallel", "parallel"),
        name="ffn_final" if final else "ffn",
    )(*args)


def _inproj_kernel(x_ref, mod_ref, w_ref, rq_ref, rk_ref, rv_ref, rg_ref, dq_ref, dk_ref, dv_ref,
                   gr_ref, gd_ref):
    x = x_ref[0]
    h = _modulated_norm(x, mod_ref[0]).astype(BF16)

    def proj(c):
        return jnp.dot(h, w_ref[:, c * D_MODEL:(c + 1) * D_MODEL], preferred_element_type=F32)

    u = proj(0).astype(BF16)
    for hh in range(RET_HEADS):
        rq_ref[0, hh] = u[:, hh * RET_QK_DIM:(hh + 1) * RET_QK_DIM]
        rk_ref[0, hh] = u[:, (RET_HEADS + hh) * RET_QK_DIM:(RET_HEADS + hh + 1) * RET_QK_DIM]
    u = proj(1).astype(BF16)
    for hh in range(RET_HEADS):
        rv_ref[0, hh] = u[:, hh * RET_V_DIM:(hh + 1) * RET_V_DIM]
    rg_ref[0] = proj(2).astype(BF16)
    u = (proj(3) * (DIFF_HEAD_DIM ** -0.5 * LOG2E)).astype(BF16)
    for hh in range(DIFF_HEADS):
        dq_ref[0, hh] = u[:, hh * DIFF_V_DIM:(hh + 1) * DIFF_V_DIM]
    u = proj(4).astype(BF16)
    for hh in range(DIFF_HEADS):
        dk_ref[0, hh] = u[:, hh * DIFF_V_DIM:(hh + 1) * DIFF_V_DIM]
    u = proj(5).astype(BF16)
    for hh in range(DIFF_HEADS):
        dv_ref[0, hh] = u[:, hh * DIFF_V_DIM:(hh + 1) * DIFF_V_DIM]
    gr_ref[0] = proj(6).astype(BF16)
    gd_ref[0] = proj(7).astype(BF16)


def _inproj(x, mod, w_in):
    B, S, D = x.shape
    TM = FFN_TM
    head_spec = lambda nh, w: pl.BlockSpec((1, nh, TM, w), lambda b, t: (b, 0, t, 0))
    tok_spec = pl.BlockSpec((1, TM, D), lambda b, t: (b, t, 0))
    sds = jax.ShapeDtypeStruct
    return pl.pallas_call(
        _inproj_kernel,
        grid=(B, S // TM),
        in_specs=[tok_spec, pl.BlockSpec((1, 3, D), lambda b, t: (b, 0, 0)), _resident(w_in.shape)],
        out_specs=[
            head_spec(RET_HEADS, RET_QK_DIM), head_spec(RET_HEADS, RET_QK_DIM),
            head_spec(RET_HEADS, RET_V_DIM), tok_spec,
            head_spec(DIFF_HEADS, DIFF_V_DIM), head_spec(DIFF_HEADS, DIFF_V_DIM),
            head_spec(DIFF_HEADS, DIFF_V_DIM), tok_spec, tok_spec,
        ],
        out_shape=[
            sds((B, RET_HEADS, S, RET_QK_DIM), BF16), sds((B, RET_HEADS, S, RET_QK_DIM), BF16),
            sds((B, RET_HEADS, S, RET_V_DIM), BF16), sds((B, S, D), BF16),
            sds((B, DIFF_HEADS, S, DIFF_V_DIM), BF16), sds((B, DIFF_HEADS, S, DIFF_V_DIM), BF16),
            sds((B, DIFF_HEADS, S, DIFF_V_DIM), BF16), sds((B, S, D), BF16), sds((B, S, D), BF16),
        ],
        compiler_params=_params("parallel", "parallel"),
        name="mixer_inproj",
    )(x, mod, w_in)


def _retention_consts():
    L = RET_L
    gamma = 1.0 - 2.0 ** (-5.0 - jnp.arange(RET_HEADS, dtype=F32))
    log_g = jnp.log(gamma)
    r = jnp.arange(L, dtype=F32)
    scale = RET_QK_DIM ** -0.5
    allowed = (jnp.arange(L)[None, :] // CHUNK) <= (jnp.arange(L)[:, None] // CHUNK)
    intra = jnp.exp(log_g[:, None, None] * jnp.abs(r[:, None] - r[None, :]))
    intra = jnp.where(allowed[None], intra, 0.0) * scale
    q_dec = jnp.exp(log_g[:, None] * r[None, :]) * scale
    k_dec = jnp.exp(log_g[:, None] * (L - r)[None, :])
    blk_dec = jnp.exp(log_g * L)
    q_dec = jnp.broadcast_to(q_dec[:, :, None], (RET_HEADS, L, RET_QK_DIM))
    k_dec = jnp.broadcast_to(k_dec[:, :, None], (RET_HEADS, L, RET_QK_DIM))
    blk_dec = jnp.broadcast_to(blk_dec[:, None, None], (RET_HEADS, 1, RET_V_DIM))
    return intra, q_dec, k_dec, blk_dec


def _retention_kernel(q_ref, k_ref, v_ref, g_ref, w_ref, qd_ref, kd_ref, bd_ref, gn_ref, o_ref, state_ref):
    t = pl.program_id(2)

    @pl.when(t == 0)
    def _():
        state_ref[...] = jnp.zeros_like(state_ref)

    q = q_ref[0, 0]
    k = k_ref[0, 0]
    v = v_ref[0, 0]
    s = lax.dot_general(q, k, (((1,), (1,)), ((), ())), preferred_element_type=F32)
    p = (s * w_ref[0]).astype(BF16)
    y = jnp.dot(p, v, preferred_element_type=F32)
    state = state_ref[...]
    qd = (q.astype(F32) * qd_ref[0]).astype(BF16)
    y = y + jnp.dot(qd, state.astype(BF16), preferred_element_type=F32)
    kd_t = (k.astype(F32) * kd_ref[0]).T.astype(BF16)
    state_ref[...] = state * bd_ref[0] + jnp.dot(kd_t, v, preferred_element_type=F32)
    ms = jnp.mean(y * y, axis=-1, keepdims=True)
    y = y * lax.rsqrt(ms + NORM_EPS) * gn_ref[0]
    g = g_ref[0].astype(F32)
    o_ref[0] = (y * (0.5 * g) * (1.0 + jnp.tanh(0.5 * g))).astype(BF16)


def _retention(rq, rk, rv, rg, ret_gn_l):
    B, H, S, dk = rq.shape
    dv = rv.shape[-1]
    L = RET_L
    intra, q_dec, k_dec, blk_dec = _retention_consts()
    blk = lambda w: pl.BlockSpec((1, 1, L, w), lambda b, h, t: (b, h, t, 0))
    per_head = lambda shape: pl.BlockSpec((1,) + shape, lambda b, h, t: (h, 0, 0))
    return pl.pallas_call(
        _retention_kernel,
        grid=(B, H, S // L),
        in_specs=[
            blk(dk), blk(dk), blk(dv),
            pl.BlockSpec((1, L, dv), lambda b, h, t: (b, t, h)),
            per_head((L, L)), per_head((L, dk)), per_head((L, dk)), per_head((1, dv)), per_head((1, dv)),
        ],
        out_specs=pl.BlockSpec((1, L, dv), lambda b, h, t: (b, t, h)),
        out_shape=jax.ShapeDtypeStruct((B, S, H * dv), BF16),
        scratch_shapes=[pltpu.VMEM((dk, dv), F32)],
        compiler_params=_params("parallel", "parallel", "arbitrary"),
        name="retention",
    )(rq, rk, rv, rg, intra, q_dec, k_dec, blk_dec, ret_gn_l.reshape(H, 1, dv))


def _attn_bias_tables(n_heads):
    T = ATT_TQ
    slopes = 2.0 ** (-8.0 * jnp.arange(1, n_heads + 1, dtype=F32) / n_heads)
    r = jnp.arange(T, dtype=F32)
    a, c = r[:, None], r[None, :]
    allowed = (jnp.arange(T)[:, None] // CHUNK) <= (jnp.arange(T)[None, :] // CHUNK)
    diag = slopes[:, None, None] * (c - jnp.abs(c - a))[None]
    diag = jnp.where(allowed[None], diag, NEG_BIG)
    diag = jnp.concatenate([diag, diag], axis=-1)
    off = jnp.broadcast_to((slopes[:, None] * r[None, :])[:, :, None], (n_heads, T, 128))
    return slopes * LOG2E, diag * LOG2E, off * LOG2E


def _diffattn_kernel(slope_ref, q_ref, k_ref, v_ref, db_ref, ob_ref, lam_ref, sub_ref, o_ref,
                     vt_ref, qbd_ref, s_ref, p_ref, bmax_ref, alpha_ref, m_ref, knorm_ref, acc_ref, *, lam_init):
    h = pl.program_id(1)
    g = pl.program_id(2)
    TQ, TK = ATT_TQ, ATT_TK
    n_kv = vt_ref.shape[0]

    def max_sq_norms(x):
        xsq = jnp.square(x.astype(F32))
        first = lax.broadcasted_iota(jnp.int32, xsq.shape, 1) < DIFF_HEAD_DIM
        n1 = jnp.sum(jnp.where(first, xsq, 0.0), axis=1, keepdims=True)
        n2 = jnp.sum(jnp.where(first, 0.0, xsq), axis=1, keepdims=True)
        return jnp.max(n1, axis=0, keepdims=True), jnp.max(n2, axis=0, keepdims=True)

    @pl.when(g == 0)
    def _():
        k1sq = jnp.zeros((1, 1), F32)
        k2sq = jnp.zeros((1, 1), F32)
        for c in range(n_kv):
            vt_ref[c, 0:DIFF_V_DIM] = v_ref[0, 0, c * TK:(c + 1) * TK, :].astype(F32).T.astype(BF16)
            ones_row = lax.broadcasted_iota(jnp.int32, (ATT_SUM_ROWS, TK), 0) == 0
            vt_ref[c, DIFF_V_DIM:DIFF_V_DIM + ATT_SUM_ROWS] = ones_row.astype(F32).astype(BF16)
            c1, c2 = max_sq_norms(k_ref[0, 0, c * TK:(c + 1) * TK, :])
            k1sq, k2sq = jnp.maximum(k1sq, c1), jnp.maximum(k2sq, c2)
        knorm_ref[0:1] = jnp.broadcast_to(k1sq, (1, 128))
        knorm_ref[1:2] = jnp.broadcast_to(k2sq, (1, 128))

    slope = slope_ref[h]
    tiles = range(ATT_TILES)
    tile_idx = [g * ATT_TILES + x for x in tiles]
    for x in tiles:
        q = q_ref[0, 0, x * TQ:(x + 1) * TQ, :]
        lane = lax.broadcasted_iota(jnp.int32, q.shape, 1)
        zero = jnp.zeros_like(q)
        qbd_ref[x, 0:TQ] = jnp.where(lane < DIFF_HEAD_DIM, q, zero)
        qbd_ref[x, TQ:2 * TQ] = jnp.where(lane >= DIFF_HEAD_DIM, q, zero)
        m_ref[x] = jnp.full(m_ref.shape[1:], NEG_BIG, F32)
        acc_ref[x] = jnp.zeros(acc_ref.shape[1:], F32)

    def block_of(x, t):
        return jnp.maximum(tile_idx[x] - t, 0)

    def stage_scores(x, t, slot, diagonal):
        k = k_ref[0, 0, pl.ds(pl.multiple_of(block_of(x, t) * TK, TK), TK), :]
        s = lax.dot_general(k, qbd_ref[x], (((1,), (1,)), ((), ())), preferred_element_type=F32)
        bmax = None
        for r in range(0, TK, ATT_EXP_ROWS):
            rows = slice(r, r + ATT_EXP_ROWS)
            if diagonal:
                u = s[rows] + db_ref[0, rows]
            else:
                u = s[rows] + jnp.concatenate([ob_ref[0, rows]] * (2 * TQ // 128), axis=1)
            s_ref[x, slot, rows] = u
            cmax = jnp.max(u.reshape(ATT_EXP_ROWS // 8, 8, 2 * TQ), axis=0)
            bmax = cmax if bmax is None else jnp.maximum(bmax, cmax)
        bmax_ref[x, slot] = jnp.max(bmax, axis=0, keepdims=True)

    def stage_softmax(x, t, slot, diagonal):
        if diagonal:
            shift = 0.0
        else:
            shift = jnp.where(t < n_steps[x], -slope * (t * TK).astype(F32), NEG_BIG)
        m = m_ref[x]
        m_new = jnp.maximum(m, bmax_ref[x, slot] + shift)
        alpha = jnp.exp2(m - m_new)
        m_ref[x] = m_new
        alpha_ref[x, slot] = alpha
        ref = m_new - shift
        for r in range(0, TK, ATT_EXP_ROWS):
            rows = slice(r, r + ATT_EXP_ROWS)
            p_ref[x, slot, rows] = jnp.exp2(s_ref[x, slot, rows] - ref).astype(BF16)

    def stage_pv(x, t, slot):
        pv = jnp.dot(vt_ref[block_of(x, t)], p_ref[x, slot], preferred_element_type=F32)
        acc_ref[x] = alpha_ref[x, slot] * acc_ref[x] + pv

    for x in tiles:
        stage_scores(x, 0, 0, True)
    for x in tiles:
        stage_softmax(x, 0, 0, True)
    for x in tiles:
        stage_scores(x, 1, 1, False)

    n_steps = []
    for x in tiles:
        q1sq, q2sq = max_sq_norms(q_ref[0, 0, x * TQ:(x + 1) * TQ, :])
        bound = jnp.sqrt(jnp.maximum(q1sq * knorm_ref[0:1, 0:1], q2sq * knorm_ref[1:2, 0:1])) * 1.001
        m_min = jnp.min(m_ref[x], axis=1, keepdims=True)
        reach = (ATT_EXP2_ZERO + bound - m_min) / (slope * TK)
        n = jnp.minimum(jnp.floor(reach) + 2.0, (tile_idx[x] + 1).astype(F32))
        n_steps.append(jnp.max(n).astype(jnp.int32))

    def pipeline_tick(t, slot):
        for x in tiles:
            stage_pv(x, t, slot)
        for x in tiles:
            stage_softmax(x, t + 1, 1 - slot, False)
        for x in tiles:
            stage_scores(x, t + 2, slot, False)

    def body(t, carry):
        @pl.when(jnp.bitwise_and(t, 1) == 0)
        def _():
            pipeline_tick(t, 0)

        @pl.when(jnp.bitwise_and(t, 1) == 1)
        def _():
            pipeline_tick(t, 1)

        return carry

    lax.fori_loop(0, functools.reduce(jnp.maximum, n_steps), body, 0)
    lp = lam_ref[...]
    lam = (jnp.exp(jnp.sum(lp[0:1] * lp[1:2], axis=-1, keepdims=True))
           - jnp.exp(jnp.sum(lp[2:3] * lp[3:4], axis=-1, keepdims=True)) + lam_init)
    for x in tiles:
        o = acc_ref[x, 0:DIFF_V_DIM] / acc_ref[x, DIFF_V_DIM:DIFF_V_DIM + 1]
        y = (o[:, :TQ] - lam * o[:, TQ:]).T
        ms = jnp.mean(y * y, axis=-1, keepdims=True)
        y = y * lax.rsqrt(ms + NORM_EPS) * sub_ref[...] * (1.0 - lam_init)
        o_ref[0, x * TQ:(x + 1) * TQ] = y.astype(BF16)


def _diffattn(dq, dk, dv, lam_params, subln, layer_idx):
    B, H, S, w = dq.shape
    lam_init = 0.8 - 0.6 * math.exp(-0.3 * layer_idx)
    slopes, diag_bias, off_bias = _attn_bias_tables(H)
    NT, TQS = ATT_TILES, ATT_TILES * ATT_TQ
    grid_spec = pltpu.PrefetchScalarGridSpec(
        num_scalar_prefetch=1,
        grid=(B, H, S // TQS),
        in_specs=[
            pl.BlockSpec((1, 1, TQS, w), lambda b, h, i, sl: (b, h, i, 0)),
            pl.BlockSpec((1, 1, S, w), lambda b, h, i, sl: (b, h, 0, 0)),
            pl.BlockSpec((1, 1, S, w), lambda b, h, i, sl: (b, h, 0, 0)),
            pl.BlockSpec((1, ATT_TK, 2 * ATT_TQ), lambda b, h, i, sl: (h, 0, 0)),
            pl.BlockSpec((1, ATT_TK, 128), lambda b, h, i, sl: (h, 0, 0)),
            pl.BlockSpec((4, DIFF_HEAD_DIM), lambda b, h, i, sl: (0, 0)),
            pl.BlockSpec((1, w), lambda b, h, i, sl: (0, 0)),
        ],
        out_specs=pl.BlockSpec((1, TQS, w), lambda b, h, i, sl: (b, i, h)),
        scratch_shapes=[
            pltpu.VMEM((S // ATT_TK, w + ATT_SUM_ROWS, ATT_TK), BF16),
            pltpu.VMEM((NT, 2 * ATT_TQ, w), BF16),
            pltpu.VMEM((NT, 2, ATT_TK, 2 * ATT_TQ), F32),
            pltpu.VMEM((NT, 2, ATT_TK, 2 * ATT_TQ), BF16),
            pltpu.VMEM((NT, 2, 1, 2 * ATT_TQ), F32),
            pltpu.VMEM((NT, 2, 1, 2 * ATT_TQ), F32),
            pltpu.VMEM((NT, 1, 2 * ATT_TQ), F32),
            pltpu.VMEM((8, 128), F32),
            pltpu.VMEM((NT, w + ATT_SUM_ROWS, 2 * ATT_TQ), F32),
        ],
    )
    return pl.pallas_call(
        functools.partial(_diffattn_kernel, lam_init=lam_init),
        grid_spec=grid_spec,
        out_shape=jax.ShapeDtypeStruct((B, S, H * w), BF16),
        compiler_params=_params("parallel", "parallel", "arbitrary"),
        name="diff_attention",
    )(slopes, dq, dk, dv, diag_bias, off_bias, lam_params, subln.reshape(1, w))


def _merge_kernel(x_ref, mod_ref, yr_ref, yd_ref, gr_ref, gd_ref, wr_ref, wd_ref, wo_ref, o_ref):
    def sigmoid(g):
        return 0.5 * (1.0 + jnp.tanh(0.5 * g))

    br = jnp.dot(yr_ref[0], wr_ref[...], preferred_element_type=F32)
    bd = jnp.dot(yd_ref[0], wd_ref[...], preferred_element_type=F32)
    merged = sigmoid(gr_ref[0].astype(F32)) * br + sigmoid(gd_ref[0].astype(F32)) * bd
    out = jnp.dot(merged.astype(BF16), wo_ref[...], preferred_element_type=F32)
    o_ref[0] = x_ref[0] + mod_ref[0][2:3] * out


def _merge(x, mod, y_ret, y_diff, g_ret, g_diff, w_rb, w_db, w_o):
    B, S, D = x.shape
    tok = pl.BlockSpec((1, FFN_TM, D), lambda b, t: (b, t, 0))
    return pl.pallas_call(
        _merge_kernel,
        grid=(B, S // FFN_TM),
        in_specs=[tok, pl.BlockSpec((1, 3, D), lambda b, t: (b, 0, 0)), tok, tok, tok, tok,
                  _resident(w_rb.shape), _resident(w_db.shape), _resident(w_o.shape)],
        out_specs=tok,
        out_shape=jax.ShapeDtypeStruct(x.shape, F32),
        compiler_params=_params("parallel", "parallel"),
        name="mixer_merge",
    )(x, mod, y_ret, y_diff, g_ret, g_diff, w_rb, w_db, w_o)


def _prep_ffn_weights(w_up, w_down):
    n = D_FF // FFN_FC
    up = w_up.astype(BF16).reshape(D_MODEL, 2, n, FFN_FC).transpose(2, 0, 1, 3).reshape(n, D_MODEL, 2 * FFN_FC)
    down = w_down.astype(BF16).reshape(n, FFN_FC, D_MODEL)
    return up, down


def kernel(x, c, w_ada, b_ada, norm_w, w_ffn_up, w_ffn_down, w_in, ret_gn, lambda_q1, lambda_k1, lambda_q2,
           lambda_k2, diff_subln, w_ret_branch, w_diff_branch, w_out, final_norm):
    B, S, D = x.shape
    assert D == D_MODEL and S % max(FFN_TM, RET_L, ATT_TILES * ATT_TQ) == 0 and ATT_TQ == ATT_TK
    mod_all = _adaln(c, w_ada, b_ada, norm_w).reshape(DEPTH, B, N_SUB, 3, D)
    for l in range(DEPTH):
        mod = [mod_all[l, :, s] for s in range(N_SUB)]
        up, down = _prep_ffn_weights(w_ffn_up[l, 0], w_ffn_down[l, 0])
        x = _ffn(x, mod[0], up, down)
        rq, rk, rv, rg, dq, dk, dv, g_ret, g_diff = _inproj(x, mod[1], w_in[l].astype(BF16))
        y_ret = _retention(rq, rk, rv, rg, ret_gn[l])
        lam_params = jnp.stack([lambda_q1[l], lambda_k1[l], lambda_q2[l], lambda_k2[l]])
        y_diff = _diffattn(dq, dk, dv, lam_params, diff_subln[l], l)
        x = _merge(x, mod[1], y_ret, y_diff, g_ret, g_diff, w_ret_branch[l].astype(BF16),
                   w_diff_branch[l].astype(BF16), w_out[l].astype(BF16))
        up, down = _prep_ffn_weights(w_ffn_up[l, 1], w_ffn_down[l, 1])
        x = _ffn(x, mod[2], up, down, final_w=final_norm if l == DEPTH - 1 else None)
    return x
```

```python
import functools
import math

import jax
import jax.numpy as jnp
from jax import lax
from jax.experimental import pallas as pl
from jax.experimental.pallas import tpu as pltpu

DEPTH = 4
D_MODEL = 1024
CHUNK = 64
CHUNK_SHIFT = 6
RET_HEADS = 4
RET_QK_DIM = 128
RET_V_DIM = 256
DIFF_HEADS = 8
DIFF_HEAD_DIM = 64
DIFF_V_DIM = 2 * DIFF_HEAD_DIM
D_FF = 2816
N_SUB = 3
NORM_EPS = 1e-6
IN_WIDTH = 8 * D_MODEL

F32 = jnp.float32
BF16 = jnp.bfloat16

V7X_VMEM_LIMIT_BYTES = 56 * 1024 * 1024

FFN_TM = 512
FFN_FC = 256
RET_L = 256
ATT_TQ = 512
ATT_TK = 512
ATT_EXP_ROWS = 32
ATT_TILES = 2
ATT_SUM_ROWS = 16
ATT_EXP2_ZERO = 160.0
NEG_BIG = -1e30
LOG2E = math.log2(math.e)


def _resident(shape):
    nd = len(shape)
    return pl.BlockSpec(shape, lambda *_: (0,) * nd, pipeline_mode=pl.Buffered(1))


def _params(*sem, flags=None):
    return pltpu.CompilerParams(dimension_semantics=sem, vmem_limit_bytes=V7X_VMEM_LIMIT_BYTES, flags=flags)


def _adaln_kernel(c_ref, w_ref, b_ref, nw_ref, o_ref):
    j = pl.program_id(1)
    c = c_ref[...]
    cond = c / (1.0 + jnp.exp(-c))
    r = jnp.dot(cond, w_ref[0], preferred_element_type=F32,
                precision=lax.Precision.HIGHEST) + b_ref[0]
    kind = j % 3
    sub = j // 3
    r = jnp.where(kind == 1, (1.0 + r) * nw_ref[0, 0], r)
    r = jnp.where(jnp.logical_and(kind == 2, sub != 1), 0.5 * r, r)
    o_ref[0] = r


def _adaln(c, w_ada, b_ada, norm_w):
    B = c.shape[0]
    n_tiles = N_SUB * 3
    return pl.pallas_call(
        _adaln_kernel,
        grid=(DEPTH, n_tiles),
        in_specs=[
            pl.BlockSpec((B, D_MODEL), lambda l, j: (0, 0)),
            pl.BlockSpec((1, D_MODEL, D_MODEL), lambda l, j: (l, 0, j)),
            pl.BlockSpec((1, 1, D_MODEL), lambda l, j: (l, 0, j)),
            pl.BlockSpec((1, 1, 1, D_MODEL), lambda l, j: (l, j // 3, 0, 0)),
        ],
        out_specs=pl.BlockSpec((1, B, D_MODEL), lambda l, j: (l, 0, j)),
        out_shape=jax.ShapeDtypeStruct((DEPTH, B, n_tiles * D_MODEL), F32),
        compiler_params=_params("arbitrary", "arbitrary"),
        name="adaln_mod",
    )(c, w_ada, b_ada.reshape(DEPTH, 1, n_tiles * D_MODEL), norm_w.reshape(DEPTH, N_SUB, 1, D_MODEL))


def _modulated_norm(x, mod):
    ms = jnp.mean(x * x, axis=-1, keepdims=True)
    return x * lax.rsqrt(ms + NORM_EPS) * mod[1:2] + mod[0:1]


def _ffn_kernel(x_ref, mod_ref, wup_ref, wdn_ref, *rest, n_chunks, final):
    if final:
        fw_ref, o_ref = rest
    else:
        (o_ref,) = rest
    x = x_ref[0]
    mod = mod_ref[0]
    h = _modulated_norm(x, mod).astype(BF16)
    acc = jnp.zeros(x.shape, F32)
    for c in range(n_chunks):
        u = jnp.dot(h, wup_ref[c], preferred_element_type=F32)
        a = u[:, :FFN_FC]
        b = u[:, FFN_FC:]
        act = (0.5 * a) * (1.0 + jnp.tanh(0.5 * a)) * b
        acc = acc + jnp.dot(act.astype(BF16), wdn_ref[c], preferred_element_type=F32)
    y = x + mod[2:3] * acc
    if final:
        ms = jnp.mean(y * y, axis=-1, keepdims=True)
        y = y * lax.rsqrt(ms + NORM_EPS) * fw_ref[...]
    o_ref[0] = y


def _ffn(x, mod, wup, wdn, final_w=None):
    B, S, D = x.shape
    n_chunks = wup.shape[0]
    final = final_w is not None
    in_specs = [
        pl.BlockSpec((1, FFN_TM, D), lambda b, t: (b, t, 0)),
        pl.BlockSpec((1, 3, D), lambda b, t: (b, 0, 0)),
        _resident(wup.shape),
        _resident(wdn.shape),
    ]
    args = [x, mod, wup, wdn]
    if final:
        in_specs.append(_resident((1, D)))
        args.append(final_w.reshape(1, D))
    return pl.pallas_call(
        functools.partial(_ffn_kernel, n_chunks=n_chunks, final=final),
        grid=(B, S // FFN_TM),
        in_specs=in_specs,
        out_specs=pl.BlockSpec((1, FFN_TM, D), lambda b, t: (b, t, 0)),
        out_shape=jax.ShapeDtypeStruct(x.shape, F32),
        compiler_params=_params("parallel", "parallel"),
        name="ffn_final" if final else "ffn",
    )(*args)


def _inproj_kernel(x_ref, mod_ref, w_ref, rq_ref, rk_ref, rv_ref, rg_ref, dq_ref, dk_ref, dv_ref,
                   gr_ref, gd_ref):
    x = x_ref[0]
    h = _modulated_norm(x, mod_ref[0]).astype(BF16)

    def proj(c):
        return jnp.dot(h, w_ref[:, c * D_MODEL:(c + 1) * D_MODEL], preferred_element_type=F32)

    u = proj(0).astype(BF16)
    for hh in range(RET_HEADS):
        rq_ref[0, hh] = u[:, hh * RET_QK_DIM:(hh + 1) * RET_QK_DIM]
        rk_ref[0, hh] = u[:, (RET_HEADS + hh) * RET_QK_DIM:(RET_HEADS + hh + 1) * RET_QK_DIM]
    u = proj(1).astype(BF16)
    for hh in range(RET_HEADS):
        rv_ref[0, hh] = u[:, hh * RET_V_DIM:(hh + 1) * RET_V_DIM]
    rg_ref[0] = proj(2).astype(BF16)
    u = (proj(3) * (DIFF_HEAD_DIM ** -0.5 * LOG2E)).astype(BF16)
    for hh in range(DIFF_HEADS):
        dq_ref[0, hh] = u[:, hh * DIFF_V_DIM:(hh + 1) * DIFF_V_DIM]
    u = proj(4).astype(BF16)
    for hh in range(DIFF_HEADS):
        dk_ref[0, hh] = u[:, hh * DIFF_V_DIM:(hh + 1) * DIFF_V_DIM]
    u = proj(5).astype(BF16)
    for hh in range(DIFF_HEADS):
        dv_ref[0, hh] = u[:, hh * DIFF_V_DIM:(hh + 1) * DIFF_V_DIM]
    gr_ref[0] = proj(6).astype(BF16)
    gd_ref[0] = proj(7).astype(BF16)


def _inproj(x, mod, w_in):
    B, S, D = x.shape
    TM = FFN_TM
    head_spec = lambda nh, w: pl.BlockSpec((1, nh, TM, w), lambda b, t: (b, 0, t, 0))
    tok_spec = pl.BlockSpec((1, TM, D), lambda b, t: (b, t, 0))
    sds = jax.ShapeDtypeStruct
    return pl.pallas_call(
        _inproj_kernel,
        grid=(B, S // TM),
        in_specs=[tok_spec, pl.BlockSpec((1, 3, D), lambda b, t: (b, 0, 0)), _resident(w_in.shape)],
        out_specs=[
            head_spec(RET_HEADS, RET_QK_DIM), head_spec(RET_HEADS, RET_QK_DIM),
            head_spec(RET_HEADS, RET_V_DIM), tok_spec,
            head_spec(DIFF_HEADS, DIFF_V_DIM), head_spec(DIFF_HEADS, DIFF_V_DIM),
            head_spec(DIFF_HEADS, DIFF_V_DIM), tok_spec, tok_spec,
        ],
        out_shape=[
            sds((B, RET_HEADS, S, RET_QK_DIM), BF16), sds((B, RET_HEADS, S, RET_QK_DIM), BF16),
            sds((B, RET_HEADS, S, RET_V_DIM), BF16), sds((B, S, D), BF16),
            sds((B, DIFF_HEADS, S, DIFF_V_DIM), BF16), sds((B, DIFF_HEADS, S, DIFF_V_DIM), BF16),
            sds((B, DIFF_HEADS, S, DIFF_V_DIM), BF16), sds((B, S, D), BF16), sds((B, S, D), BF16),
        ],
        compiler_params=_params("parallel", "parallel"),
        name="mixer_inproj",
    )(x, mod, w_in)


def _retention_consts():
    L = RET_L
    gamma = 1.0 - 2.0 ** (-5.0 - jnp.arange(RET_HEADS, dtype=F32))
    log_g = jnp.log(gamma)
    r = jnp.arange(L, dtype=F32)
    scale = RET_QK_DIM ** -0.5
    allowed = (jnp.arange(L)[None, :] // CHUNK) <= (jnp.arange(L)[:, None] // CHUNK)
    intra = jnp.exp(log_g[:, None, None] * jnp.abs(r[:, None] - r[None, :]))
    intra = jnp.where(allowed[None], intra, 0.0) * scale
    q_dec = jnp.exp(log_g[:, None] * r[None, :]) * scale
    k_dec = jnp.exp(log_g[:, None] * (L - r)[None, :])
    blk_dec = jnp.exp(log_g * L)
    q_dec = jnp.broadcast_to(q_dec[:, :, None], (RET_HEADS, L, RET_QK_DIM))
    k_dec = jnp.broadcast_to(k_dec[:, :, None], (RET_HEADS, L, RET_QK_DIM))
    blk_dec = jnp.broadcast_to(blk_dec[:, None, None], (RET_HEADS, 1, RET_V_DIM))
    return intra, q_dec, k_dec, blk_dec


def _retention_kernel(q_ref, k_ref, v_ref, g_ref, w_ref, qd_ref, kd_ref, bd_ref, gn_ref, o_ref, state_ref):
    t = pl.program_id(2)

    @pl.when(t == 0)
    def _():
        state_ref[...] = jnp.zeros_like(state_ref)

    q = q_ref[0, 0]
    k = k_ref[0, 0]
    v = v_ref[0, 0]
    s = lax.dot_general(q, k, (((1,), (1,)), ((), ())), preferred_element_type=F32)
    p = (s * w_ref[0]).astype(BF16)
    y = jnp.dot(p, v, preferred_element_type=F32)
    state = state_ref[...]
    qd = (q.astype(F32) * qd_ref[0]).astype(BF16)
    y = y + jnp.dot(qd, state.astype(BF16), preferred_element_type=F32)
    kd_t = (k.astype(F32) * kd_ref[0]).T.astype(BF16)
    state_ref[...] = state * bd_ref[0] + jnp.dot(kd_t, v, preferred_element_type=F32)
    ms = jnp.mean(y * y, axis=-1, keepdims=True)
    y = y * lax.rsqrt(ms + NORM_EPS) * gn_ref[0]
    g = g_ref[0].astype(F32)
    o_ref[0] = (y * (0.5 * g) * (1.0 + jnp.tanh(0.5 * g))).astype(BF16)


def _retention(rq, rk, rv, rg, ret_gn_l):
    B, H, S, dk = rq.shape
    dv = rv.shape[-1]
    L = RET_L
    intra, q_dec, k_dec, blk_dec = _retention_consts()
    blk = lambda w: pl.BlockSpec((1, 1, L, w), lambda b, h, t: (b, h, t, 0))
    per_head = lambda shape: pl.BlockSpec((1,) + shape, lambda b, h, t: (h, 0, 0))
    return pl.pallas_call(
        _retention_kernel,
        grid=(B, H, S // L),
        in_specs=[
            blk(dk), blk(dk), blk(dv),
            pl.BlockSpec((1, L, dv), lambda b, h, t: (b, t, h)),
            per_head((L, L)), per_head((L, dk)), per_head((L, dk)), per_head((1, dv)), per_head((1, dv)),
        ],
        out_specs=pl.BlockSpec((1, L, dv), lambda b, h, t: (b, t, h)),
        out_shape=jax.ShapeDtypeStruct((B, S, H * dv), BF16),
        scratch_shapes=[pltpu.VMEM((dk, dv), F32)],
        compiler_params=_params("parallel", "parallel", "arbitrary"),
        name="retention",
    )(rq, rk, rv, rg, intra, q_dec, k_dec, blk_dec, ret_gn_l.reshape(H, 1, dv))


def _attn_bias_tables(n_heads):
    T = ATT_TQ
    slopes = 2.0 ** (-8.0 * jnp.arange(1, n_heads + 1, dtype=F32) / n_heads)
    r = jnp.arange(T, dtype=F32)
    a, c = r[:, None], r[None, :]
    allowed = (jnp.arange(T)[:, None] // CHUNK) <= (jnp.arange(T)[None, :] // CHUNK)
    diag = slopes[:, None, None] * (c - jnp.abs(c - a))[None]
    diag = jnp.where(allowed[None], diag, NEG_BIG)
    diag = jnp.concatenate([diag, diag], axis=-1)
    off = jnp.broadcast_to((slopes[:, None] * r[None, :])[:, :, None], (n_heads, T, 128))
    return slopes * LOG2E, diag * LOG2E, off * LOG2E


def _diffattn_kernel(slope_ref, q_ref, k_ref, v_ref, db_ref, ob_ref, lam_ref, sub_ref, o_ref,
                     vt_ref, qbd_ref, s_ref, p_ref, bmax_ref, alpha_ref, m_ref, knorm_ref, acc_ref, *, lam_init):
    h = pl.program_id(1)
    g = pl.program_id(2)
    TQ, TK = ATT_TQ, ATT_TK
    n_kv = vt_ref.shape[0]

    def max_sq_norms(x):
        xsq = jnp.square(x.astype(F32))
        first = lax.broadcasted_iota(jnp.int32, xsq.shape, 1) < DIFF_HEAD_DIM
        n1 = jnp.sum(jnp.where(first, xsq, 0.0), axis=1, keepdims=True)
        n2 = jnp.sum(jnp.where(first, 0.0, xsq), axis=1, keepdims=True)
        return jnp.max(n1, axis=0, keepdims=True), jnp.max(n2, axis=0, keepdims=True)

    @pl.when(g == 0)
    def _():
        k1sq = jnp.zeros((1, 1), F32)
        k2sq = jnp.zeros((1, 1), F32)
        for c in range(n_kv):
            vt_ref[c, 0:DIFF_V_DIM] = v_ref[0, 0, c * TK:(c + 1) * TK, :].astype(F32).T.astype(BF16)
            ones_row = lax.broadcasted_iota(jnp.int32, (ATT_SUM_ROWS, TK), 0) == 0
            vt_ref[c, DIFF_V_DIM:DIFF_V_DIM + ATT_SUM_ROWS] = ones_row.astype(F32).astype(BF16)
            c1, c2 = max_sq_norms(k_ref[0, 0, c * TK:(c + 1) * TK, :])
            k1sq, k2sq = jnp.maximum(k1sq, c1), jnp.maximum(k2sq, c2)
        knorm_ref[0:1] = jnp.broadcast_to(k1sq, (1, 128))
        knorm_ref[1:2] = jnp.broadcast_to(k2sq, (1, 128))

    slope = slope_ref[h]
    tiles = range(ATT_TILES)
    tile_idx = [g * ATT_TILES + x for x in tiles]
    for x in tiles:
        q = q_ref[0, 0, x * TQ:(x + 1) * TQ, :]
        lane = lax.broadcasted_iota(jnp.int32, q.shape, 1)
        zero = jnp.zeros_like(q)
        qbd_ref[x, 0:TQ] = jnp.where(lane < DIFF_HEAD_DIM, q, zero)
        qbd_ref[x, TQ:2 * TQ] = jnp.where(lane >= DIFF_HEAD_DIM, q, zero)
        m_ref[x] = jnp.full(m_ref.shape[1:], NEG_BIG, F32)
        acc_ref[x] = jnp.zeros(acc_ref.shape[1:], F32)

    def block_of(x, t):
        return jnp.maximum(tile_idx[x] - t, 0)

    def stage_scores(x, t, slot, diagonal):
        k = k_ref[0, 0, pl.ds(pl.multiple_of(block_of(x, t) * TK, TK), TK), :]
        s = lax.dot_general(k, qbd_ref[x], (((1,), (1,)), ((), ())), preferred_element_type=F32)
        bmax = None
        for r in range(0, TK, ATT_EXP_ROWS):
            rows = slice(r, r + ATT_EXP_ROWS)
            if diagonal:
                u = s[rows] + db_ref[0, rows]
            else:
                u = s[rows] + jnp.concatenate([ob_ref[0, rows]] * (2 * TQ // 128), axis=1)
            s_ref[x, slot, rows] = u
            cmax = jnp.max(u.reshape(ATT_EXP_ROWS // 8, 8, 2 * TQ), axis=0)
            bmax = cmax if bmax is None else jnp.maximum(bmax, cmax)
        bmax_ref[x, slot] = jnp.max(bmax, axis=0, keepdims=True)

    def stage_softmax(x, t, slot, diagonal):
        if diagonal:
            shift = 0.0
        else:
            shift = jnp.where(t < n_steps[x], -slope * (t * TK).astype(F32), NEG_BIG)
        m = m_ref[x]
        m_new = jnp.maximum(m, bmax_ref[x, slot] + shift)
        alpha = jnp.exp2(m - m_new)
        m_ref[x] = m_new
        alpha_ref[x, slot] = alpha
        ref = m_new - shift
        for r in range(0, TK, ATT_EXP_ROWS):
            rows = slice(r, r + ATT_EXP_ROWS)
            p_ref[x, slot, rows] = jnp.exp2(s_ref[x, slot, rows] - ref).astype(BF16)

    def stage_pv(x, t, slot):
        pv = jnp.dot(vt_ref[block_of(x, t)], p_ref[x, slot], preferred_element_type=F32)
        acc_ref[x] = alpha_ref[x, slot] * acc_ref[x] + pv

    for x in tiles:
        stage_scores(x, 0, 0, True)
    for x in tiles:
        stage_softmax(x, 0, 0, True)
    for x in tiles:
        stage_scores(x, 1, 1, False)

    n_steps = []
    for x in tiles:
        q1sq, q2sq = max_sq_norms(q_ref[0, 0, x * TQ:(x + 1) * TQ, :])
        bound = jnp.sqrt(jnp.maximum(q1sq * knorm_ref[0:1, 0:1], q2sq * knorm_ref[1:2, 0:1])) * 1.001
        m_min = jnp.min(m_ref[x], axis=1, keepdims=True)
        reach = (ATT_EXP2_ZERO + bound - m_min) / (slope * TK)
        n = jnp.minimum(jnp.floor(reach) + 2.0, (tile_idx[x] + 1).astype(F32))
        n_steps.append(jnp.max(n).astype(jnp.int32))

    def pipeline_tick(t, slot):
        for x in tiles:
            stage_pv(x, t, slot)
        for x in tiles:
            stage_softmax(x, t + 1, 1 - slot, False)
        for x in tiles:
            stage_scores(x, t + 2, slot, False)

    def pipeline_drain(t, slot):
        for x in tiles:
            stage_pv(x, t, slot)
        for x in tiles:
            stage_softmax(x, t + 1, 1 - slot, False)
        for x in tiles:
            stage_pv(x, t + 1, 1 - slot)

    def by_parity(fn, t):
        for slot in (0, 1):
            @pl.when(jnp.bitwise_and(t, 1) == slot)
            def _():
                fn(t, slot)

    def body(t, carry):
        by_parity(pipeline_tick, t)
        return carry

    n_ticks = jnp.maximum(functools.reduce(jnp.maximum, n_steps), 2)
    lax.fori_loop(0, n_ticks - 2, body, 0)
    by_parity(pipeline_drain, n_ticks - 2)
    lp = lam_ref[...]
    lam = (jnp.exp(jnp.sum(lp[0:1] * lp[1:2], axis=-1, keepdims=True))
           - jnp.exp(jnp.sum(lp[2:3] * lp[3:4], axis=-1, keepdims=True)) + lam_init)
    for x in tiles:
        o = acc_ref[x, 0:DIFF_V_DIM] / acc_ref[x, DIFF_V_DIM:DIFF_V_DIM + 1]
        y = (o[:, :TQ] - lam * o[:, TQ:]).T
        ms = jnp.mean(y * y, axis=-1, keepdims=True)
        y = y * lax.rsqrt(ms + NORM_EPS) * sub_ref[...] * (1.0 - lam_init)
        o_ref[0, x * TQ:(x + 1) * TQ] = y.astype(BF16)


def _diffattn(dq, dk, dv, lam_params, subln, layer_idx):
    B, H, S, w = dq.shape
    lam_init = 0.8 - 0.6 * math.exp(-0.3 * layer_idx)
    slopes, diag_bias, off_bias = _attn_bias_tables(H)
    NT, TQS = ATT_TILES, ATT_TILES * ATT_TQ
    grid_spec = pltpu.PrefetchScalarGridSpec(
        num_scalar_prefetch=1,
        grid=(B, H, S // TQS),
        in_specs=[
            pl.BlockSpec((1, 1, TQS, w), lambda b, h, i, sl: (b, h, i, 0)),
            pl.BlockSpec((1, 1, S, w), lambda b, h, i, sl: (b, h, 0, 0)),
            pl.BlockSpec((1, 1, S, w), lambda b, h, i, sl: (b, h, 0, 0)),
            pl.BlockSpec((1, ATT_TK, 2 * ATT_TQ), lambda b, h, i, sl: (h, 0, 0)),
            pl.BlockSpec((1, ATT_TK, 128), lambda b, h, i, sl: (h, 0, 0)),
            pl.BlockSpec((4, DIFF_HEAD_DIM), lambda b, h, i, sl: (0, 0)),
            pl.BlockSpec((1, w), lambda b, h, i, sl: (0, 0)),
        ],
        out_specs=pl.BlockSpec((1, TQS, w), lambda b, h, i, sl: (b, i, h)),
        scratch_shapes=[
            pltpu.VMEM((S // ATT_TK, w + ATT_SUM_ROWS, ATT_TK), BF16),
            pltpu.VMEM((NT, 2 * ATT_TQ, w), BF16),
            pltpu.VMEM((NT, 2, ATT_TK, 2 * ATT_TQ), F32),
            pltpu.VMEM((NT, 2, ATT_TK, 2 * ATT_TQ), BF16),
            pltpu.VMEM((NT, 2, 1, 2 * ATT_TQ), F32),
            pltpu.VMEM((NT, 2, 1, 2 * ATT_TQ), F32),
            pltpu.VMEM((NT, 1, 2 * ATT_TQ), F32),
            pltpu.VMEM((8, 128), F32),
            pltpu.VMEM((NT, w + ATT_SUM_ROWS, 2 * ATT_TQ), F32),
        ],
    )
    return pl.pallas_call(
        functools.partial(_diffattn_kernel, lam_init=lam_init),
        grid_spec=grid_spec,
        out_shape=jax.ShapeDtypeStruct((B, S, H * w), BF16),
        compiler_params=_params("parallel", "parallel", "arbitrary"),
        name="diff_attention",
    )(slopes, dq, dk, dv, diag_bias, off_bias, lam_params, subln.reshape(1, w))


def _merge_kernel(x_ref, mod_ref, yr_ref, yd_ref, gr_ref, gd_ref, wr_ref, wd_ref, wo_ref, o_ref):
    def sigmoid(g):
        return 0.5 * (1.0 + jnp.tanh(0.5 * g))

    br = jnp.dot(yr_ref[0], wr_ref[...], preferred_element_type=F32)
    bd = jnp.dot(yd_ref[0], wd_ref[...], preferred_element_type=F32)
    merged = sigmoid(gr_ref[0].astype(F32)) * br + sigmoid(gd_ref[0].astype(F32)) * bd
    out = jnp.dot(merged.astype(BF16), wo_ref[...], preferred_element_type=F32)
    o_ref[0] = x_ref[0] + mod_ref[0][2:3] * out


def _merge(x, mod, y_ret, y_diff, g_ret, g_diff, w_rb, w_db, w_o):
    B, S, D = x.shape
    tok = pl.BlockSpec((1, FFN_TM, D), lambda b, t: (b, t, 0))
    return pl.pallas_call(
        _merge_kernel,
        grid=(B, S // FFN_TM),
        in_specs=[tok, pl.BlockSpec((1, 3, D), lambda b, t: (b, 0, 0)), tok, tok, tok, tok,
                  _resident(w_rb.shape), _resident(w_db.shape), _resident(w_o.shape)],
        out_specs=tok,
        out_shape=jax.ShapeDtypeStruct(x.shape, F32),
        compiler_params=_params("parallel", "parallel"),
        name="mixer_merge",
    )(x, mod, y_ret, y_diff, g_ret, g_diff, w_rb, w_db, w_o)


def _prep_ffn_weights(w_up, w_down):
    n = D_FF // FFN_FC
    up = w_up.astype(BF16).reshape(D_MODEL, 2, n, FFN_FC).transpose(2, 0, 1, 3).reshape(n, D_MODEL, 2 * FFN_FC)
    down = w_down.astype(BF16).reshape(n, FFN_FC, D_MODEL)
    return up, down


def kernel(x, c, w_ada, b_ada, norm_w, w_ffn_up, w_ffn_down, w_in, ret_gn, lambda_q1, lambda_k1, lambda_q2,
           lambda_k2, diff_subln, w_ret_branch, w_diff_branch, w_out, final_norm):
    B, S, D = x.shape
    assert D == D_MODEL and S % max(FFN_TM, RET_L, ATT_TILES * ATT_TQ) == 0 and ATT_TQ == ATT_TK
    mod_all = _adaln(c, w_ada, b_ada, norm_w).reshape(DEPTH, B, N_SUB, 3, D)
    for l in range(DEPTH):
        mod = [mod_all[l, :, s] for s in range(N_SUB)]
        up, down = _prep_ffn_weights(w_ffn_up[l, 0], w_ffn_down[l, 0])
        x = _ffn(x, mod[0], up, down)
        rq, rk, rv, rg, dq, dk, dv, g_ret, g_diff = _inproj(x, mod[1], w_in[l].astype(BF16))
        y_ret = _retention(rq, rk, rv, rg, ret_gn[l])
        lam_params = jnp.stack([lambda_q1[l], lambda_k1[l], lambda_q2[l], lambda_k2[l]])
        y_diff = _diffattn(dq, dk, dv, lam_params, diff_subln[l], l)
        x = _merge(x, mod[1], y_ret, y_diff, g_ret, g_diff, w_ret_branch[l].astype(BF16),
                   w_diff_branch[l].astype(BF16), w_out[l].astype(BF16))
        up, down = _prep_ffn_weights(w_ffn_up[l, 1], w_ffn_down[l, 1])
        x = _ffn(x, mod[2], up, down, final_w=final_norm if l == DEPTH - 1 else None)
    return x
```

```python
import functools
import math

import jax
import jax.numpy as jnp
from jax import lax
from jax.experimental import pallas as pl
from jax.experimental.pallas import tpu as pltpu

DEPTH = 4
D_MODEL = 1024
CHUNK = 64
CHUNK_SHIFT = 6
RET_HEADS = 4
RET_QK_DIM = 128
RET_V_DIM = 256
DIFF_HEADS = 8
DIFF_HEAD_DIM = 64
DIFF_V_DIM = 2 * DIFF_HEAD_DIM
D_FF = 2816
N_SUB = 3
NORM_EPS = 1e-6
IN_WIDTH = 8 * D_MODEL

F32 = jnp.float32
BF16 = jnp.bfloat16

V7X_VMEM_LIMIT_BYTES = 56 * 1024 * 1024

FFN_TM = 512
FFN_FC = 256
RET_L = 256
ATT_TQ = 512
ATT_TK = 512
ATT_EXP_ROWS = 16
ATT_TILES = 2
ATT_SUM_ROWS = 16
ATT_EXP2_ZERO = 160.0
NEG_BIG = -1e30
LOG2E = math.log2(math.e)


def _resident(shape):
    nd = len(shape)
    return pl.BlockSpec(shape, lambda *_: (0,) * nd, pipeline_mode=pl.Buffered(1))


def _params(*sem, flags=None):
    return pltpu.CompilerParams(dimension_semantics=sem, vmem_limit_bytes=V7X_VMEM_LIMIT_BYTES, flags=flags)


def _adaln_kernel(c_ref, w_ref, b_ref, nw_ref, o_ref):
    j = pl.program_id(1)
    c = c_ref[...]
    cond = c / (1.0 + jnp.exp(-c))
    r = jnp.dot(cond, w_ref[0], preferred_element_type=F32,
                precision=lax.Precision.HIGHEST) + b_ref[0]
    kind = j % 3
    sub = j // 3
    r = jnp.where(kind == 1, (1.0 + r) * nw_ref[0, 0], r)
    r = jnp.where(jnp.logical_and(kind == 2, sub != 1), 0.5 * r, r)
    o_ref[0] = r


def _adaln(c, w_ada, b_ada, norm_w):
    B = c.shape[0]
    n_tiles = N_SUB * 3
    return pl.pallas_call(
        _adaln_kernel,
        grid=(DEPTH, n_tiles),
        in_specs=[
            pl.BlockSpec((B, D_MODEL), lambda l, j: (0, 0)),
            pl.BlockSpec((1, D_MODEL, D_MODEL), lambda l, j: (l, 0, j)),
            pl.BlockSpec((1, 1, D_MODEL), lambda l, j: (l, 0, j)),
            pl.BlockSpec((1, 1, 1, D_MODEL), lambda l, j: (l, j // 3, 0, 0)),
        ],
        out_specs=pl.BlockSpec((1, B, D_MODEL), lambda l, j: (l, 0, j)),
        out_shape=jax.ShapeDtypeStruct((DEPTH, B, n_tiles * D_MODEL), F32),
        compiler_params=_params("arbitrary", "arbitrary"),
        name="adaln_mod",
    )(c, w_ada, b_ada.reshape(DEPTH, 1, n_tiles * D_MODEL), norm_w.reshape(DEPTH, N_SUB, 1, D_MODEL))


def _modulated_norm(x, mod):
    ms = jnp.mean(x * x, axis=-1, keepdims=True)
    return x * lax.rsqrt(ms + NORM_EPS) * mod[1:2] + mod[0:1]


def _ffn_kernel(x_ref, mod_ref, wup_ref, wdn_ref, *rest, n_chunks, final):
    if final:
        fw_ref, o_ref = rest
    else:
        (o_ref,) = rest
    x = x_ref[0]
    mod = mod_ref[0]
    h = _modulated_norm(x, mod).astype(BF16)
    acc = jnp.zeros(x.shape, F32)
    for c in range(n_chunks):
        u = jnp.dot(h, wup_ref[c], preferred_element_type=F32)
        a = u[:, :FFN_FC]
        b = u[:, FFN_FC:]
        act = (0.5 * a) * (1.0 + jnp.tanh(0.5 * a)) * b
        acc = acc + jnp.dot(act.astype(BF16), wdn_ref[c], preferred_element_type=F32)
    y = x + mod[2:3] * acc
    if final:
        ms = jnp.mean(y * y, axis=-1, keepdims=True)
        y = y * lax.rsqrt(ms + NORM_EPS) * fw_ref[...]
    o_ref[0] = y


def _ffn(x, mod, wup, wdn, final_w=None):
    B, S, D = x.shape
    n_chunks = wup.shape[0]
    final = final_w is not None
    in_specs = [
        pl.BlockSpec((1, FFN_TM, D), lambda b, t: (b, t, 0)),
        pl.BlockSpec((1, 3, D), lambda b, t: (b, 0, 0)),
        _resident(wup.shape),
        _resident(wdn.shape),
    ]
    args = [x, mod, wup, wdn]
    if final:
        in_specs.append(_resident((1, D)))
        args.append(final_w.reshape(1, D))
    return pl.pallas_call(
        functools.partial(_ffn_kernel, n_chunks=n_chunks, final=final),
        grid=(B, S // FFN_TM),
        in_specs=in_specs,
        out_specs=pl.BlockSpec((1, FFN_TM, D), lambda b, t: (b, t, 0)),
        out_shape=jax.ShapeDtypeStruct(x.shape, F32),
        compiler_params=_params("parallel", "parallel"),
        name="ffn_final" if final else "ffn",
    )(*args)


def _inproj_kernel(x_ref, mod_ref, w_ref, rq_ref, rk_ref, rv_ref, rg_ref, dq_ref, dk_ref, dv_ref,
                   gr_ref, gd_ref):
    x = x_ref[0]
    h = _modulated_norm(x, mod_ref[0]).astype(BF16)

    def proj(c):
        return jnp.dot(h, w_ref[:, c * D_MODEL:(c + 1) * D_MODEL], preferred_element_type=F32)

    u = proj(0).astype(BF16)
    for hh in range(RET_HEADS):
        rq_ref[0, hh] = u[:, hh * RET_QK_DIM:(hh + 1) * RET_QK_DIM]
        rk_ref[0, hh] = u[:, (RET_HEADS + hh) * RET_QK_DIM:(RET_HEADS + hh + 1) * RET_QK_DIM]
    u = proj(1).astype(BF16)
    for hh in range(RET_HEADS):
        rv_ref[0, hh] = u[:, hh * RET_V_DIM:(hh + 1) * RET_V_DIM]
    rg_ref[0] = proj(2).astype(BF16)
    u = (proj(3) * (DIFF_HEAD_DIM ** -0.5 * LOG2E)).astype(BF16)
    for hh in range(DIFF_HEADS):
        dq_ref[0, hh] = u[:, hh * DIFF_V_DIM:(hh + 1) * DIFF_V_DIM]
    u = proj(4).astype(BF16)
    for hh in range(DIFF_HEADS):
        dk_ref[0, hh] = u[:, hh * DIFF_V_DIM:(hh + 1) * DIFF_V_DIM]
    u = proj(5).astype(BF16)
    for hh in range(DIFF_HEADS):
        dv_ref[0, hh] = u[:, hh * DIFF_V_DIM:(hh + 1) * DIFF_V_DIM]
    gr_ref[0] = proj(6).astype(BF16)
    gd_ref[0] = proj(7).astype(BF16)


def _inproj(x, mod, w_in):
    B, S, D = x.shape
    TM = FFN_TM
    head_spec = lambda nh, w: pl.BlockSpec((1, nh, TM, w), lambda b, t: (b, 0, t, 0))
    tok_spec = pl.BlockSpec((1, TM, D), lambda b, t: (b, t, 0))
    sds = jax.ShapeDtypeStruct
    return pl.pallas_call(
        _inproj_kernel,
        grid=(B, S // TM),
        in_specs=[tok_spec, pl.BlockSpec((1, 3, D), lambda b, t: (b, 0, 0)), _resident(w_in.shape)],
        out_specs=[
            head_spec(RET_HEADS, RET_QK_DIM), head_spec(RET_HEADS, RET_QK_DIM),
            head_spec(RET_HEADS, RET_V_DIM), tok_spec,
            head_spec(DIFF_HEADS, DIFF_V_DIM), head_spec(DIFF_HEADS, DIFF_V_DIM),
            head_spec(DIFF_HEADS, DIFF_V_DIM), tok_spec, tok_spec,
        ],
        out_shape=[
            sds((B, RET_HEADS, S, RET_QK_DIM), BF16), sds((B, RET_HEADS, S, RET_QK_DIM), BF16),
            sds((B, RET_HEADS, S, RET_V_DIM), BF16), sds((B, S, D), BF16),
            sds((B, DIFF_HEADS, S, DIFF_V_DIM), BF16), sds((B, DIFF_HEADS, S, DIFF_V_DIM), BF16),
            sds((B, DIFF_HEADS, S, DIFF_V_DIM), BF16), sds((B, S, D), BF16), sds((B, S, D), BF16),
        ],
        compiler_params=_params("parallel", "parallel"),
        name="mixer_inproj",
    )(x, mod, w_in)


def _retention_consts():
    L = RET_L
    gamma = 1.0 - 2.0 ** (-5.0 - jnp.arange(RET_HEADS, dtype=F32))
    log_g = jnp.log(gamma)
    r = jnp.arange(L, dtype=F32)
    scale = RET_QK_DIM ** -0.5
    allowed = (jnp.arange(L)[None, :] // CHUNK) <= (jnp.arange(L)[:, None] // CHUNK)
    intra = jnp.exp(log_g[:, None, None] * jnp.abs(r[:, None] - r[None, :]))
    intra = jnp.where(allowed[None], intra, 0.0) * scale
    q_dec = jnp.exp(log_g[:, None] * r[None, :]) * scale
    k_dec = jnp.exp(log_g[:, None] * (L - r)[None, :])
    blk_dec = jnp.exp(log_g * L)
    q_dec = jnp.broadcast_to(q_dec[:, :, None], (RET_HEADS, L, RET_QK_DIM))
    k_dec = jnp.broadcast_to(k_dec[:, :, None], (RET_HEADS, L, RET_QK_DIM))
    blk_dec = jnp.broadcast_to(blk_dec[:, None, None], (RET_HEADS, 1, RET_V_DIM))
    return intra, q_dec, k_dec, blk_dec


def _retention_kernel(q_ref, k_ref, v_ref, g_ref, w_ref, qd_ref, kd_ref, bd_ref, gn_ref, o_ref, state_ref):
    t = pl.program_id(1)

    @pl.when(t == 0)
    def _():
        state_ref[...] = jnp.zeros_like(state_ref)

    dv = v_ref.shape[-1]
    for h in range(q_ref.shape[1]):
        q = q_ref[0, h]
        k = k_ref[0, h]
        v = v_ref[0, h]
        s = lax.dot_general(q, k, (((1,), (1,)), ((), ())), preferred_element_type=F32)
        p = (s * w_ref[h]).astype(BF16)
        y = jnp.dot(p, v, preferred_element_type=F32)
        state = state_ref[h]
        qd = (q.astype(F32) * qd_ref[h]).astype(BF16)
        y = y + jnp.dot(qd, state.astype(BF16), preferred_element_type=F32)
        kd_t = (k.astype(F32) * kd_ref[h]).T.astype(BF16)
        state_ref[h] = state * bd_ref[h] + jnp.dot(kd_t, v, preferred_element_type=F32)
        ms = jnp.mean(y * y, axis=-1, keepdims=True)
        y = y * lax.rsqrt(ms + NORM_EPS) * gn_ref[h]
        g = g_ref[0, :, h * dv:(h + 1) * dv].astype(F32)
        o_ref[0, :, h * dv:(h + 1) * dv] = (y * (0.5 * g) * (1.0 + jnp.tanh(0.5 * g))).astype(BF16)


def _retention(rq, rk, rv, rg, ret_gn_l):
    B, H, S, dk = rq.shape
    dv = rv.shape[-1]
    L = RET_L
    intra, q_dec, k_dec, blk_dec = _retention_consts()
    blk = lambda w: pl.BlockSpec((1, H, L, w), lambda b, t: (b, 0, t, 0))
    tok = pl.BlockSpec((1, L, H * dv), lambda b, t: (b, t, 0))
    return pl.pallas_call(
        _retention_kernel,
        grid=(B, S // L),
        in_specs=[
            blk(dk), blk(dk), blk(dv), tok,
            _resident((H, L, L)), _resident((H, L, dk)), _resident((H, L, dk)), _resident((H, 1, dv)),
            _resident((H, 1, dv)),
        ],
        out_specs=tok,
        out_shape=jax.ShapeDtypeStruct((B, S, H * dv), BF16),
        scratch_shapes=[pltpu.VMEM((H, dk, dv), F32)],
        compiler_params=_params("parallel", "arbitrary"),
        name="retention",
    )(rq, rk, rv, rg, intra, q_dec, k_dec, blk_dec, ret_gn_l.reshape(H, 1, dv))


def _attn_bias_tables(n_heads):
    T = ATT_TQ
    slopes = 2.0 ** (-8.0 * jnp.arange(1, n_heads + 1, dtype=F32) / n_heads)
    r = jnp.arange(T, dtype=F32)
    a, c = r[:, None], r[None, :]
    allowed = (jnp.arange(T)[:, None] // CHUNK) <= (jnp.arange(T)[None, :] // CHUNK)
    diag = slopes[:, None, None] * (c - jnp.abs(c - a))[None]
    diag = jnp.where(allowed[None], diag, NEG_BIG)
    diag = jnp.concatenate([diag, diag], axis=-1)
    off = jnp.broadcast_to((slopes[:, None] * r[None, :])[:, :, None], (n_heads, T, 128))
    return slopes * LOG2E, diag * LOG2E, off * LOG2E


def _diffattn_kernel(slope_ref, q_ref, k_ref, v_ref, db_ref, ob_ref, lam_ref, sub_ref, o_ref,
                     vt_ref, qbd_ref, s_ref, p_ref, bmax_ref, alpha_ref, m_ref, knorm_ref, acc_ref, *, lam_init):
    h = pl.program_id(1)
    g = pl.program_id(2)
    TQ, TK = ATT_TQ, ATT_TK
    n_kv = vt_ref.shape[0]

    def max_sq_norms(x):
        xsq = jnp.square(x.astype(F32))
        first = lax.broadcasted_iota(jnp.int32, xsq.shape, 1) < DIFF_HEAD_DIM
        n1 = jnp.sum(jnp.where(first, xsq, 0.0), axis=1, keepdims=True)
        n2 = jnp.sum(jnp.where(first, 0.0, xsq), axis=1, keepdims=True)
        return jnp.max(n1, axis=0, keepdims=True), jnp.max(n2, axis=0, keepdims=True)

    @pl.when(g == 0)
    def _():
        k1sq = jnp.zeros((1, 1), F32)
        k2sq = jnp.zeros((1, 1), F32)
        for c in range(n_kv):
            vt_ref[c, 0:DIFF_V_DIM] = v_ref[0, 0, c * TK:(c + 1) * TK, :].astype(F32).T.astype(BF16)
            ones_row = lax.broadcasted_iota(jnp.int32, (ATT_SUM_ROWS, TK), 0) == 0
            vt_ref[c, DIFF_V_DIM:DIFF_V_DIM + ATT_SUM_ROWS] = ones_row.astype(F32).astype(BF16)
            c1, c2 = max_sq_norms(k_ref[0, 0, c * TK:(c + 1) * TK, :])
            k1sq, k2sq = jnp.maximum(k1sq, c1), jnp.maximum(k2sq, c2)
        knorm_ref[0:1] = jnp.broadcast_to(k1sq, (1, 128))
        knorm_ref[1:2] = jnp.broadcast_to(k2sq, (1, 128))

    slope = slope_ref[h]
    tiles = range(ATT_TILES)
    tile_idx = [g * ATT_TILES + x for x in tiles]
    for x in tiles:
        q = q_ref[0, 0, x * TQ:(x + 1) * TQ, :]
        lane = lax.broadcasted_iota(jnp.int32, q.shape, 1)
        zero = jnp.zeros_like(q)
        qbd_ref[x, 0:TQ] = jnp.where(lane < DIFF_HEAD_DIM, q, zero)
        qbd_ref[x, TQ:2 * TQ] = jnp.where(lane >= DIFF_HEAD_DIM, q, zero)
        m_ref[x] = jnp.full(m_ref.shape[1:], NEG_BIG, F32)
        acc_ref[x] = jnp.zeros(acc_ref.shape[1:], F32)

    def block_of(x, t):
        return jnp.maximum(tile_idx[x] - t, 0)

    def stage_scores(x, t, slot, diagonal):
        k = k_ref[0, 0, pl.ds(pl.multiple_of(block_of(x, t) * TK, TK), TK), :]
        s = lax.dot_general(k, qbd_ref[x], (((1,), (1,)), ((), ())), preferred_element_type=F32)
        bmax = None
        for r in range(0, TK, ATT_EXP_ROWS):
            rows = slice(r, r + ATT_EXP_ROWS)
            if diagonal:
                u = s[rows] + db_ref[0, rows]
            else:
                u = s[rows] + jnp.concatenate([ob_ref[0, rows]] * (2 * TQ // 128), axis=1)
            s_ref[x, slot, rows] = u
            cmax = jnp.max(u.reshape(ATT_EXP_ROWS // 8, 8, 2 * TQ), axis=0)
            bmax = cmax if bmax is None else jnp.maximum(bmax, cmax)
        bmax_ref[x, slot] = jnp.max(bmax, axis=0, keepdims=True)

    def stage_softmax(x, t, slot, diagonal):
        if diagonal:
            shift = 0.0
        else:
            shift = jnp.where(t < n_steps[x], -slope * (t * TK).astype(F32), NEG_BIG)
        m = m_ref[x]
        m_new = jnp.maximum(m, bmax_ref[x, slot] + shift)
        alpha = jnp.exp2(m - m_new)
        m_ref[x] = m_new
        alpha_ref[x, slot] = alpha
        ref = m_new - shift
        for r in range(0, TK, ATT_EXP_ROWS):
            rows = slice(r, r + ATT_EXP_ROWS)
            p_ref[x, slot, rows] = jnp.exp2(s_ref[x, slot, rows] - ref).astype(BF16)

    def stage_pv(x, t, slot):
        pv = jnp.dot(vt_ref[block_of(x, t)], p_ref[x, slot], preferred_element_type=F32)
        acc_ref[x] = alpha_ref[x, slot] * acc_ref[x] + pv

    for x in tiles:
        stage_scores(x, 0, 0, True)
    for x in tiles:
        stage_softmax(x, 0, 0, True)
    for x in tiles:
        stage_scores(x, 1, 1, False)

    n_steps = []
    for x in tiles:
        q1sq, q2sq = max_sq_norms(q_ref[0, 0, x * TQ:(x + 1) * TQ, :])
        bound = jnp.sqrt(jnp.maximum(q1sq * knorm_ref[0:1, 0:1], q2sq * knorm_ref[1:2, 0:1])) * 1.001
        m_min = jnp.min(m_ref[x], axis=1, keepdims=True)
        reach = (ATT_EXP2_ZERO + bound - m_min) / (slope * TK)
        n = jnp.minimum(jnp.floor(reach) + 2.0, (tile_idx[x] + 1).astype(F32))
        n_steps.append(jnp.max(n).astype(jnp.int32))

    def pipeline_tick(t, slot):
        for x in tiles:
            stage_pv(x, t, slot)
        for x in tiles:
            stage_softmax(x, t + 1, 1 - slot, False)
        for x in tiles:
            stage_scores(x, t + 2, slot, False)

    def pipeline_drain(t, slot):
        for x in tiles:
            stage_pv(x, t, slot)
        for x in tiles:
            stage_softmax(x, t + 1, 1 - slot, False)
        for x in tiles:
            stage_pv(x, t + 1, 1 - slot)

    def by_parity(fn, t):
        for slot in (0, 1):
            @pl.when(jnp.bitwise_and(t, 1) == slot)
            def _():
                fn(t, slot)

    def body(t, carry):
        by_parity(pipeline_tick, t)
        return carry

    n_ticks = jnp.maximum(functools.reduce(jnp.maximum, n_steps), 2)
    lax.fori_loop(0, n_ticks - 2, body, 0)
    by_parity(pipeline_drain, n_ticks - 2)
    lp = lam_ref[...]
    lam = (jnp.exp(jnp.sum(lp[0:1] * lp[1:2], axis=-1, keepdims=True))
           - jnp.exp(jnp.sum(lp[2:3] * lp[3:4], axis=-1, keepdims=True)) + lam_init)
    for x in tiles:
        o = acc_ref[x, 0:DIFF_V_DIM] / acc_ref[x, DIFF_V_DIM:DIFF_V_DIM + 1]
        y = (o[:, :TQ] - lam * o[:, TQ:]).T
        ms = jnp.mean(y * y, axis=-1, keepdims=True)
        y = y * lax.rsqrt(ms + NORM_EPS) * sub_ref[...] * (1.0 - lam_init)
        o_ref[0, x * TQ:(x + 1) * TQ] = y.astype(BF16)


def _diffattn(dq, dk, dv, lam_params, subln, layer_idx):
    B, H, S, w = dq.shape
    lam_init = 0.8 - 0.6 * math.exp(-0.3 * layer_idx)
    slopes, diag_bias, off_bias = _attn_bias_tables(H)
    NT, TQS = ATT_TILES, ATT_TILES * ATT_TQ
    grid_spec = pltpu.PrefetchScalarGridSpec(
        num_scalar_prefetch=1,
        grid=(B, H, S // TQS),
        in_specs=[
            pl.BlockSpec((1, 1, TQS, w), lambda b, h, i, sl: (b, h, i, 0)),
            pl.BlockSpec((1, 1, S, w), lambda b, h, i, sl: (b, h, 0, 0)),
            pl.BlockSpec((1, 1, S, w), lambda b, h, i, sl: (b, h, 0, 0)),
            pl.BlockSpec((1, ATT_TK, 2 * ATT_TQ), lambda b, h, i, sl: (h, 0, 0)),
            pl.BlockSpec((1, ATT_TK, 128), lambda b, h, i, sl: (h, 0, 0)),
            pl.BlockSpec((4, DIFF_HEAD_DIM), lambda b, h, i, sl: (0, 0)),
            pl.BlockSpec((1, w), lambda b, h, i, sl: (0, 0)),
        ],
        out_specs=pl.BlockSpec((1, TQS, w), lambda b, h, i, sl: (b, i, h)),
        scratch_shapes=[
            pltpu.VMEM((S // ATT_TK, w + ATT_SUM_ROWS, ATT_TK), BF16),
            pltpu.VMEM((NT, 2 * ATT_TQ, w), BF16),
            pltpu.VMEM((NT, 2, ATT_TK, 2 * ATT_TQ), F32),
            pltpu.VMEM((NT, 2, ATT_TK, 2 * ATT_TQ), BF16),
            pltpu.VMEM((NT, 2, 1, 2 * ATT_TQ), F32),
            pltpu.VMEM((NT, 2, 1, 2 * ATT_TQ), F32),
            pltpu.VMEM((NT, 1, 2 * ATT_TQ), F32),
            pltpu.VMEM((8, 128), F32),
            pltpu.VMEM((NT, w + ATT_SUM_ROWS, 2 * ATT_TQ), F32),
        ],
    )
    return pl.pallas_call(
        functools.partial(_diffattn_kernel, lam_init=lam_init),
        grid_spec=grid_spec,
        out_shape=jax.ShapeDtypeStruct((B, S, H * w), BF16),
        compiler_params=_params("parallel", "parallel", "arbitrary"),
        name="diff_attention",
    )(slopes, dq, dk, dv, diag_bias, off_bias, lam_params, subln.reshape(1, w))


def _merge_kernel(x_ref, mod_ref, yr_ref, yd_ref, gr_ref, gd_ref, wr_ref, wd_ref, wo_ref, o_ref):
    def sigmoid(g):
        return 0.5 * (1.0 + jnp.tanh(0.5 * g))

    br = jnp.dot(yr_ref[0], wr_ref[...], preferred_element_type=F32)
    bd = jnp.dot(yd_ref[0], wd_ref[...], preferred_element_type=F32)
    merged = sigmoid(gr_ref[0].astype(F32)) * br + sigmoid(gd_ref[0].astype(F32)) * bd
    out = jnp.dot(merged.astype(BF16), wo_ref[...], preferred_element_type=F32)
    o_ref[0] = x_ref[0] + mod_ref[0][2:3] * out


def _merge(x, mod, y_ret, y_diff, g_ret, g_diff, w_rb, w_db, w_o):
    B, S, D = x.shape
    tok = pl.BlockSpec((1, FFN_TM, D), lambda b, t: (b, t, 0))
    return pl.pallas_call(
        _merge_kernel,
        grid=(B, S // FFN_TM),
        in_specs=[tok, pl.BlockSpec((1, 3, D), lambda b, t: (b, 0, 0)), tok, tok, tok, tok,
                  _resident(w_rb.shape), _resident(w_db.shape), _resident(w_o.shape)],
        out_specs=tok,
        out_shape=jax.ShapeDtypeStruct(x.shape, F32),
        compiler_params=_params("parallel", "parallel"),
        name="mixer_merge",
    )(x, mod, y_ret, y_diff, g_ret, g_diff, w_rb, w_db, w_o)


def _prep_ffn_weights(w_up, w_down):
    n = D_FF // FFN_FC
    up = w_up.astype(BF16).reshape(D_MODEL, 2, n, FFN_FC).transpose(2, 0, 1, 3).reshape(n, D_MODEL, 2 * FFN_FC)
    down = w_down.astype(BF16).reshape(n, FFN_FC, D_MODEL)
    return up, down


def kernel(x, c, w_ada, b_ada, norm_w, w_ffn_up, w_ffn_down, w_in, ret_gn, lambda_q1, lambda_k1, lambda_q2,
           lambda_k2, diff_subln, w_ret_branch, w_diff_branch, w_out, final_norm):
    B, S, D = x.shape
    assert D == D_MODEL and S % max(FFN_TM, RET_L, ATT_TILES * ATT_TQ) == 0 and ATT_TQ == ATT_TK
    mod_all = _adaln(c, w_ada, b_ada, norm_w).reshape(DEPTH, B, N_SUB, 3, D)
    for l in range(DEPTH):
        mod = [mod_all[l, :, s] for s in range(N_SUB)]
        up, down = _prep_ffn_weights(w_ffn_up[l, 0], w_ffn_down[l, 0])
        x = _ffn(x, mod[0], up, down)
        rq, rk, rv, rg, dq, dk, dv, g_ret, g_diff = _inproj(x, mod[1], w_in[l].astype(BF16))
        y_ret = _retention(rq, rk, rv, rg, ret_gn[l])
        lam_params = jnp.stack([lambda_q1[l], lambda_k1[l], lambda_q2[l], lambda_k2[l]])
        y_diff = _diffattn(dq, dk, dv, lam_params, diff_subln[l], l)
        x = _merge(x, mod[1], y_ret, y_diff, g_ret, g_diff, w_ret_branch[l].astype(BF16),
                   w_diff_branch[l].astype(BF16), w_out[l].astype(BF16))
        up, down = _prep_ffn_weights(w_ffn_up[l, 1], w_ffn_down[l, 1])
        x = _ffn(x, mod[2], up, down, final_w=final_norm if l == DEPTH - 1 else None)
    return x
```

```python
import functools
import math

import jax
import jax.numpy as jnp
from jax import lax
from jax.experimental import pallas as pl
from jax.experimental.pallas import tpu as pltpu

DEPTH = 4
D_MODEL = 1024
CHUNK = 64
CHUNK_SHIFT = 6
RET_HEADS = 4
RET_QK_DIM = 128
RET_V_DIM = 256
DIFF_HEADS = 8
DIFF_HEAD_DIM = 64
DIFF_V_DIM = 2 * DIFF_HEAD_DIM
D_FF = 2816
N_SUB = 3
NORM_EPS = 1e-6
IN_WIDTH = 8 * D_MODEL

F32 = jnp.float32
BF16 = jnp.bfloat16

V7X_VMEM_LIMIT_BYTES = 56 * 1024 * 1024

FFN_TM = 512
FFN_FC = 256
RET_L = 256
ATT_TQ = 512
ATT_TK = 512
ATT_EXP_ROWS = 16
ATT_TILES = 2
ATT_POS_LANES = 3
ATT_SUM_ROWS = 16
ATT_EXP2_ZERO = 160.0
NEG_BIG = -1e30
LOG2E = math.log2(math.e)


def _resident(shape):
    nd = len(shape)
    return pl.BlockSpec(shape, lambda *_: (0,) * nd, pipeline_mode=pl.Buffered(1))


def _params(*sem, flags=None):
    return pltpu.CompilerParams(dimension_semantics=sem, vmem_limit_bytes=V7X_VMEM_LIMIT_BYTES, flags=flags)


def _adaln_kernel(c_ref, w_ref, b_ref, nw_ref, o_ref):
    j = pl.program_id(1)
    c = c_ref[...]
    cond = c / (1.0 + jnp.exp(-c))
    r = jnp.dot(cond, w_ref[0], preferred_element_type=F32,
                precision=lax.Precision.HIGHEST) + b_ref[0]
    kind = j % 3
    sub = j // 3
    r = jnp.where(kind == 1, (1.0 + r) * nw_ref[0, 0], r)
    r = jnp.where(jnp.logical_and(kind == 2, sub != 1), 0.5 * r, r)
    o_ref[0] = r


def _adaln(c, w_ada, b_ada, norm_w):
    B = c.shape[0]
    n_tiles = N_SUB * 3
    return pl.pallas_call(
        _adaln_kernel,
        grid=(DEPTH, n_tiles),
        in_specs=[
            pl.BlockSpec((B, D_MODEL), lambda l, j: (0, 0)),
            pl.BlockSpec((1, D_MODEL, D_MODEL), lambda l, j: (l, 0, j)),
            pl.BlockSpec((1, 1, D_MODEL), lambda l, j: (l, 0, j)),
            pl.BlockSpec((1, 1, 1, D_MODEL), lambda l, j: (l, j // 3, 0, 0)),
        ],
        out_specs=pl.BlockSpec((1, B, D_MODEL), lambda l, j: (l, 0, j)),
        out_shape=jax.ShapeDtypeStruct((DEPTH, B, n_tiles * D_MODEL), F32),
        compiler_params=_params("arbitrary", "arbitrary"),
        name="adaln_mod",
    )(c, w_ada, b_ada.reshape(DEPTH, 1, n_tiles * D_MODEL), norm_w.reshape(DEPTH, N_SUB, 1, D_MODEL))


def _modulated_norm(x, mod):
    ms = jnp.mean(x * x, axis=-1, keepdims=True)
    return x * lax.rsqrt(ms + NORM_EPS) * mod[1:2] + mod[0:1]


def _ffn_kernel(x_ref, mod_ref, wup_ref, wdn_ref, *rest, n_chunks, final):
    if final:
        fw_ref, o_ref = rest
    else:
        (o_ref,) = rest
    x = x_ref[0]
    mod = mod_ref[0]
    h = _modulated_norm(x, mod).astype(BF16)
    acc = jnp.zeros(x.shape, F32)
    for c in range(n_chunks):
        u = jnp.dot(h, wup_ref[c], preferred_element_type=F32)
        a = u[:, :FFN_FC]
        b = u[:, FFN_FC:]
        act = (0.5 * a) * (1.0 + jnp.tanh(0.5 * a)) * b
        acc = acc + jnp.dot(act.astype(BF16), wdn_ref[c], preferred_element_type=F32)
    y = x + mod[2:3] * acc
    if final:
        ms = jnp.mean(y * y, axis=-1, keepdims=True)
        y = y * lax.rsqrt(ms + NORM_EPS) * fw_ref[...]
    o_ref[0] = y


def _ffn(x, mod, wup, wdn, final_w=None):
    B, S, D = x.shape
    n_chunks = wup.shape[0]
    final = final_w is not None
    in_specs = [
        pl.BlockSpec((1, FFN_TM, D), lambda b, t: (b, t, 0)),
        pl.BlockSpec((1, 3, D), lambda b, t: (b, 0, 0)),
        _resident(wup.shape),
        _resident(wdn.shape),
    ]
    args = [x, mod, wup, wdn]
    if final:
        in_specs.append(_resident((1, D)))
        args.append(final_w.reshape(1, D))
    return pl.pallas_call(
        functools.partial(_ffn_kernel, n_chunks=n_chunks, final=final),
        grid=(B, S // FFN_TM),
        in_specs=in_specs,
        out_specs=pl.BlockSpec((1, FFN_TM, D), lambda b, t: (b, t, 0)),
        out_shape=jax.ShapeDtypeStruct(x.shape, F32),
        compiler_params=_params("parallel", "parallel"),
        name="ffn_final" if final else "ffn",
    )(*args)


def _inproj_kernel(x_ref, mod_ref, w_ref, rq_ref, rk_ref, rv_ref, rg_ref, dq_ref, dk_ref, dv_ref,
                   gr_ref, gd_ref):
    x = x_ref[0]
    h = _modulated_norm(x, mod_ref[0]).astype(BF16)

    def proj(c):
        return jnp.dot(h, w_ref[:, c * D_MODEL:(c + 1) * D_MODEL], preferred_element_type=F32)

    u = proj(0).astype(BF16)
    for hh in range(RET_HEADS):
        rq_ref[0, hh] = u[:, hh * RET_QK_DIM:(hh + 1) * RET_QK_DIM]
        rk_ref[0, hh] = u[:, (RET_HEADS + hh) * RET_QK_DIM:(RET_HEADS + hh + 1) * RET_QK_DIM]
    u = proj(1).astype(BF16)
    for hh in range(RET_HEADS):
        rv_ref[0, hh] = u[:, hh * RET_V_DIM:(hh + 1) * RET_V_DIM]
    rg_ref[0] = proj(2).astype(BF16)
    u = (proj(3) * (DIFF_HEAD_DIM ** -0.5 * LOG2E)).astype(BF16)
    for hh in range(DIFF_HEADS):
        dq_ref[0, hh] = u[:, hh * DIFF_V_DIM:(hh + 1) * DIFF_V_DIM]
    u = proj(4).astype(BF16)
    for hh in range(DIFF_HEADS):
        dk_ref[0, hh] = u[:, hh * DIFF_V_DIM:(hh + 1) * DIFF_V_DIM]
    u = proj(5).astype(BF16)
    for hh in range(DIFF_HEADS):
        dv_ref[0, hh] = u[:, hh * DIFF_V_DIM:(hh + 1) * DIFF_V_DIM]
    gr_ref[0] = proj(6).astype(BF16)
    gd_ref[0] = proj(7).astype(BF16)


def _inproj(x, mod, w_in):
    B, S, D = x.shape
    TM = FFN_TM
    head_spec = lambda nh, w: pl.BlockSpec((1, nh, TM, w), lambda b, t: (b, 0, t, 0))
    tok_spec = pl.BlockSpec((1, TM, D), lambda b, t: (b, t, 0))
    sds = jax.ShapeDtypeStruct
    return pl.pallas_call(
        _inproj_kernel,
        grid=(B, S // TM),
        in_specs=[tok_spec, pl.BlockSpec((1, 3, D), lambda b, t: (b, 0, 0)), _resident(w_in.shape)],
        out_specs=[
            head_spec(RET_HEADS, RET_QK_DIM), head_spec(RET_HEADS, RET_QK_DIM),
            head_spec(RET_HEADS, RET_V_DIM), tok_spec,
            head_spec(DIFF_HEADS, DIFF_V_DIM), head_spec(DIFF_HEADS, DIFF_V_DIM),
            head_spec(DIFF_HEADS, DIFF_V_DIM), tok_spec, tok_spec,
        ],
        out_shape=[
            sds((B, RET_HEADS, S, RET_QK_DIM), BF16), sds((B, RET_HEADS, S, RET_QK_DIM), BF16),
            sds((B, RET_HEADS, S, RET_V_DIM), BF16), sds((B, S, D), BF16),
            sds((B, DIFF_HEADS, S, DIFF_V_DIM), BF16), sds((B, DIFF_HEADS, S, DIFF_V_DIM), BF16),
            sds((B, DIFF_HEADS, S, DIFF_V_DIM), BF16), sds((B, S, D), BF16), sds((B, S, D), BF16),
        ],
        compiler_params=_params("parallel", "parallel"),
        name="mixer_inproj",
    )(x, mod, w_in)


def _retention_consts():
    L = RET_L
    gamma = 1.0 - 2.0 ** (-5.0 - jnp.arange(RET_HEADS, dtype=F32))
    log_g = jnp.log(gamma)
    r = jnp.arange(L, dtype=F32)
    scale = RET_QK_DIM ** -0.5
    allowed = (jnp.arange(L)[None, :] // CHUNK) <= (jnp.arange(L)[:, None] // CHUNK)
    intra = jnp.exp(log_g[:, None, None] * jnp.abs(r[:, None] - r[None, :]))
    intra = jnp.where(allowed[None], intra, 0.0) * scale
    q_dec = jnp.exp(log_g[:, None] * r[None, :]) * scale
    k_dec = jnp.exp(log_g[:, None] * (L - r)[None, :])
    blk_dec = jnp.exp(log_g * L)
    q_dec = jnp.broadcast_to(q_dec[:, :, None], (RET_HEADS, L, RET_QK_DIM))
    k_dec = jnp.broadcast_to(k_dec[:, :, None], (RET_HEADS, L, RET_QK_DIM))
    blk_dec = jnp.broadcast_to(blk_dec[:, None, None], (RET_HEADS, 1, RET_V_DIM))
    return intra, q_dec, k_dec, blk_dec


def _retention_kernel(q_ref, k_ref, v_ref, g_ref, w_ref, qd_ref, kd_ref, bd_ref, gn_ref, o_ref, state_ref):
    t = pl.program_id(1)

    @pl.when(t == 0)
    def _():
        state_ref[...] = jnp.zeros_like(state_ref)

    dv = v_ref.shape[-1]
    for h in range(q_ref.shape[1]):
        q = q_ref[0, h]
        k = k_ref[0, h]
        v = v_ref[0, h]
        s = lax.dot_general(q, k, (((1,), (1,)), ((), ())), preferred_element_type=F32)
        p = (s * w_ref[h]).astype(BF16)
        y = jnp.dot(p, v, preferred_element_type=F32)
        state = state_ref[h]
        qd = (q.astype(F32) * qd_ref[h]).astype(BF16)
        y = y + jnp.dot(qd, state.astype(BF16), preferred_element_type=F32)
        kd_t = (k.astype(F32) * kd_ref[h]).T.astype(BF16)
        state_ref[h] = state * bd_ref[h] + jnp.dot(kd_t, v, preferred_element_type=F32)
        ms = jnp.mean(y * y, axis=-1, keepdims=True)
        y = y * lax.rsqrt(ms + NORM_EPS) * gn_ref[h]
        g = g_ref[0, :, h * dv:(h + 1) * dv].astype(F32)
        o_ref[0, :, h * dv:(h + 1) * dv] = (y * (0.5 * g) * (1.0 + jnp.tanh(0.5 * g))).astype(BF16)


def _retention(rq, rk, rv, rg, ret_gn_l):
    B, H, S, dk = rq.shape
    dv = rv.shape[-1]
    L = RET_L
    intra, q_dec, k_dec, blk_dec = _retention_consts()
    blk = lambda w: pl.BlockSpec((1, H, L, w), lambda b, t: (b, 0, t, 0))
    tok = pl.BlockSpec((1, L, H * dv), lambda b, t: (b, t, 0))
    return pl.pallas_call(
        _retention_kernel,
        grid=(B, S // L),
        in_specs=[
            blk(dk), blk(dk), blk(dv), tok,
            _resident((H, L, L)), _resident((H, L, dk)), _resident((H, L, dk)), _resident((H, 1, dv)),
            _resident((H, 1, dv)),
        ],
        out_specs=tok,
        out_shape=jax.ShapeDtypeStruct((B, S, H * dv), BF16),
        scratch_shapes=[pltpu.VMEM((H, dk, dv), F32)],
        compiler_params=_params("parallel", "arbitrary"),
        name="retention",
    )(rq, rk, rv, rg, intra, q_dec, k_dec, blk_dec, ret_gn_l.reshape(H, 1, dv))


def _attn_bias_tables(n_heads):
    T = ATT_TQ
    slopes = 2.0 ** (-8.0 * jnp.arange(1, n_heads + 1, dtype=F32) / n_heads) * LOG2E
    r = jnp.arange(T, dtype=F32)
    a, c = r[:, None], r[None, :]
    allowed = (jnp.arange(T)[:, None] // CHUNK) <= (jnp.arange(T)[None, :] // CHUNK)
    diag = slopes[:, None, None] * (-2.0 * jnp.maximum(a - c, 0.0))[None]
    diag = jnp.where(allowed[None], diag, NEG_BIG)
    diag = jnp.concatenate([diag, diag], axis=-1)
    p0 = slopes.astype(BF16)
    p1 = (slopes - p0.astype(F32)).astype(BF16)
    p2 = (slopes - p0.astype(F32) - p1.astype(F32)).astype(BF16)
    parts = jnp.stack([p0, p1, p2], axis=1).astype(F32)
    coef = jnp.concatenate([parts * CHUNK, parts, jnp.zeros((n_heads, 128 - 2 * ATT_POS_LANES), F32)], axis=1)
    return slopes, diag, coef.astype(BF16).reshape(n_heads, 1, 128)


def _diffattn_kernel(slope_ref, q_ref, k_ref, v_ref, db_ref, coef_ref, lam_ref, sub_ref, o_ref,
                     vt_ref, kaug_ref, qbd_ref, s_ref, p_ref, bmax_ref, alpha_ref, m_ref, knorm_ref, acc_ref, *,
                     lam_init):
    h = pl.program_id(1)
    g = pl.program_id(2)
    TQ, TK = ATT_TQ, ATT_TK
    n_kv = vt_ref.shape[0]

    def max_sq_norms(x):
        xsq = jnp.square(x.astype(F32))
        first = lax.broadcasted_iota(jnp.int32, xsq.shape, 1) < DIFF_HEAD_DIM
        n1 = jnp.sum(jnp.where(first, xsq, 0.0), axis=1, keepdims=True)
        n2 = jnp.sum(jnp.where(first, 0.0, xsq), axis=1, keepdims=True)
        return jnp.max(n1, axis=0, keepdims=True), jnp.max(n2, axis=0, keepdims=True)

    @pl.when(g == 0)
    def _():
        k1sq = jnp.zeros((1, 1), F32)
        k2sq = jnp.zeros((1, 1), F32)
        for c in range(n_kv):
            vt_ref[c, 0:DIFF_V_DIM] = v_ref[0, 0, c * TK:(c + 1) * TK, :].astype(F32).T.astype(BF16)
            ones_row = lax.broadcasted_iota(jnp.int32, (ATT_SUM_ROWS, TK), 0) == 0
            vt_ref[c, DIFF_V_DIM:DIFF_V_DIM + ATT_SUM_ROWS] = ones_row.astype(F32).astype(BF16)
            k = k_ref[0, 0, c * TK:(c + 1) * TK, :]
            c1, c2 = max_sq_norms(k)
            k1sq, k2sq = jnp.maximum(k1sq, c1), jnp.maximum(k2sq, c2)
            pos = c * TK + lax.broadcasted_iota(jnp.int32, k.shape, 0)
            lane = lax.broadcasted_iota(jnp.int32, k.shape, 1)
            feat = jnp.where(lane < ATT_POS_LANES, jnp.right_shift(pos, CHUNK_SHIFT),
                             jnp.where(lane < 2 * ATT_POS_LANES, jnp.bitwise_and(pos, CHUNK - 1), 0))
            kaug_ref[c * TK:(c + 1) * TK, 0:2 * DIFF_HEAD_DIM] = k
            kaug_ref[c * TK:(c + 1) * TK, 2 * DIFF_HEAD_DIM:] = feat.astype(F32).astype(BF16)
        knorm_ref[0:1] = jnp.broadcast_to(k1sq, (1, 128))
        knorm_ref[1:2] = jnp.broadcast_to(k2sq, (1, 128))

    slope = slope_ref[h]
    tiles = range(ATT_TILES)
    tile_idx = [g * ATT_TILES + x for x in tiles]
    for x in tiles:
        q = q_ref[0, 0, x * TQ:(x + 1) * TQ, :]
        lane = lax.broadcasted_iota(jnp.int32, q.shape, 1)
        zero = jnp.zeros_like(q)
        qbd_ref[x, 0:TQ, 0:2 * DIFF_HEAD_DIM] = jnp.where(lane < DIFF_HEAD_DIM, q, zero)
        qbd_ref[x, TQ:2 * TQ, 0:2 * DIFF_HEAD_DIM] = jnp.where(lane >= DIFF_HEAD_DIM, q, zero)
        qbd_ref[x, :, 2 * DIFF_HEAD_DIM:] = jnp.broadcast_to(coef_ref[0], (2 * TQ, 128))
        m_ref[x] = jnp.full(m_ref.shape[1:], NEG_BIG, F32)
        acc_ref[x] = jnp.zeros(acc_ref.shape[1:], F32)

    def block_of(x, t):
        return jnp.maximum(tile_idx[x] - t, 0)

    def stage_scores(x, t, slot, diagonal):
        k = kaug_ref[pl.ds(pl.multiple_of(block_of(x, t) * TK, TK), TK), :]
        s = lax.dot_general(k, qbd_ref[x], (((1,), (1,)), ((), ())), preferred_element_type=F32)
        bmax = None
        for r in range(0, TK, ATT_EXP_ROWS):
            rows = slice(r, r + ATT_EXP_ROWS)
            u = s[rows] + db_ref[0, rows] if diagonal else s[rows]
            s_ref[x, slot, rows] = u
            cmax = jnp.max(u.reshape(ATT_EXP_ROWS // 8, 8, 2 * TQ), axis=0)
            bmax = cmax if bmax is None else jnp.maximum(bmax, cmax)
        bmax_ref[x, slot] = jnp.max(bmax, axis=0, keepdims=True)

    def stage_softmax(x, t, slot, diagonal):
        if diagonal:
            shift = 0.0
        else:
            shift = jnp.where(t < n_steps[x], 0.0, NEG_BIG)
        m = m_ref[x]
        m_new = jnp.maximum(m, bmax_ref[x, slot] + shift)
        alpha = jnp.exp2(m - m_new)
        m_ref[x] = m_new
        alpha_ref[x, slot] = alpha
        ref = m_new - shift
        for r in range(0, TK, ATT_EXP_ROWS):
            rows = slice(r, r + ATT_EXP_ROWS)
            p_ref[x, slot, rows] = jnp.exp2(s_ref[x, slot, rows] - ref).astype(BF16)

    def stage_pv(x, t, slot):
        pv = jnp.dot(vt_ref[block_of(x, t)], p_ref[x, slot], preferred_element_type=F32)
        acc_ref[x] = alpha_ref[x, slot] * acc_ref[x] + pv

    for x in tiles:
        stage_scores(x, 0, 0, True)
    for x in tiles:
        stage_softmax(x, 0, 0, True)
    for x in tiles:
        stage_scores(x, 1, 1, False)

    n_steps = []
    for x in tiles:
        q1sq, q2sq = max_sq_norms(q_ref[0, 0, x * TQ:(x + 1) * TQ, :])
        bound = jnp.sqrt(jnp.maximum(q1sq * knorm_ref[0:1, 0:1], q2sq * knorm_ref[1:2, 0:1])) * 1.001
        m_min = jnp.min(m_ref[x], axis=1, keepdims=True)
        idx = tile_idx[x].astype(F32)
        reach = idx + (ATT_EXP2_ZERO + bound - m_min) / (slope * TK)
        n = jnp.minimum(jnp.floor(reach) + 2.0, idx + 1.0)
        n_steps.append(jnp.max(n).astype(jnp.int32))

    def pipeline_tick(t, slot):
        for x in tiles:
            stage_pv(x, t, slot)
        for x in tiles:
            stage_softmax(x, t + 1, 1 - slot, False)
        for x in tiles:
            stage_scores(x, t + 2, slot, False)

    def pipeline_drain(t, slot):
        for x in tiles:
            stage_pv(x, t, slot)
        for x in tiles:
            stage_softmax(x, t + 1, 1 - slot, False)
        for x in tiles:
            stage_pv(x, t + 1, 1 - slot)

    def by_parity(fn, t):
        for slot in (0, 1):
            @pl.when(jnp.bitwise_and(t, 1) == slot)
            def _():
                fn(t, slot)

    def body(t, carry):
        by_parity(pipeline_tick, t)
        return carry

    n_ticks = jnp.maximum(functools.reduce(jnp.maximum, n_steps), 2)
    lax.fori_loop(0, n_ticks - 2, body, 0)
    by_parity(pipeline_drain, n_ticks - 2)
    lp = lam_ref[...]
    lam = (jnp.exp(jnp.sum(lp[0:1] * lp[1:2], axis=-1, keepdims=True))
           - jnp.exp(jnp.sum(lp[2:3] * lp[3:4], axis=-1, keepdims=True)) + lam_init)
    for x in tiles:
        o = acc_ref[x, 0:DIFF_V_DIM] / acc_ref[x, DIFF_V_DIM:DIFF_V_DIM + 1]
        y = (o[:, :TQ] - lam * o[:, TQ:]).T
        ms = jnp.mean(y * y, axis=-1, keepdims=True)
        y = y * lax.rsqrt(ms + NORM_EPS) * sub_ref[...] * (1.0 - lam_init)
        o_ref[0, x * TQ:(x + 1) * TQ] = y.astype(BF16)


def _diffattn(dq, dk, dv, lam_params, subln, layer_idx):
    B, H, S, w = dq.shape
    lam_init = 0.8 - 0.6 * math.exp(-0.3 * layer_idx)
    slopes, diag_bias, coef = _attn_bias_tables(H)
    NT, TQS = ATT_TILES, ATT_TILES * ATT_TQ
    grid_spec = pltpu.PrefetchScalarGridSpec(
        num_scalar_prefetch=1,
        grid=(B, H, S // TQS),
        in_specs=[
            pl.BlockSpec((1, 1, TQS, w), lambda b, h, i, sl: (b, h, i, 0)),
            pl.BlockSpec((1, 1, S, w), lambda b, h, i, sl: (b, h, 0, 0)),
            pl.BlockSpec((1, 1, S, w), lambda b, h, i, sl: (b, h, 0, 0)),
            pl.BlockSpec((1, ATT_TK, 2 * ATT_TQ), lambda b, h, i, sl: (h, 0, 0)),
            pl.BlockSpec((1, 1, 128), lambda b, h, i, sl: (h, 0, 0)),
            pl.BlockSpec((4, DIFF_HEAD_DIM), lambda b, h, i, sl: (0, 0)),
            pl.BlockSpec((1, w), lambda b, h, i, sl: (0, 0)),
        ],
        out_specs=pl.BlockSpec((1, TQS, w), lambda b, h, i, sl: (b, i, h)),
        scratch_shapes=[
            pltpu.VMEM((S // ATT_TK, w + ATT_SUM_ROWS, ATT_TK), BF16),
            pltpu.VMEM((S, 2 * w), BF16),
            pltpu.VMEM((NT, 2 * ATT_TQ, 2 * w), BF16),
            pltpu.VMEM((NT, 2, ATT_TK, 2 * ATT_TQ), F32),
            pltpu.VMEM((NT, 2, ATT_TK, 2 * ATT_TQ), BF16),
            pltpu.VMEM((NT, 2, 1, 2 * ATT_TQ), F32),
            pltpu.VMEM((NT, 2, 1, 2 * ATT_TQ), F32),
            pltpu.VMEM((NT, 1, 2 * ATT_TQ), F32),
            pltpu.VMEM((8, 128), F32),
            pltpu.VMEM((NT, w + ATT_SUM_ROWS, 2 * ATT_TQ), F32),
        ],
    )
    return pl.pallas_call(
        functools.partial(_diffattn_kernel, lam_init=lam_init),
        grid_spec=grid_spec,
        out_shape=jax.ShapeDtypeStruct((B, S, H * w), BF16),
        compiler_params=_params("parallel", "parallel", "arbitrary"),
        name="diff_attention",
    )(slopes, dq, dk, dv, diag_bias, coef, lam_params, subln.reshape(1, w))


def _merge_kernel(x_ref, mod_ref, yr_ref, yd_ref, gr_ref, gd_ref, wr_ref, wd_ref, wo_ref, o_ref):
    def sigmoid(g):
        return 0.5 * (1.0 + jnp.tanh(0.5 * g))

    br = jnp.dot(yr_ref[0], wr_ref[...], preferred_element_type=F32)
    bd = jnp.dot(yd_ref[0], wd_ref[...], preferred_element_type=F32)
    merged = sigmoid(gr_ref[0].astype(F32)) * br + sigmoid(gd_ref[0].astype(F32)) * bd
    out = jnp.dot(merged.astype(BF16), wo_ref[...], preferred_element_type=F32)
    o_ref[0] = x_ref[0] + mod_ref[0][2:3] * out


def _merge(x, mod, y_ret, y_diff, g_ret, g_diff, w_rb, w_db, w_o):
    B, S, D = x.shape
    tok = pl.BlockSpec((1, FFN_TM, D), lambda b, t: (b, t, 0))
    return pl.pallas_call(
        _merge_kernel,
        grid=(B, S // FFN_TM),
        in_specs=[tok, pl.BlockSpec((1, 3, D), lambda b, t: (b, 0, 0)), tok, tok, tok, tok,
                  _resident(w_rb.shape), _resident(w_db.shape), _resident(w_o.shape)],
        out_specs=tok,
        out_shape=jax.ShapeDtypeStruct(x.shape, F32),
        compiler_params=_params("parallel", "parallel"),
        name="mixer_merge",
    )(x, mod, y_ret, y_diff, g_ret, g_diff, w_rb, w_db, w_o)


def _prep_ffn_weights(w_up, w_down):
    n = D_FF // FFN_FC
    up = w_up.astype(BF16).reshape(D_MODEL, 2, n, FFN_FC).transpose(2, 0, 1, 3).reshape(n, D_MODEL, 2 * FFN_FC)
    down = w_down.astype(BF16).reshape(n, FFN_FC, D_MODEL)
    return up, down


def kernel(x, c, w_ada, b_ada, norm_w, w_ffn_up, w_ffn_down, w_in, ret_gn, lambda_q1, lambda_k1, lambda_q2,
           lambda_k2, diff_subln, w_ret_branch, w_diff_branch, w_out, final_norm):
    B, S, D = x.shape
    assert D == D_MODEL and S % max(FFN_TM, RET_L, ATT_TILES * ATT_TQ) == 0 and ATT_TQ == ATT_TK
    mod_all = _adaln(c, w_ada, b_ada, norm_w).reshape(DEPTH, B, N_SUB, 3, D)
    for l in range(DEPTH):
        mod = [mod_all[l, :, s] for s in range(N_SUB)]
        up, down = _prep_ffn_weights(w_ffn_up[l, 0], w_ffn_down[l, 0])
        x = _ffn(x, mod[0], up, down)
        rq, rk, rv, rg, dq, dk, dv, g_ret, g_diff = _inproj(x, mod[1], w_in[l].astype(BF16))
        y_ret = _retention(rq, rk, rv, rg, ret_gn[l])
        lam_params = jnp.stack([lambda_q1[l], lambda_k1[l], lambda_q2[l], lambda_k2[l]])
        y_diff = _diffattn(dq, dk, dv, lam_params, diff_subln[l], l)
        x = _merge(x, mod[1], y_ret, y_diff, g_ret, g_diff, w_ret_branch[l].astype(BF16),
                   w_diff_branch[l].astype(BF16), w_out[l].astype(BF16))
        up, down = _prep_ffn_weights(w_ffn_up[l, 1], w_ffn_down[l, 1])
        x = _ffn(x, mod[2], up, down, final_w=final_norm if l == DEPTH - 1 else None)
    return x
```

```python
import functools
import math

import jax
import jax.numpy as jnp
from jax import lax
from jax.experimental import pallas as pl
from jax.experimental.pallas import tpu as pltpu

DEPTH = 4
D_MODEL = 1024
CHUNK = 64
CHUNK_SHIFT = 6
RET_HEADS = 4
RET_QK_DIM = 128
RET_V_DIM = 256
DIFF_HEADS = 8
DIFF_HEAD_DIM = 64
DIFF_V_DIM = 2 * DIFF_HEAD_DIM
D_FF = 2816
N_SUB = 3
NORM_EPS = 1e-6
IN_WIDTH = 8 * D_MODEL

F32 = jnp.float32
BF16 = jnp.bfloat16

V7X_VMEM_LIMIT_BYTES = 56 * 1024 * 1024

FFN_TM = 1024
PROJ_TM = 512
FFN_FC = 256
RET_L = 256
ATT_TQ = 512
ATT_TK = 512
ATT_EXP_ROWS = 16
ATT_TILES = 2
ATT_POS_LANES = 3
ATT_SUM_ROWS = 16
ATT_EXP2_ZERO = 160.0
NEG_BIG = -1e30
LOG2E = math.log2(math.e)


def _resident(shape):
    nd = len(shape)
    return pl.BlockSpec(shape, lambda *_: (0,) * nd, pipeline_mode=pl.Buffered(1))


def _resident_slice(shape, lead):
    n_lead, nd = len(lead), len(shape)
    index = tuple(lead) + (0,) * (nd - n_lead)
    return pl.BlockSpec((1,) * n_lead + tuple(shape[n_lead:]), lambda *_: index, pipeline_mode=pl.Buffered(1))


def _params(*sem, flags=None):
    return pltpu.CompilerParams(dimension_semantics=sem, vmem_limit_bytes=V7X_VMEM_LIMIT_BYTES, flags=flags)


def _adaln_kernel(c_ref, w_ref, b_ref, nw_ref, o_ref):
    j = pl.program_id(1)
    c = c_ref[...]
    cond = c / (1.0 + jnp.exp(-c))
    r = jnp.dot(cond, w_ref[0], preferred_element_type=F32,
                precision=lax.Precision.HIGHEST) + b_ref[0]
    kind = j % 3
    sub = j // 3
    r = jnp.where(kind == 1, (1.0 + r) * nw_ref[0, 0], r)
    r = jnp.where(jnp.logical_and(kind == 2, sub != 1), 0.5 * r, r)
    o_ref[0] = r


def _adaln(c, w_ada, b_ada, norm_w):
    B = c.shape[0]
    n_tiles = N_SUB * 3
    return pl.pallas_call(
        _adaln_kernel,
        grid=(DEPTH, n_tiles),
        in_specs=[
            pl.BlockSpec((B, D_MODEL), lambda l, j: (0, 0)),
            pl.BlockSpec((1, D_MODEL, D_MODEL), lambda l, j: (l, 0, j)),
            pl.BlockSpec((1, 1, D_MODEL), lambda l, j: (l, 0, j)),
            pl.BlockSpec((1, 1, 1, D_MODEL), lambda l, j: (l, j // 3, 0, 0)),
        ],
        out_specs=pl.BlockSpec((1, B, D_MODEL), lambda l, j: (l, 0, j)),
        out_shape=jax.ShapeDtypeStruct((DEPTH, B, n_tiles * D_MODEL), F32),
        compiler_params=_params("arbitrary", "arbitrary"),
        name="adaln_mod",
    )(c, w_ada, b_ada.reshape(DEPTH, 1, n_tiles * D_MODEL), norm_w.reshape(DEPTH, N_SUB, 1, D_MODEL))


def _modulated_norm(x, mod):
    ms = jnp.mean(x * x, axis=-1, keepdims=True)
    return x * lax.rsqrt(ms + NORM_EPS) * mod[1:2] + mod[0:1]


def _ffn_kernel(x_ref, mod_ref, wup_ref, wdn_ref, *rest, n_chunks, final):
    if final:
        fw_ref, o_ref = rest
    else:
        (o_ref,) = rest
    x = x_ref[0]
    mod = mod_ref[0]
    h = _modulated_norm(x, mod).astype(BF16)
    acc = jnp.zeros(x.shape, F32)
    for c in range(n_chunks):
        cols = slice(c * FFN_FC, (c + 1) * FFN_FC)
        a = jnp.dot(h, wup_ref[0, 0, :, cols], preferred_element_type=F32)
        b = jnp.dot(h, wup_ref[0, 0, :, D_FF + c * FFN_FC:D_FF + (c + 1) * FFN_FC], preferred_element_type=F32)
        act = (0.5 * a) * (1.0 + jnp.tanh(0.5 * a)) * b
        acc = acc + jnp.dot(act.astype(BF16), wdn_ref[0, 0, cols, :], preferred_element_type=F32)
    y = x + mod[2:3] * acc
    if final:
        ms = jnp.mean(y * y, axis=-1, keepdims=True)
        y = y * lax.rsqrt(ms + NORM_EPS) * fw_ref[...]
    o_ref[0] = y


def _ffn(x, mod, wup, wdn, layer, which, final_w=None):
    B, S, D = x.shape
    n_chunks = D_FF // FFN_FC
    final = final_w is not None
    in_specs = [
        pl.BlockSpec((1, FFN_TM, D), lambda b, t: (b, t, 0)),
        pl.BlockSpec((1, 3, D), lambda b, t: (b, 0, 0)),
        _resident_slice(wup.shape, (layer, which)),
        _resident_slice(wdn.shape, (layer, which)),
    ]
    args = [x, mod, wup, wdn]
    if final:
        in_specs.append(_resident((1, D)))
        args.append(final_w.reshape(1, D))
    return pl.pallas_call(
        functools.partial(_ffn_kernel, n_chunks=n_chunks, final=final),
        grid=(B, S // FFN_TM),
        in_specs=in_specs,
        out_specs=pl.BlockSpec((1, FFN_TM, D), lambda b, t: (b, t, 0)),
        out_shape=jax.ShapeDtypeStruct(x.shape, F32),
        compiler_params=_params("parallel", "parallel"),
        name="ffn_final" if final else "ffn",
    )(*args)


def _inproj_kernel(x_ref, mod_ref, w_ref, rq_ref, rk_ref, rv_ref, rg_ref, dq_ref, dk_ref, dv_ref,
                   gr_ref, gd_ref):
    x = x_ref[0]
    h = _modulated_norm(x, mod_ref[0]).astype(BF16)

    def proj(c):
        return jnp.dot(h, w_ref[0, :, c * D_MODEL:(c + 1) * D_MODEL], preferred_element_type=F32)

    u = proj(0).astype(BF16)
    for hh in range(RET_HEADS):
        rq_ref[0, hh] = u[:, hh * RET_QK_DIM:(hh + 1) * RET_QK_DIM]
        rk_ref[0, hh] = u[:, (RET_HEADS + hh) * RET_QK_DIM:(RET_HEADS + hh + 1) * RET_QK_DIM]
    u = proj(1).astype(BF16)
    for hh in range(RET_HEADS):
        rv_ref[0, hh] = u[:, hh * RET_V_DIM:(hh + 1) * RET_V_DIM]
    rg_ref[0] = proj(2).astype(BF16)
    u = (proj(3) * (DIFF_HEAD_DIM ** -0.5 * LOG2E)).astype(BF16)
    for hh in range(DIFF_HEADS):
        dq_ref[0, hh] = u[:, hh * DIFF_V_DIM:(hh + 1) * DIFF_V_DIM]
    u = proj(4).astype(BF16)
    for hh in range(DIFF_HEADS):
        dk_ref[0, hh] = u[:, hh * DIFF_V_DIM:(hh + 1) * DIFF_V_DIM]
    u = proj(5).astype(BF16)
    for hh in range(DIFF_HEADS):
        dv_ref[0, hh] = u[:, hh * DIFF_V_DIM:(hh + 1) * DIFF_V_DIM]
    gr_ref[0] = proj(6).astype(BF16)
    gd_ref[0] = proj(7).astype(BF16)


def _inproj(x, mod, w_in, layer):
    B, S, D = x.shape
    TM = PROJ_TM
    head_spec = lambda nh, w: pl.BlockSpec((1, nh, TM, w), lambda b, t: (b, 0, t, 0))
    tok_spec = pl.BlockSpec((1, TM, D), lambda b, t: (b, t, 0))
    sds = jax.ShapeDtypeStruct
    return pl.pallas_call(
        _inproj_kernel,
        grid=(B, S // TM),
        in_specs=[tok_spec, pl.BlockSpec((1, 3, D), lambda b, t: (b, 0, 0)),
                  _resident_slice(w_in.shape, (layer,))],
        out_specs=[
            head_spec(RET_HEADS, RET_QK_DIM), head_spec(RET_HEADS, RET_QK_DIM),
            head_spec(RET_HEADS, RET_V_DIM), tok_spec,
            head_spec(DIFF_HEADS, DIFF_V_DIM), head_spec(DIFF_HEADS, DIFF_V_DIM),
            head_spec(DIFF_HEADS, DIFF_V_DIM), tok_spec, tok_spec,
        ],
        out_shape=[
            sds((B, RET_HEADS, S, RET_QK_DIM), BF16), sds((B, RET_HEADS, S, RET_QK_DIM), BF16),
            sds((B, RET_HEADS, S, RET_V_DIM), BF16), sds((B, S, D), BF16),
            sds((B, DIFF_HEADS, S, DIFF_V_DIM), BF16), sds((B, DIFF_HEADS, S, DIFF_V_DIM), BF16),
            sds((B, DIFF_HEADS, S, DIFF_V_DIM), BF16), sds((B, S, D), BF16), sds((B, S, D), BF16),
        ],
        compiler_params=_params("parallel", "parallel"),
        name="mixer_inproj",
    )(x, mod, w_in)


def _retention_consts():
    L = RET_L
    gamma = 1.0 - 2.0 ** (-5.0 - jnp.arange(RET_HEADS, dtype=F32))
    log_g = jnp.log(gamma)
    r = jnp.arange(L, dtype=F32)
    scale = RET_QK_DIM ** -0.5
    allowed = (jnp.arange(L)[None, :] // CHUNK) <= (jnp.arange(L)[:, None] // CHUNK)
    intra = jnp.exp(log_g[:, None, None] * jnp.abs(r[:, None] - r[None, :]))
    intra = jnp.where(allowed[None], intra, 0.0) * scale
    q_dec = jnp.exp(log_g[:, None] * r[None, :]) * scale
    k_dec = jnp.exp(log_g[:, None] * (L - r)[None, :])
    blk_dec = jnp.exp(log_g * L)
    q_dec = jnp.broadcast_to(q_dec[:, :, None], (RET_HEADS, L, RET_QK_DIM))
    k_dec = jnp.broadcast_to(k_dec[:, :, None], (RET_HEADS, L, RET_QK_DIM))
    blk_dec = jnp.broadcast_to(blk_dec[:, None, None], (RET_HEADS, 1, RET_V_DIM))
    return intra, q_dec, k_dec, blk_dec


def _retention_kernel(q_ref, k_ref, v_ref, g_ref, w_ref, qd_ref, kd_ref, bd_ref, gn_ref, o_ref, state_ref):
    t = pl.program_id(1)

    @pl.when(t == 0)
    def _():
        state_ref[...] = jnp.zeros_like(state_ref)

    dv = v_ref.shape[-1]
    for h in range(q_ref.shape[1]):
        q = q_ref[0, h]
        k = k_ref[0, h]
        v = v_ref[0, h]
        s = lax.dot_general(q, k, (((1,), (1,)), ((), ())), preferred_element_type=F32)
        p = (s * w_ref[h]).astype(BF16)
        y = jnp.dot(p, v, preferred_element_type=F32)
        state = state_ref[h]
        qd = (q.astype(F32) * qd_ref[h]).astype(BF16)
        y = y + jnp.dot(qd, state.astype(BF16), preferred_element_type=F32)
        kd_t = (k.astype(F32) * kd_ref[h]).T.astype(BF16)
        state_ref[h] = state * bd_ref[h] + jnp.dot(kd_t, v, preferred_element_type=F32)
        ms = jnp.mean(y * y, axis=-1, keepdims=True)
        y = y * lax.rsqrt(ms + NORM_EPS) * gn_ref[h]
        g = g_ref[0, :, h * dv:(h + 1) * dv].astype(F32)
        o_ref[0, :, h * dv:(h + 1) * dv] = (y * (0.5 * g) * (1.0 + jnp.tanh(0.5 * g))).astype(BF16)


def _retention(rq, rk, rv, rg, ret_gn_l):
    B, H, S, dk = rq.shape
    dv = rv.shape[-1]
    L = RET_L
    intra, q_dec, k_dec, blk_dec = _retention_consts()
    blk = lambda w: pl.BlockSpec((1, H, L, w), lambda b, t: (b, 0, t, 0))
    tok = pl.BlockSpec((1, L, H * dv), lambda b, t: (b, t, 0))
    return pl.pallas_call(
        _retention_kernel,
        grid=(B, S // L),
        in_specs=[
            blk(dk), blk(dk), blk(dv), tok,
            _resident((H, L, L)), _resident((H, L, dk)), _resident((H, L, dk)), _resident((H, 1, dv)),
            _resident((H, 1, dv)),
        ],
        out_specs=tok,
        out_shape=jax.ShapeDtypeStruct((B, S, H * dv), BF16),
        scratch_shapes=[pltpu.VMEM((H, dk, dv), F32)],
        compiler_params=_params("parallel", "arbitrary"),
        name="retention",
    )(rq, rk, rv, rg, intra, q_dec, k_dec, blk_dec, ret_gn_l.reshape(H, 1, dv))


def _attn_bias_tables(n_heads):
    T = ATT_TQ
    slopes = 2.0 ** (-8.0 * jnp.arange(1, n_heads + 1, dtype=F32) / n_heads) * LOG2E
    r = jnp.arange(T, dtype=F32)
    a, c = r[:, None], r[None, :]
    allowed = (jnp.arange(T)[:, None] // CHUNK) <= (jnp.arange(T)[None, :] // CHUNK)
    diag = slopes[:, None, None] * (-2.0 * jnp.maximum(a - c, 0.0))[None]
    diag = jnp.where(allowed[None], diag, NEG_BIG)
    diag = jnp.concatenate([diag, diag], axis=-1)
    p0 = slopes.astype(BF16)
    p1 = (slopes - p0.astype(F32)).astype(BF16)
    p2 = (slopes - p0.astype(F32) - p1.astype(F32)).astype(BF16)
    parts = jnp.stack([p0, p1, p2], axis=1).astype(F32)
    coef = jnp.concatenate([parts * CHUNK, parts, jnp.zeros((n_heads, 128 - 2 * ATT_POS_LANES), F32)], axis=1)
    return slopes, diag, coef.astype(BF16).reshape(n_heads, 1, 128)


def _diffattn_kernel(slope_ref, q_ref, k_ref, v_ref, db_ref, coef_ref, lam_ref, sub_ref, o_ref,
                     vt_ref, kaug_ref, qbd_ref, s_ref, p_ref, bmax_ref, alpha_ref, m_ref, knorm_ref, acc_ref, *,
                     lam_init):
    h = pl.program_id(1)
    g = pl.program_id(2)
    TQ, TK = ATT_TQ, ATT_TK
    n_kv = vt_ref.shape[0]

    def max_sq_norms(x):
        xsq = jnp.square(x.astype(F32))
        first = lax.broadcasted_iota(jnp.int32, xsq.shape, 1) < DIFF_HEAD_DIM
        n1 = jnp.sum(jnp.where(first, xsq, 0.0), axis=1, keepdims=True)
        n2 = jnp.sum(jnp.where(first, 0.0, xsq), axis=1, keepdims=True)
        return jnp.max(n1, axis=0, keepdims=True), jnp.max(n2, axis=0, keepdims=True)

    @pl.when(g == 0)
    def _():
        k1sq = jnp.zeros((1, 1), F32)
        k2sq = jnp.zeros((1, 1), F32)
        for c in range(n_kv):
            vt_ref[c, 0:DIFF_V_DIM] = v_ref[0, 0, c * TK:(c + 1) * TK, :].astype(F32).T.astype(BF16)
            ones_row = lax.broadcasted_iota(jnp.int32, (ATT_SUM_ROWS, TK), 0) == 0
            vt_ref[c, DIFF_V_DIM:DIFF_V_DIM + ATT_SUM_ROWS] = ones_row.astype(F32).astype(BF16)
            k = k_ref[0, 0, c * TK:(c + 1) * TK, :]
            c1, c2 = max_sq_norms(k)
            k1sq, k2sq = jnp.maximum(k1sq, c1), jnp.maximum(k2sq, c2)
            pos = c * TK + lax.broadcasted_iota(jnp.int32, k.shape, 0)
            lane = lax.broadcasted_iota(jnp.int32, k.shape, 1)
            feat = jnp.where(lane < ATT_POS_LANES, jnp.right_shift(pos, CHUNK_SHIFT),
                             jnp.where(lane < 2 * ATT_POS_LANES, jnp.bitwise_and(pos, CHUNK - 1), 0))
            kaug_ref[c * TK:(c + 1) * TK, 0:2 * DIFF_HEAD_DIM] = k
            kaug_ref[c * TK:(c + 1) * TK, 2 * DIFF_HEAD_DIM:] = feat.astype(F32).astype(BF16)
        knorm_ref[0:1] = jnp.broadcast_to(k1sq, (1, 128))
        knorm_ref[1:2] = jnp.broadcast_to(k2sq, (1, 128))

    slope = slope_ref[h]
    tiles = range(ATT_TILES)
    tile_idx = [g * ATT_TILES + x for x in tiles]
    for x in tiles:
        q = q_ref[0, 0, x * TQ:(x + 1) * TQ, :]
        lane = lax.broadcasted_iota(jnp.int32, q.shape, 1)
        zero = jnp.zeros_like(q)
        qbd_ref[x, 0:TQ, 0:2 * DIFF_HEAD_DIM] = jnp.where(lane < DIFF_HEAD_DIM, q, zero)
        qbd_ref[x, TQ:2 * TQ, 0:2 * DIFF_HEAD_DIM] = jnp.where(lane >= DIFF_HEAD_DIM, q, zero)
        qbd_ref[x, :, 2 * DIFF_HEAD_DIM:] = jnp.broadcast_to(coef_ref[0], (2 * TQ, 128))
        m_ref[x] = jnp.full(m_ref.shape[1:], NEG_BIG, F32)
        acc_ref[x] = jnp.zeros(acc_ref.shape[1:], F32)

    def block_of(x, t):
        return jnp.maximum(tile_idx[x] - t, 0)

    def stage_scores(x, t, slot, diagonal):
        k = kaug_ref[pl.ds(pl.multiple_of(block_of(x, t) * TK, TK), TK), :]
        s = lax.dot_general(k, qbd_ref[x], (((1,), (1,)), ((), ())), preferred_element_type=F32)
        bmax = None
        for r in range(0, TK, ATT_EXP_ROWS):
            rows = slice(r, r + ATT_EXP_ROWS)
            u = s[rows] + db_ref[0, rows] if diagonal else s[rows]
            s_ref[x, slot, rows] = u
            cmax = jnp.max(u.reshape(ATT_EXP_ROWS // 8, 8, 2 * TQ), axis=0)
            bmax = cmax if bmax is None else jnp.maximum(bmax, cmax)
        bmax_ref[x, slot] = jnp.max(bmax, axis=0, keepdims=True)

    def stage_softmax(x, t, slot, diagonal):
        if diagonal:
            shift = 0.0
        else:
            shift = jnp.where(t < n_steps[x], 0.0, NEG_BIG)
        m = m_ref[x]
        m_new = jnp.maximum(m, bmax_ref[x, slot] + shift)
        alpha = jnp.exp2(m - m_new)
        m_ref[x] = m_new
        alpha_ref[x, slot] = alpha
        ref = m_new - shift
        for r in range(0, TK, ATT_EXP_ROWS):
            rows = slice(r, r + ATT_EXP_ROWS)
            p_ref[x, slot, rows] = jnp.exp2(s_ref[x, slot, rows] - ref).astype(BF16)

    def stage_pv(x, t, slot):
        pv = jnp.dot(vt_ref[block_of(x, t)], p_ref[x, slot], preferred_element_type=F32)
        acc_ref[x] = alpha_ref[x, slot] * acc_ref[x] + pv

    for x in tiles:
        stage_scores(x, 0, 0, True)
    for x in tiles:
        stage_softmax(x, 0, 0, True)
    for x in tiles:
        stage_scores(x, 1, 1, False)

    n_steps = []
    for x in tiles:
        q1sq, q2sq = max_sq_norms(q_ref[0, 0, x * TQ:(x + 1) * TQ, :])
        bound = jnp.sqrt(jnp.maximum(q1sq * knorm_ref[0:1, 0:1], q2sq * knorm_ref[1:2, 0:1])) * 1.001
        m_min = jnp.min(m_ref[x], axis=1, keepdims=True)
        idx = tile_idx[x].astype(F32)
        reach = idx + (ATT_EXP2_ZERO + bound - m_min) / (slope * TK)
        n = jnp.minimum(jnp.floor(reach) + 2.0, idx + 1.0)
        n_steps.append(jnp.max(n).astype(jnp.int32))

    def pipeline_tick(t, slot):
        for x in tiles:
            stage_pv(x, t, slot)
        for x in tiles:
            stage_softmax(x, t + 1, 1 - slot, False)
        for x in tiles:
            stage_scores(x, t + 2, slot, False)

    def pipeline_drain(t, slot):
        for x in tiles:
            stage_pv(x, t, slot)
        for x in tiles:
            stage_softmax(x, t + 1, 1 - slot, False)
        for x in tiles:
            stage_pv(x, t + 1, 1 - slot)

    def by_parity(fn, t):
        for slot in (0, 1):
            @pl.when(jnp.bitwise_and(t, 1) == slot)
            def _():
                fn(t, slot)

    def body(t, carry):
        by_parity(pipeline_tick, t)
        return carry

    n_ticks = jnp.maximum(functools.reduce(jnp.maximum, n_steps), 2)
    lax.fori_loop(0, n_ticks - 2, body, 0)
    by_parity(pipeline_drain, n_ticks - 2)
    lp = lam_ref[...]
    lam = (jnp.exp(jnp.sum(lp[0:1] * lp[1:2], axis=-1, keepdims=True))
           - jnp.exp(jnp.sum(lp[2:3] * lp[3:4], axis=-1, keepdims=True)) + lam_init)
    for x in tiles:
        o = acc_ref[x, 0:DIFF_V_DIM] / acc_ref[x, DIFF_V_DIM:DIFF_V_DIM + 1]
        y = (o[:, :TQ] - lam * o[:, TQ:]).T
        ms = jnp.mean(y * y, axis=-1, keepdims=True)
        y = y * lax.rsqrt(ms + NORM_EPS) * sub_ref[...] * (1.0 - lam_init)
        o_ref[0, x * TQ:(x + 1) * TQ] = y.astype(BF16)


def _diffattn(dq, dk, dv, lam_params, subln, layer_idx):
    B, H, S, w = dq.shape
    lam_init = 0.8 - 0.6 * math.exp(-0.3 * layer_idx)
    slopes, diag_bias, coef = _attn_bias_tables(H)
    NT, TQS = ATT_TILES, ATT_TILES * ATT_TQ
    grid_spec = pltpu.PrefetchScalarGridSpec(
        num_scalar_prefetch=1,
        grid=(B, H, S // TQS),
        in_specs=[
            pl.BlockSpec((1, 1, TQS, w), lambda b, h, i, sl: (b, h, i, 0)),
            pl.BlockSpec((1, 1, S, w), lambda b, h, i, sl: (b, h, 0, 0)),
            pl.BlockSpec((1, 1, S, w), lambda b, h, i, sl: (b, h, 0, 0)),
            pl.BlockSpec((1, ATT_TK, 2 * ATT_TQ), lambda b, h, i, sl: (h, 0, 0)),
            pl.BlockSpec((1, 1, 128), lambda b, h, i, sl: (h, 0, 0)),
            pl.BlockSpec((4, DIFF_HEAD_DIM), lambda b, h, i, sl: (0, 0)),
            pl.BlockSpec((1, w), lambda b, h, i, sl: (0, 0)),
        ],
        out_specs=pl.BlockSpec((1, TQS, w), lambda b, h, i, sl: (b, i, h)),
        scratch_shapes=[
            pltpu.VMEM((S // ATT_TK, w + ATT_SUM_ROWS, ATT_TK), BF16),
            pltpu.VMEM((S, 2 * w), BF16),
            pltpu.VMEM((NT, 2 * ATT_TQ, 2 * w), BF16),
            pltpu.VMEM((NT, 2, ATT_TK, 2 * ATT_TQ), F32),
            pltpu.VMEM((NT, 2, ATT_TK, 2 * ATT_TQ), BF16),
            pltpu.VMEM((NT, 2, 1, 2 * ATT_TQ), F32),
            pltpu.VMEM((NT, 2, 1, 2 * ATT_TQ), F32),
            pltpu.VMEM((NT, 1, 2 * ATT_TQ), F32),
            pltpu.VMEM((8, 128), F32),
            pltpu.VMEM((NT, w + ATT_SUM_ROWS, 2 * ATT_TQ), F32),
        ],
    )
    return pl.pallas_call(
        functools.partial(_diffattn_kernel, lam_init=lam_init),
        grid_spec=grid_spec,
        out_shape=jax.ShapeDtypeStruct((B, S, H * w), BF16),
        compiler_params=_params("parallel", "parallel", "arbitrary"),
        name="diff_attention",
    )(slopes, dq, dk, dv, diag_bias, coef, lam_params, subln.reshape(1, w))


def _merge_kernel(x_ref, mod_ref, yr_ref, yd_ref, gr_ref, gd_ref, wr_ref, wd_ref, wo_ref, o_ref):
    def sigmoid(g):
        return 0.5 * (1.0 + jnp.tanh(0.5 * g))

    br = jnp.dot(yr_ref[0], wr_ref[0], preferred_element_type=F32)
    bd = jnp.dot(yd_ref[0], wd_ref[0], preferred_element_type=F32)
    merged = sigmoid(gr_ref[0].astype(F32)) * br + sigmoid(gd_ref[0].astype(F32)) * bd
    out = jnp.dot(merged.astype(BF16), wo_ref[0], preferred_element_type=F32)
    o_ref[0] = x_ref[0] + mod_ref[0][2:3] * out


def _merge(x, mod, y_ret, y_diff, g_ret, g_diff, w_rb, w_db, w_o, layer):
    B, S, D = x.shape
    tok = pl.BlockSpec((1, PROJ_TM, D), lambda b, t: (b, t, 0))
    return pl.pallas_call(
        _merge_kernel,
        grid=(B, S // PROJ_TM),
        in_specs=[tok, pl.BlockSpec((1, 3, D), lambda b, t: (b, 0, 0)), tok, tok, tok, tok,
                  _resident_slice(w_rb.shape, (layer,)), _resident_slice(w_db.shape, (layer,)),
                  _resident_slice(w_o.shape, (layer,))],
        out_specs=tok,
        out_shape=jax.ShapeDtypeStruct(x.shape, F32),
        compiler_params=_params("parallel", "parallel"),
        name="mixer_merge",
    )(x, mod, y_ret, y_diff, g_ret, g_diff, w_rb, w_db, w_o)


def kernel(x, c, w_ada, b_ada, norm_w, w_ffn_up, w_ffn_down, w_in, ret_gn, lambda_q1, lambda_k1, lambda_q2,
           lambda_k2, diff_subln, w_ret_branch, w_diff_branch, w_out, final_norm):
    B, S, D = x.shape
    assert D == D_MODEL and S % max(FFN_TM, RET_L, ATT_TILES * ATT_TQ) == 0 and ATT_TQ == ATT_TK
    mod_all = _adaln(c, w_ada, b_ada, norm_w).reshape(DEPTH, B, N_SUB, 3, D)
    up, down, w_in_b = w_ffn_up.astype(BF16), w_ffn_down.astype(BF16), w_in.astype(BF16)
    w_rb, w_db, w_o = w_ret_branch.astype(BF16), w_diff_branch.astype(BF16), w_out.astype(BF16)
    for l in range(DEPTH):
        mod = [mod_all[l, :, s] for s in range(N_SUB)]
        x = _ffn(x, mod[0], up, down, l, 0)
        rq, rk, rv, rg, dq, dk, dv, g_ret, g_diff = _inproj(x, mod[1], w_in_b, l)
        y_ret = _retention(rq, rk, rv, rg, ret_gn[l])
        lam_params = jnp.stack([lambda_q1[l], lambda_k1[l], lambda_q2[l], lambda_k2[l]])
        y_diff = _diffattn(dq, dk, dv, lam_params, diff_subln[l], l)
        x = _merge(x, mod[1], y_ret, y_diff, g_ret, g_diff, w_rb, w_db, w_o, l)
        x = _ffn(x, mod[2], up, down, l, 1, final_w=final_norm if l == DEPTH - 1 else None)
    return x
```

```python
import functools
import math

import jax
import jax.numpy as jnp
from jax import lax
from jax.experimental import pallas as pl
from jax.experimental.pallas import tpu as pltpu

DEPTH = 4
D_MODEL = 1024
CHUNK = 64
CHUNK_SHIFT = 6
RET_HEADS = 4
RET_QK_DIM = 128
RET_V_DIM = 256
DIFF_HEADS = 8
DIFF_HEAD_DIM = 64
DIFF_V_DIM = 2 * DIFF_HEAD_DIM
D_FF = 2816
N_SUB = 3
NORM_EPS = 1e-6
IN_WIDTH = 8 * D_MODEL

F32 = jnp.float32
BF16 = jnp.bfloat16

V7X_VMEM_LIMIT_BYTES = 56 * 1024 * 1024

FFN_TM = 1024
PROJ_TM = 512
FFN_FC = 256
RET_L = 256
ATT_TQ = 512
ATT_TK = 512
ATT_EXP_ROWS = 16
ATT_TILES = 2
ATT_POS_LANES = 3
ATT_SUM_ROWS = 16
ATT_EXP2_ZERO = 160.0
NEG_BIG = -1e30
LOG2E = math.log2(math.e)


def _resident(shape):
    nd = len(shape)
    return pl.BlockSpec(shape, lambda *_: (0,) * nd, pipeline_mode=pl.Buffered(1))


def _resident_slice(shape, lead):
    n_lead, nd = len(lead), len(shape)
    index = tuple(lead) + (0,) * (nd - n_lead)
    return pl.BlockSpec((1,) * n_lead + tuple(shape[n_lead:]), lambda *_: index, pipeline_mode=pl.Buffered(1))


def _params(*sem, flags=None):
    return pltpu.CompilerParams(dimension_semantics=sem, vmem_limit_bytes=V7X_VMEM_LIMIT_BYTES, flags=flags)


def _adaln_kernel(c_ref, w_ref, b_ref, nw_ref, o_ref):
    j = pl.program_id(1)
    c = c_ref[...]
    cond = c / (1.0 + jnp.exp(-c))
    r = jnp.dot(cond, w_ref[0], preferred_element_type=F32,
                precision=lax.Precision.HIGHEST) + b_ref[0]
    kind = j % 3
    sub = j // 3
    r = jnp.where(kind == 1, (1.0 + r) * nw_ref[0, 0], r)
    r = jnp.where(jnp.logical_and(kind == 2, sub != 1), 0.5 * r, r)
    o_ref[0] = r


def _adaln(c, w_ada, b_ada, norm_w):
    B = c.shape[0]
    n_tiles = N_SUB * 3
    return pl.pallas_call(
        _adaln_kernel,
        grid=(DEPTH, n_tiles),
        in_specs=[
            pl.BlockSpec((B, D_MODEL), lambda l, j: (0, 0)),
            pl.BlockSpec((1, D_MODEL, D_MODEL), lambda l, j: (l, 0, j)),
            pl.BlockSpec((1, 1, D_MODEL), lambda l, j: (l, 0, j)),
            pl.BlockSpec((1, 1, 1, D_MODEL), lambda l, j: (l, j // 3, 0, 0)),
        ],
        out_specs=pl.BlockSpec((1, B, D_MODEL), lambda l, j: (l, 0, j)),
        out_shape=jax.ShapeDtypeStruct((DEPTH, B, n_tiles * D_MODEL), F32),
        compiler_params=_params("arbitrary", "arbitrary"),
        name="adaln_mod",
    )(c, w_ada, b_ada.reshape(DEPTH, 1, n_tiles * D_MODEL), norm_w.reshape(DEPTH, N_SUB, 1, D_MODEL))


def _modulated_norm(x, mod):
    ms = jnp.mean(x * x, axis=-1, keepdims=True)
    return x * lax.rsqrt(ms + NORM_EPS) * mod[1:2] + mod[0:1]


def _ffn_kernel(x_ref, mod_ref, wup_ref, wdn_ref, *rest, n_chunks, final):
    if final:
        fw_ref, o_ref = rest
    else:
        (o_ref,) = rest
    x = x_ref[0]
    mod = mod_ref[0]
    h = _modulated_norm(x, mod).astype(BF16)
    acc = jnp.zeros(x.shape, F32)
    for c in range(n_chunks):
        cols = slice(c * FFN_FC, (c + 1) * FFN_FC)
        a = jnp.dot(h, wup_ref[0, 0, :, cols], preferred_element_type=F32)
        b = jnp.dot(h, wup_ref[0, 0, :, D_FF + c * FFN_FC:D_FF + (c + 1) * FFN_FC], preferred_element_type=F32)
        act = (0.5 * a) * (1.0 + jnp.tanh(0.5 * a)) * b
        acc = acc + jnp.dot(act.astype(BF16), wdn_ref[0, 0, cols, :], preferred_element_type=F32)
    y = x + mod[2:3] * acc
    if final:
        ms = jnp.mean(y * y, axis=-1, keepdims=True)
        y = y * lax.rsqrt(ms + NORM_EPS) * fw_ref[...]
    o_ref[0] = y


def _ffn(x, mod, wup, wdn, layer, which, final_w=None):
    B, S, D = x.shape
    n_chunks = D_FF // FFN_FC
    final = final_w is not None
    in_specs = [
        pl.BlockSpec((1, FFN_TM, D), lambda b, t: (b, t, 0)),
        pl.BlockSpec((1, 3, D), lambda b, t: (b, 0, 0)),
        _resident_slice(wup.shape, (layer, which)),
        _resident_slice(wdn.shape, (layer, which)),
    ]
    args = [x, mod, wup, wdn]
    if final:
        in_specs.append(_resident((1, D)))
        args.append(final_w.reshape(1, D))
    return pl.pallas_call(
        functools.partial(_ffn_kernel, n_chunks=n_chunks, final=final),
        grid=(B, S // FFN_TM),
        in_specs=in_specs,
        out_specs=pl.BlockSpec((1, FFN_TM, D), lambda b, t: (b, t, 0)),
        out_shape=jax.ShapeDtypeStruct(x.shape, F32),
        compiler_params=_params("parallel", "parallel"),
        name="ffn_final" if final else "ffn",
    )(*args)


def _inproj_kernel(x_ref, mod_ref, w_ref, rq_ref, rk_ref, rv_ref, rg_ref, dq_ref, dk_ref, dv_ref,
                   gr_ref, gd_ref):
    x = x_ref[0]
    h = _modulated_norm(x, mod_ref[0]).astype(BF16)

    def proj(c):
        return jnp.dot(h, w_ref[0, :, c * D_MODEL:(c + 1) * D_MODEL], preferred_element_type=F32)

    u = proj(0).astype(BF16)
    for hh in range(RET_HEADS):
        rq_ref[0, hh] = u[:, hh * RET_QK_DIM:(hh + 1) * RET_QK_DIM]
        rk_ref[0, hh] = u[:, (RET_HEADS + hh) * RET_QK_DIM:(RET_HEADS + hh + 1) * RET_QK_DIM]
    u = proj(1).astype(BF16)
    for hh in range(RET_HEADS):
        rv_ref[0, hh] = u[:, hh * RET_V_DIM:(hh + 1) * RET_V_DIM]
    rg_ref[0] = proj(2).astype(BF16)
    u = (proj(3) * (DIFF_HEAD_DIM ** -0.5 * LOG2E)).astype(BF16)
    for hh in range(DIFF_HEADS):
        dq_ref[0, hh] = u[:, hh * DIFF_V_DIM:(hh + 1) * DIFF_V_DIM]
    u = proj(4).astype(BF16)
    for hh in range(DIFF_HEADS):
        dk_ref[0, hh] = u[:, hh * DIFF_V_DIM:(hh + 1) * DIFF_V_DIM]
    u = proj(5).astype(BF16)
    for hh in range(DIFF_HEADS):
        dv_ref[0, hh] = u[:, hh * DIFF_V_DIM:(hh + 1) * DIFF_V_DIM]
    gr_ref[0] = proj(6).astype(BF16)
    gd_ref[0] = proj(7).astype(BF16)


def _inproj(x, mod, w_in, layer):
    B, S, D = x.shape
    TM = PROJ_TM
    head_spec = lambda nh, w: pl.BlockSpec((1, nh, TM, w), lambda b, t: (b, 0, t, 0))
    tok_spec = pl.BlockSpec((1, TM, D), lambda b, t: (b, t, 0))
    sds = jax.ShapeDtypeStruct
    return pl.pallas_call(
        _inproj_kernel,
        grid=(B, S // TM),
        in_specs=[tok_spec, pl.BlockSpec((1, 3, D), lambda b, t: (b, 0, 0)),
                  _resident_slice(w_in.shape, (layer,))],
        out_specs=[
            head_spec(RET_HEADS, RET_QK_DIM), head_spec(RET_HEADS, RET_QK_DIM),
            head_spec(RET_HEADS, RET_V_DIM), tok_spec,
            head_spec(DIFF_HEADS, DIFF_V_DIM), head_spec(DIFF_HEADS, DIFF_V_DIM),
            head_spec(DIFF_HEADS, DIFF_V_DIM), tok_spec, tok_spec,
        ],
        out_shape=[
            sds((B, RET_HEADS, S, RET_QK_DIM), BF16), sds((B, RET_HEADS, S, RET_QK_DIM), BF16),
            sds((B, RET_HEADS, S, RET_V_DIM), BF16), sds((B, S, D), BF16),
            sds((B, DIFF_HEADS, S, DIFF_V_DIM), BF16), sds((B, DIFF_HEADS, S, DIFF_V_DIM), BF16),
            sds((B, DIFF_HEADS, S, DIFF_V_DIM), BF16), sds((B, S, D), BF16), sds((B, S, D), BF16),
        ],
        compiler_params=_params("parallel", "parallel"),
        name="mixer_inproj",
    )(x, mod, w_in)


def _retention_consts():
    L = RET_L
    gamma = 1.0 - 2.0 ** (-5.0 - jnp.arange(RET_HEADS, dtype=F32))
    log_g = jnp.log(gamma)
    r = jnp.arange(L, dtype=F32)
    scale = RET_QK_DIM ** -0.5
    allowed = (jnp.arange(L)[None, :] // CHUNK) <= (jnp.arange(L)[:, None] // CHUNK)
    intra = jnp.exp(log_g[:, None, None] * jnp.abs(r[:, None] - r[None, :]))
    intra = jnp.where(allowed[None], intra, 0.0) * scale
    q_dec = jnp.exp(log_g[:, None] * r[None, :]) * scale
    k_dec = jnp.exp(log_g[:, None] * (L - r)[None, :])
    blk_dec = jnp.exp(log_g * L)
    q_dec = jnp.broadcast_to(q_dec[:, :, None], (RET_HEADS, L, RET_QK_DIM))
    k_dec = jnp.broadcast_to(k_dec[:, :, None], (RET_HEADS, L, RET_QK_DIM))
    blk_dec = jnp.broadcast_to(blk_dec[:, None, None], (RET_HEADS, 1, RET_V_DIM))
    return intra, q_dec, k_dec, blk_dec


def _retention_kernel(q_ref, k_ref, v_ref, g_ref, w_ref, qd_ref, kd_ref, bd_ref, gn_ref, o_ref, state_ref):
    t = pl.program_id(1)

    @pl.when(t == 0)
    def _():
        state_ref[...] = jnp.zeros_like(state_ref)

    dv = v_ref.shape[-1]
    for h in range(q_ref.shape[1]):
        q = q_ref[0, h]
        k = k_ref[0, h]
        v = v_ref[0, h]
        s = lax.dot_general(q, k, (((1,), (1,)), ((), ())), preferred_element_type=F32)
        p = (s * w_ref[h]).astype(BF16)
        y = jnp.dot(p, v, preferred_element_type=F32)
        state = state_ref[h]
        qd = (q.astype(F32) * qd_ref[h]).astype(BF16)
        y = y + jnp.dot(qd, state.astype(BF16), preferred_element_type=F32)
        kd_t = (k.astype(F32) * kd_ref[h]).T.astype(BF16)
        state_ref[h] = state * bd_ref[h] + jnp.dot(kd_t, v, preferred_element_type=F32)
        ms = jnp.mean(y * y, axis=-1, keepdims=True)
        y = y * lax.rsqrt(ms + NORM_EPS) * gn_ref[h]
        g = g_ref[0, :, h * dv:(h + 1) * dv].astype(F32)
        o_ref[0, :, h * dv:(h + 1) * dv] = (y * (0.5 * g) * (1.0 + jnp.tanh(0.5 * g))).astype(BF16)


def _retention(rq, rk, rv, rg, ret_gn_l):
    B, H, S, dk = rq.shape
    dv = rv.shape[-1]
    L = RET_L
    intra, q_dec, k_dec, blk_dec = _retention_consts()
    blk = lambda w: pl.BlockSpec((1, H, L, w), lambda b, t: (b, 0, t, 0))
    tok = pl.BlockSpec((1, L, H * dv), lambda b, t: (b, t, 0))
    return pl.pallas_call(
        _retention_kernel,
        grid=(B, S // L),
        in_specs=[
            blk(dk), blk(dk), blk(dv), tok,
            _resident((H, L, L)), _resident((H, L, dk)), _resident((H, L, dk)), _resident((H, 1, dv)),
            _resident((H, 1, dv)),
        ],
        out_specs=tok,
        out_shape=jax.ShapeDtypeStruct((B, S, H * dv), BF16),
        scratch_shapes=[pltpu.VMEM((H, dk, dv), F32)],
        compiler_params=_params("parallel", "arbitrary"),
        name="retention",
    )(rq, rk, rv, rg, intra, q_dec, k_dec, blk_dec, ret_gn_l.reshape(H, 1, dv))


def _attn_bias_tables(n_heads):
    T = ATT_TQ
    slopes = 2.0 ** (-8.0 * jnp.arange(1, n_heads + 1, dtype=F32) / n_heads) * LOG2E
    r = jnp.arange(T, dtype=F32)
    a, c = r[:, None], r[None, :]
    allowed = (jnp.arange(T)[:, None] // CHUNK) <= (jnp.arange(T)[None, :] // CHUNK)
    diag = slopes[:, None, None] * (-2.0 * jnp.maximum(a - c, 0.0))[None]
    diag = jnp.where(allowed[None], diag, NEG_BIG)
    diag = jnp.concatenate([diag, diag], axis=-1)
    p0 = slopes.astype(BF16)
    p1 = (slopes - p0.astype(F32)).astype(BF16)
    p2 = (slopes - p0.astype(F32) - p1.astype(F32)).astype(BF16)
    parts = jnp.stack([p0, p1, p2], axis=1).astype(F32)
    coef = jnp.concatenate([parts * CHUNK, parts, jnp.zeros((n_heads, 128 - 2 * ATT_POS_LANES), F32)], axis=1)
    return slopes, diag, coef.astype(BF16).reshape(n_heads, 1, 128)


def _diffattn_kernel(slope_ref, q_ref, k_ref, v_ref, db_ref, coef_ref, lam_ref, sub_ref, o_ref,
                     vt_ref, kaug_ref, qbd_ref, s_ref, p_ref, bmax_ref, alpha_ref, m_ref, knorm_ref, acc_ref, *,
                     lam_init):
    h = pl.program_id(1)
    g = pl.program_id(2)
    TQ, TK = ATT_TQ, ATT_TK
    n_kv = vt_ref.shape[0]

    def max_sq_norms(x):
        xsq = jnp.square(x.astype(F32))
        first = lax.broadcasted_iota(jnp.int32, xsq.shape, 1) < DIFF_HEAD_DIM
        n1 = jnp.sum(jnp.where(first, xsq, 0.0), axis=1, keepdims=True)
        n2 = jnp.sum(jnp.where(first, 0.0, xsq), axis=1, keepdims=True)
        return jnp.max(n1, axis=0, keepdims=True), jnp.max(n2, axis=0, keepdims=True)

    @pl.when(jnp.logical_and(jnp.logical_and(pl.program_id(0) == 0, h == 0), g == 0))
    def _():
        for c in range(n_kv):
            ones_row = lax.broadcasted_iota(jnp.int32, (ATT_SUM_ROWS, TK), 0) == 0
            vt_ref[c, DIFF_V_DIM:DIFF_V_DIM + ATT_SUM_ROWS] = ones_row.astype(F32).astype(BF16)
            pos = c * TK + lax.broadcasted_iota(jnp.int32, (TK, 128), 0)
            lane = lax.broadcasted_iota(jnp.int32, (TK, 128), 1)
            feat = jnp.where(lane < ATT_POS_LANES, jnp.right_shift(pos, CHUNK_SHIFT),
                             jnp.where(lane < 2 * ATT_POS_LANES, jnp.bitwise_and(pos, CHUNK - 1), 0))
            kaug_ref[c * TK:(c + 1) * TK, 2 * DIFF_HEAD_DIM:] = feat.astype(F32).astype(BF16)

    @pl.when(g == 0)
    def _():
        k1sq = jnp.zeros((1, TK), F32)
        k2sq = jnp.zeros((1, TK), F32)
        for c in range(n_kv):
            vt_ref[c, 0:DIFF_V_DIM] = v_ref[0, 0, c * TK:(c + 1) * TK, :].astype(F32).T.astype(BF16)
            k = k_ref[0, 0, c * TK:(c + 1) * TK, :]
            kaug_ref[c * TK:(c + 1) * TK, 0:2 * DIFF_HEAD_DIM] = k
            ksq_t = jnp.square(k.astype(F32).T)
            k1sq = jnp.maximum(k1sq, jnp.sum(ksq_t[0:DIFF_HEAD_DIM], axis=0, keepdims=True))
            k2sq = jnp.maximum(k2sq, jnp.sum(ksq_t[DIFF_HEAD_DIM:], axis=0, keepdims=True))
        knorm_ref[0:1] = jnp.broadcast_to(jnp.max(k1sq, axis=1, keepdims=True), (1, 128))
        knorm_ref[1:2] = jnp.broadcast_to(jnp.max(k2sq, axis=1, keepdims=True), (1, 128))

    slope = slope_ref[h]
    tiles = range(ATT_TILES)
    tile_idx = [g * ATT_TILES + x for x in tiles]
    for x in tiles:
        q = q_ref[0, 0, x * TQ:(x + 1) * TQ, :]
        lane = lax.broadcasted_iota(jnp.int32, q.shape, 1)
        zero = jnp.zeros_like(q)
        qbd_ref[x, 0:TQ, 0:2 * DIFF_HEAD_DIM] = jnp.where(lane < DIFF_HEAD_DIM, q, zero)
        qbd_ref[x, TQ:2 * TQ, 0:2 * DIFF_HEAD_DIM] = jnp.where(lane >= DIFF_HEAD_DIM, q, zero)
        qbd_ref[x, :, 2 * DIFF_HEAD_DIM:] = jnp.broadcast_to(coef_ref[0], (2 * TQ, 128))
        m_ref[x] = jnp.full(m_ref.shape[1:], NEG_BIG, F32)
        acc_ref[x] = jnp.zeros(acc_ref.shape[1:], F32)

    def block_of(x, t):
        return jnp.maximum(tile_idx[x] - t, 0)

    def stage_scores(x, t, slot, diagonal):
        k = kaug_ref[pl.ds(pl.multiple_of(block_of(x, t) * TK, TK), TK), :]
        s = lax.dot_general(k, qbd_ref[x], (((1,), (1,)), ((), ())), preferred_element_type=F32)
        bmax = None
        for r in range(0, TK, ATT_EXP_ROWS):
            rows = slice(r, r + ATT_EXP_ROWS)
            u = s[rows] + db_ref[0, rows] if diagonal else s[rows]
            s_ref[x, slot, rows] = u
            cmax = jnp.max(u.reshape(ATT_EXP_ROWS // 8, 8, 2 * TQ), axis=0)
            bmax = cmax if bmax is None else jnp.maximum(bmax, cmax)
        bmax_ref[x, slot] = jnp.max(bmax, axis=0, keepdims=True)

    def stage_softmax(x, t, slot, diagonal):
        if diagonal:
            shift = 0.0
        else:
            shift = jnp.where(t < n_steps[x], 0.0, NEG_BIG)
        m = m_ref[x]
        m_new = jnp.maximum(m, bmax_ref[x, slot] + shift)
        alpha = jnp.exp2(m - m_new)
        m_ref[x] = m_new
        alpha_ref[x, slot] = alpha
        ref = m_new - shift
        for r in range(0, TK, ATT_EXP_ROWS):
            rows = slice(r, r + ATT_EXP_ROWS)
            p_ref[x, slot, rows] = jnp.exp2(s_ref[x, slot, rows] - ref).astype(BF16)

    def stage_pv(x, t, slot):
        pv = jnp.dot(vt_ref[block_of(x, t)], p_ref[x, slot], preferred_element_type=F32)
        acc_ref[x] = alpha_ref[x, slot] * acc_ref[x] + pv

    for x in tiles:
        stage_scores(x, 0, 0, True)
    for x in tiles:
        stage_softmax(x, 0, 0, True)
    for x in tiles:
        stage_scores(x, 1, 1, False)

    n_steps = []
    for x in tiles:
        q1sq, q2sq = max_sq_norms(q_ref[0, 0, x * TQ:(x + 1) * TQ, :])
        bound = jnp.sqrt(jnp.maximum(q1sq * knorm_ref[0:1, 0:1], q2sq * knorm_ref[1:2, 0:1])) * 1.001
        m_min = jnp.min(m_ref[x], axis=1, keepdims=True)
        idx = tile_idx[x].astype(F32)
        reach = idx + (ATT_EXP2_ZERO + bound - m_min) / (slope * TK)
        n = jnp.minimum(jnp.floor(reach) + 2.0, idx + 1.0)
        n_steps.append(jnp.max(n).astype(jnp.int32))

    def pipeline_tick(t, slot):
        for x in tiles:
            stage_pv(x, t, slot)
        for x in tiles:
            stage_softmax(x, t + 1, 1 - slot, False)
        for x in tiles:
            stage_scores(x, t + 2, slot, False)

    def pipeline_drain(t, slot):
        for x in tiles:
            stage_pv(x, t, slot)
        for x in tiles:
            stage_softmax(x, t + 1, 1 - slot, False)
        for x in tiles:
            stage_pv(x, t + 1, 1 - slot)

    def by_parity(fn, t):
        for slot in (0, 1):
            @pl.when(jnp.bitwise_and(t, 1) == slot)
            def _():
                fn(t, slot)

    def body(t, carry):
        by_parity(pipeline_tick, t)
        return carry

    n_ticks = jnp.maximum(functools.reduce(jnp.maximum, n_steps), 2)
    lax.fori_loop(0, n_ticks - 2, body, 0)
    by_parity(pipeline_drain, n_ticks - 2)
    lp = lam_ref[...]
    lam = (jnp.exp(jnp.sum(lp[0:1] * lp[1:2], axis=-1, keepdims=True))
           - jnp.exp(jnp.sum(lp[2:3] * lp[3:4], axis=-1, keepdims=True)) + lam_init)
    for x in tiles:
        o = acc_ref[x, 0:DIFF_V_DIM] * (1.0 / acc_ref[x, DIFF_V_DIM:DIFF_V_DIM + 1])
        y = (o[:, :TQ] - lam * o[:, TQ:]).T
        ms = jnp.mean(y * y, axis=-1, keepdims=True)
        y = y * lax.rsqrt(ms + NORM_EPS) * sub_ref[...] * (1.0 - lam_init)
        o_ref[0, x * TQ:(x + 1) * TQ] = y.astype(BF16)


def _diffattn(dq, dk, dv, lam_params, subln, layer_idx):
    B, H, S, w = dq.shape
    lam_init = 0.8 - 0.6 * math.exp(-0.3 * layer_idx)
    slopes, diag_bias, coef = _attn_bias_tables(H)
    NT, TQS = ATT_TILES, ATT_TILES * ATT_TQ
    grid_spec = pltpu.PrefetchScalarGridSpec(
        num_scalar_prefetch=1,
        grid=(B, H, S // TQS),
        in_specs=[
            pl.BlockSpec((1, 1, TQS, w), lambda b, h, i, sl: (b, h, i, 0)),
            pl.BlockSpec((1, 1, S, w), lambda b, h, i, sl: (b, h, 0, 0)),
            pl.BlockSpec((1, 1, S, w), lambda b, h, i, sl: (b, h, 0, 0)),
            pl.BlockSpec((1, ATT_TK, 2 * ATT_TQ), lambda b, h, i, sl: (h, 0, 0)),
            pl.BlockSpec((1, 1, 128), lambda b, h, i, sl: (h, 0, 0)),
            pl.BlockSpec((4, DIFF_HEAD_DIM), lambda b, h, i, sl: (0, 0)),
            pl.BlockSpec((1, w), lambda b, h, i, sl: (0, 0)),
        ],
        out_specs=pl.BlockSpec((1, TQS, w), lambda b, h, i, sl: (b, i, h)),
        scratch_shapes=[
            pltpu.VMEM((S // ATT_TK, w + ATT_SUM_ROWS, ATT_TK), BF16),
            pltpu.VMEM((S, 2 * w), BF16),
            pltpu.VMEM((NT, 2 * ATT_TQ, 2 * w), BF16),
            pltpu.VMEM((NT, 2, ATT_TK, 2 * ATT_TQ), F32),
            pltpu.VMEM((NT, 2, ATT_TK, 2 * ATT_TQ), BF16),
            pltpu.VMEM((NT, 2, 1, 2 * ATT_TQ), F32),
            pltpu.VMEM((NT, 2, 1, 2 * ATT_TQ), F32),
            pltpu.VMEM((NT, 1, 2 * ATT_TQ), F32),
            pltpu.VMEM((8, 128), F32),
            pltpu.VMEM((NT, w + ATT_SUM_ROWS, 2 * ATT_TQ), F32),
        ],
    )
    return pl.pallas_call(
        functools.partial(_diffattn_kernel, lam_init=lam_init),
        grid_spec=grid_spec,
        out_shape=jax.ShapeDtypeStruct((B, S, H * w), BF16),
        compiler_params=_params("arbitrary", "arbitrary", "arbitrary"),
        name="diff_attention",
    )(slopes, dq, dk, dv, diag_bias, coef, lam_params, subln.reshape(1, w))


def _merge_kernel(x_ref, mod_ref, yr_ref, yd_ref, gr_ref, gd_ref, wr_ref, wd_ref, wo_ref, o_ref):
    def sigmoid(g):
        return 0.5 * (1.0 + jnp.tanh(0.5 * g))

    br = jnp.dot(yr_ref[0], wr_ref[0], preferred_element_type=F32)
    bd = jnp.dot(yd_ref[0], wd_ref[0], preferred_element_type=F32)
    merged = sigmoid(gr_ref[0].astype(F32)) * br + sigmoid(gd_ref[0].astype(F32)) * bd
    out = jnp.dot(merged.astype(BF16), wo_ref[0], preferred_element_type=F32)
    o_ref[0] = x_ref[0] + mod_ref[0][2:3] * out


def _merge(x, mod, y_ret, y_diff, g_ret, g_diff, w_rb, w_db, w_o, layer):
    B, S, D = x.shape
    tok = pl.BlockSpec((1, PROJ_TM, D), lambda b, t: (b, t, 0))
    return pl.pallas_call(
        _merge_kernel,
        grid=(B, S // PROJ_TM),
        in_specs=[tok, pl.BlockSpec((1, 3, D), lambda b, t: (b, 0, 0)), tok, tok, tok, tok,
                  _resident_slice(w_rb.shape, (layer,)), _resident_slice(w_db.shape, (layer,)),
                  _resident_slice(w_o.shape, (layer,))],
        out_specs=tok,
        out_shape=jax.ShapeDtypeStruct(x.shape, F32),
        compiler_params=_params("parallel", "parallel"),
        name="mixer_merge",
    )(x, mod, y_ret, y_diff, g_ret, g_diff, w_rb, w_db, w_o)


def kernel(x, c, w_ada, b_ada, norm_w, w_ffn_up, w_ffn_down, w_in, ret_gn, lambda_q1, lambda_k1, lambda_q2,
           lambda_k2, diff_subln, w_ret_branch, w_diff_branch, w_out, final_norm):
    B, S, D = x.shape
    assert D == D_MODEL and S % max(FFN_TM, RET_L, ATT_TILES * ATT_TQ) == 0 and ATT_TQ == ATT_TK
    mod_all = _adaln(c, w_ada, b_ada, norm_w).reshape(DEPTH, B, N_SUB, 3, D)
    up, down, w_in_b = w_ffn_up.astype(BF16), w_ffn_down.astype(BF16), w_in.astype(BF16)
    w_rb, w_db, w_o = w_ret_branch.astype(BF16), w_diff_branch.astype(BF16), w_out.astype(BF16)
    for l in range(DEPTH):
        mod = [mod_all[l, :, s] for s in range(N_SUB)]
        x = _ffn(x, mod[0], up, down, l, 0)
        rq, rk, rv, rg, dq, dk, dv, g_ret, g_diff = _inproj(x, mod[1], w_in_b, l)
        y_ret = _retention(rq, rk, rv, rg, ret_gn[l])
        lam_params = jnp.stack([lambda_q1[l], lambda_k1[l], lambda_q2[l], lambda_k2[l]])
        y_diff = _diffattn(dq, dk, dv, lam_params, diff_subln[l], l)
        x = _merge(x, mod[1], y_ret, y_diff, g_ret, g_diff, w_rb, w_db, w_o, l)
        x = _ffn(x, mod[2], up, down, l, 1, final_w=final_norm if l == DEPTH - 1 else None)
    return x
```

```python
import functools
import math

import jax
import jax.numpy as jnp
from jax import lax
from jax.experimental import pallas as pl
from jax.experimental.pallas import tpu as pltpu

DEPTH = 4
D_MODEL = 1024
CHUNK = 64
CHUNK_SHIFT = 6
RET_HEADS = 4
RET_QK_DIM = 128
RET_V_DIM = 256
DIFF_HEADS = 8
DIFF_HEAD_DIM = 64
DIFF_V_DIM = 2 * DIFF_HEAD_DIM
D_FF = 2816
N_SUB = 3
NORM_EPS = 1e-6
IN_WIDTH = 8 * D_MODEL

F32 = jnp.float32
BF16 = jnp.bfloat16

V7X_VMEM_LIMIT_BYTES = 56 * 1024 * 1024

FFN_TM = 1024
PROJ_TM = 512
FFN_FC = 256
RET_L = 256
ATT_TQ = 512
ATT_TK = 512
ATT_EXP_ROWS = 16
ATT_TILES = 2
ATT_POS_LANES = 3
ATT_SUM_ROWS = 16
ATT_EXP2_ZERO = 160.0
NEG_BIG = -1e30
LOG2E = math.log2(math.e)


def _resident(shape):
    nd = len(shape)
    return pl.BlockSpec(shape, lambda *_: (0,) * nd, pipeline_mode=pl.Buffered(1))


def _resident_slice(shape, lead):
    n_lead, nd = len(lead), len(shape)
    index = tuple(lead) + (0,) * (nd - n_lead)
    return pl.BlockSpec((1,) * n_lead + tuple(shape[n_lead:]), lambda *_: index, pipeline_mode=pl.Buffered(1))


def _params(*sem, flags=None):
    return pltpu.CompilerParams(dimension_semantics=sem, vmem_limit_bytes=V7X_VMEM_LIMIT_BYTES, flags=flags)


def _adaln_kernel(c_ref, w_ref, b_ref, nw_ref, o_ref):
    j = pl.program_id(1)
    c = c_ref[...]
    cond = c / (1.0 + jnp.exp(-c))
    r = jnp.dot(cond, w_ref[0], preferred_element_type=F32,
                precision=lax.Precision.HIGHEST) + b_ref[0]
    kind = j % 3
    sub = j // 3
    r = jnp.where(kind == 1, (1.0 + r) * nw_ref[0, 0], r)
    r = jnp.where(jnp.logical_and(kind == 2, sub != 1), 0.5 * r, r)
    o_ref[0] = r


def _adaln(c, w_ada, b_ada, norm_w):
    B = c.shape[0]
    n_tiles = N_SUB * 3
    return pl.pallas_call(
        _adaln_kernel,
        grid=(DEPTH, n_tiles),
        in_specs=[
            pl.BlockSpec((B, D_MODEL), lambda l, j: (0, 0)),
            pl.BlockSpec((1, D_MODEL, D_MODEL), lambda l, j: (l, 0, j)),
            pl.BlockSpec((1, 1, D_MODEL), lambda l, j: (l, 0, j)),
            pl.BlockSpec((1, 1, 1, D_MODEL), lambda l, j: (l, j // 3, 0, 0)),
        ],
        out_specs=pl.BlockSpec((1, B, D_MODEL), lambda l, j: (l, 0, j)),
        out_shape=jax.ShapeDtypeStruct((DEPTH, B, n_tiles * D_MODEL), F32),
        compiler_params=_params("arbitrary", "arbitrary"),
        name="adaln_mod",
    )(c, w_ada, b_ada.reshape(DEPTH, 1, n_tiles * D_MODEL), norm_w.reshape(DEPTH, N_SUB, 1, D_MODEL))


def _modulated_norm(x, mod):
    ms = jnp.mean(x * x, axis=-1, keepdims=True)
    return x * lax.rsqrt(ms + NORM_EPS) * mod[1:2] + mod[0:1]


def _ffn_kernel(x_ref, mod_ref, wup_ref, wdn_ref, *rest, n_chunks, final):
    if final:
        fw_ref, o_ref = rest
    else:
        (o_ref,) = rest
    x = x_ref[0]
    mod = mod_ref[0]
    h = _modulated_norm(x, mod).astype(BF16)
    acc = jnp.zeros(x.shape, F32)
    for c in range(n_chunks):
        cols = slice(c * FFN_FC, (c + 1) * FFN_FC)
        a = jnp.dot(h, wup_ref[0, 0, :, cols], preferred_element_type=F32)
        b = jnp.dot(h, wup_ref[0, 0, :, D_FF + c * FFN_FC:D_FF + (c + 1) * FFN_FC], preferred_element_type=F32)
        act = (0.5 * a) * (1.0 + jnp.tanh(0.5 * a)) * b
        acc = acc + jnp.dot(act.astype(BF16), wdn_ref[0, 0, cols, :], preferred_element_type=F32)
    y = x + mod[2:3] * acc
    if final:
        ms = jnp.mean(y * y, axis=-1, keepdims=True)
        y = y * lax.rsqrt(ms + NORM_EPS) * fw_ref[...]
    o_ref[0] = y


def _ffn(x, mod, wup, wdn, layer, which, final_w=None):
    B, S, D = x.shape
    n_chunks = D_FF // FFN_FC
    final = final_w is not None
    in_specs = [
        pl.BlockSpec((1, FFN_TM, D), lambda b, t: (b, t, 0)),
        pl.BlockSpec((1, 3, D), lambda b, t: (b, 0, 0)),
        _resident_slice(wup.shape, (layer, which)),
        _resident_slice(wdn.shape, (layer, which)),
    ]
    args = [x, mod, wup, wdn]
    if final:
        in_specs.append(_resident((1, D)))
        args.append(final_w.reshape(1, D))
    return pl.pallas_call(
        functools.partial(_ffn_kernel, n_chunks=n_chunks, final=final),
        grid=(B, S // FFN_TM),
        in_specs=in_specs,
        out_specs=pl.BlockSpec((1, FFN_TM, D), lambda b, t: (b, t, 0)),
        out_shape=jax.ShapeDtypeStruct(x.shape, F32),
        compiler_params=_params("parallel", "parallel"),
        name="ffn_final" if final else "ffn",
    )(*args)


def _inproj_kernel(x_ref, mod_ref, w_ref, rq_ref, rk_ref, rv_ref, rg_ref, dq_ref, dk_ref, dv_ref,
                   gr_ref, gd_ref):
    x = x_ref[0]
    h = _modulated_norm(x, mod_ref[0]).astype(BF16)

    def proj(c):
        return jnp.dot(h, w_ref[0, :, c * D_MODEL:(c + 1) * D_MODEL], preferred_element_type=F32)

    u = proj(0).astype(BF16)
    for hh in range(RET_HEADS):
        rq_ref[0, hh] = u[:, hh * RET_QK_DIM:(hh + 1) * RET_QK_DIM]
        rk_ref[0, hh] = u[:, (RET_HEADS + hh) * RET_QK_DIM:(RET_HEADS + hh + 1) * RET_QK_DIM]
    u = proj(1).astype(BF16)
    for hh in range(RET_HEADS):
        rv_ref[0, hh] = u[:, hh * RET_V_DIM:(hh + 1) * RET_V_DIM]
    rg_ref[0] = proj(2).astype(BF16)
    u = (proj(3) * (DIFF_HEAD_DIM ** -0.5 * LOG2E)).astype(BF16)
    for hh in range(DIFF_HEADS):
        dq_ref[0, hh] = u[:, hh * DIFF_V_DIM:(hh + 1) * DIFF_V_DIM]
    u = proj(4).astype(BF16)
    for hh in range(DIFF_HEADS):
        dk_ref[0, hh] = u[:, hh * DIFF_V_DIM:(hh + 1) * DIFF_V_DIM]
    u = proj(5).astype(BF16)
    for hh in range(DIFF_HEADS):
        dv_ref[0, hh] = u[:, hh * DIFF_V_DIM:(hh + 1) * DIFF_V_DIM]
    gr_ref[0] = proj(6).astype(BF16)
    gd_ref[0] = proj(7).astype(BF16)


def _inproj(x, mod, w_in, layer):
    B, S, D = x.shape
    TM = PROJ_TM
    head_spec = lambda nh, w: pl.BlockSpec((1, nh, TM, w), lambda b, t: (b, 0, t, 0))
    tok_spec = pl.BlockSpec((1, TM, D), lambda b, t: (b, t, 0))
    sds = jax.ShapeDtypeStruct
    return pl.pallas_call(
        _inproj_kernel,
        grid=(B, S // TM),
        in_specs=[tok_spec, pl.BlockSpec((1, 3, D), lambda b, t: (b, 0, 0)),
                  _resident_slice(w_in.shape, (layer,))],
        out_specs=[
            head_spec(RET_HEADS, RET_QK_DIM), head_spec(RET_HEADS, RET_QK_DIM),
            head_spec(RET_HEADS, RET_V_DIM), tok_spec,
            head_spec(DIFF_HEADS, DIFF_V_DIM), head_spec(DIFF_HEADS, DIFF_V_DIM),
            head_spec(DIFF_HEADS, DIFF_V_DIM), tok_spec, tok_spec,
        ],
        out_shape=[
            sds((B, RET_HEADS, S, RET_QK_DIM), BF16), sds((B, RET_HEADS, S, RET_QK_DIM), BF16),
            sds((B, RET_HEADS, S, RET_V_DIM), BF16), sds((B, S, D), BF16),
            sds((B, DIFF_HEADS, S, DIFF_V_DIM), BF16), sds((B, DIFF_HEADS, S, DIFF_V_DIM), BF16),
            sds((B, DIFF_HEADS, S, DIFF_V_DIM), BF16), sds((B, S, D), BF16), sds((B, S, D), BF16),
        ],
        compiler_params=_params("parallel", "parallel"),
        name="mixer_inproj",
    )(x, mod, w_in)


def _retention_consts():
    L = RET_L
    gamma = 1.0 - 2.0 ** (-5.0 - jnp.arange(RET_HEADS, dtype=F32))
    log_g = jnp.log(gamma)
    r = jnp.arange(L, dtype=F32)
    scale = RET_QK_DIM ** -0.5
    allowed = (jnp.arange(L)[None, :] // CHUNK) <= (jnp.arange(L)[:, None] // CHUNK)
    intra = jnp.exp(log_g[:, None, None] * jnp.abs(r[:, None] - r[None, :]))
    intra = jnp.where(allowed[None], intra, 0.0) * scale
    q_dec = jnp.exp(log_g[:, None] * r[None, :]) * scale
    k_dec = jnp.exp(log_g[:, None] * (L - r)[None, :])
    blk_dec = jnp.exp(log_g * L)
    q_dec = jnp.broadcast_to(q_dec[:, :, None], (RET_HEADS, L, RET_QK_DIM))
    k_dec = jnp.broadcast_to(k_dec[:, :, None], (RET_HEADS, L, RET_QK_DIM))
    blk_dec = jnp.broadcast_to(blk_dec[:, None, None], (RET_HEADS, 1, RET_V_DIM))
    return intra, q_dec, k_dec, blk_dec


def _retention_kernel(q_ref, k_ref, v_ref, g_ref, w_ref, qd_ref, kd_ref, bd_ref, gn_ref, o_ref, state_ref):
    t = pl.program_id(1)

    @pl.when(t == 0)
    def _():
        state_ref[...] = jnp.zeros_like(state_ref)

    dv = v_ref.shape[-1]
    for h in range(q_ref.shape[1]):
        q = q_ref[0, h]
        k = k_ref[0, h]
        v = v_ref[0, h]
        s = lax.dot_general(q, k, (((1,), (1,)), ((), ())), preferred_element_type=F32)
        p = (s * w_ref[h]).astype(BF16)
        y = jnp.dot(p, v, preferred_element_type=F32)
        state = state_ref[h]
        qd = (q.astype(F32) * qd_ref[h]).astype(BF16)
        y = y + jnp.dot(qd, state.astype(BF16), preferred_element_type=F32)
        kd_t = (k.astype(F32) * kd_ref[h]).T.astype(BF16)
        state_ref[h] = state * bd_ref[h] + jnp.dot(kd_t, v, preferred_element_type=F32)
        ms = jnp.mean(y * y, axis=-1, keepdims=True)
        y = y * lax.rsqrt(ms + NORM_EPS) * gn_ref[h]
        g = g_ref[0, :, h * dv:(h + 1) * dv].astype(F32)
        o_ref[0, :, h * dv:(h + 1) * dv] = (y * (0.5 * g) * (1.0 + jnp.tanh(0.5 * g))).astype(BF16)


def _retention(rq, rk, rv, rg, ret_gn_l):
    B, H, S, dk = rq.shape
    dv = rv.shape[-1]
    L = RET_L
    intra, q_dec, k_dec, blk_dec = _retention_consts()
    blk = lambda w: pl.BlockSpec((1, H, L, w), lambda b, t: (b, 0, t, 0))
    tok = pl.BlockSpec((1, L, H * dv), lambda b, t: (b, t, 0))
    return pl.pallas_call(
        _retention_kernel,
        grid=(B, S // L),
        in_specs=[
            blk(dk), blk(dk), blk(dv), tok,
            _resident((H, L, L)), _resident((H, L, dk)), _resident((H, L, dk)), _resident((H, 1, dv)),
            _resident((H, 1, dv)),
        ],
        out_specs=tok,
        out_shape=jax.ShapeDtypeStruct((B, S, H * dv), BF16),
        scratch_shapes=[pltpu.VMEM((H, dk, dv), F32)],
        compiler_params=_params("parallel", "arbitrary"),
        name="retention",
    )(rq, rk, rv, rg, intra, q_dec, k_dec, blk_dec, ret_gn_l.reshape(H, 1, dv))


def _attn_bias_tables(n_heads):
    T = ATT_TQ
    slopes = 2.0 ** (-8.0 * jnp.arange(1, n_heads + 1, dtype=F32) / n_heads) * LOG2E
    r = jnp.arange(T, dtype=F32)
    a, c = r[:, None], r[None, :]
    allowed = (jnp.arange(T)[:, None] // CHUNK) <= (jnp.arange(T)[None, :] // CHUNK)
    diag = slopes[:, None, None] * (-2.0 * jnp.maximum(a - c, 0.0))[None]
    diag = jnp.where(allowed[None], diag, NEG_BIG)
    diag = jnp.concatenate([diag, diag], axis=-1)
    p0 = slopes.astype(BF16)
    p1 = (slopes - p0.astype(F32)).astype(BF16)
    p2 = (slopes - p0.astype(F32) - p1.astype(F32)).astype(BF16)
    parts = jnp.stack([p0, p1, p2], axis=1).astype(F32)
    coef = jnp.concatenate([parts * CHUNK, parts, jnp.zeros((n_heads, 128 - 2 * ATT_POS_LANES), F32)], axis=1)
    return slopes, diag, coef.astype(BF16).reshape(n_heads, 1, 128)


def _diffattn_kernel(slope_ref, q_ref, k_ref, v_ref, db_ref, coef_ref, lam_ref, sub_ref, o_ref,
                     vt_ref, kaug_ref, qbd_ref, s_ref, p_ref, bmax_ref, alpha_ref, m_ref, knorm_ref, acc_ref, *,
                     lam_init):
    h = pl.program_id(1)
    TK = ATT_TK
    n_kv = vt_ref.shape[0]

    @pl.when(jnp.logical_and(pl.program_id(0) == 0, h == 0))
    def _():
        for c in range(n_kv):
            ones_row = lax.broadcasted_iota(jnp.int32, (ATT_SUM_ROWS, TK), 0) == 0
            vt_ref[c, DIFF_V_DIM:DIFF_V_DIM + ATT_SUM_ROWS] = ones_row.astype(F32).astype(BF16)
            pos = c * TK + lax.broadcasted_iota(jnp.int32, (TK, 128), 0)
            lane = lax.broadcasted_iota(jnp.int32, (TK, 128), 1)
            feat = jnp.where(lane < ATT_POS_LANES, jnp.right_shift(pos, CHUNK_SHIFT),
                             jnp.where(lane < 2 * ATT_POS_LANES, jnp.bitwise_and(pos, CHUNK - 1), 0))
            kaug_ref[c * TK:(c + 1) * TK, 2 * DIFF_HEAD_DIM:] = feat.astype(F32).astype(BF16)

    k1sq = jnp.zeros((1, TK), F32)
    k2sq = jnp.zeros((1, TK), F32)
    for c in range(n_kv):
        vt_ref[c, 0:DIFF_V_DIM] = v_ref[0, 0, c * TK:(c + 1) * TK, :].astype(F32).T.astype(BF16)
        k = k_ref[0, 0, c * TK:(c + 1) * TK, :]
        kaug_ref[c * TK:(c + 1) * TK, 0:2 * DIFF_HEAD_DIM] = k
        ksq_t = jnp.square(k.astype(F32).T)
        k1sq = jnp.maximum(k1sq, jnp.sum(ksq_t[0:DIFF_HEAD_DIM], axis=0, keepdims=True))
        k2sq = jnp.maximum(k2sq, jnp.sum(ksq_t[DIFF_HEAD_DIM:], axis=0, keepdims=True))
    knorm_ref[0:1] = jnp.broadcast_to(jnp.max(k1sq, axis=1, keepdims=True), (1, 128))
    knorm_ref[1:2] = jnp.broadcast_to(jnp.max(k2sq, axis=1, keepdims=True), (1, 128))

    def group(g, carry):
        _diffattn_group(g, h, slope_ref, q_ref, db_ref, coef_ref, lam_ref, sub_ref, o_ref, vt_ref, kaug_ref,
                        qbd_ref, s_ref, p_ref, bmax_ref, alpha_ref, m_ref, knorm_ref, acc_ref, lam_init=lam_init)
        return carry

    lax.fori_loop(0, q_ref.shape[2] // (ATT_TILES * ATT_TQ), group, 0)


def _max_sq_norms(x):
    xsq = jnp.square(x.astype(F32))
    first = lax.broadcasted_iota(jnp.int32, xsq.shape, 1) < DIFF_HEAD_DIM
    n1 = jnp.sum(jnp.where(first, xsq, 0.0), axis=1, keepdims=True)
    n2 = jnp.sum(jnp.where(first, 0.0, xsq), axis=1, keepdims=True)
    return jnp.max(n1, axis=0, keepdims=True), jnp.max(n2, axis=0, keepdims=True)


def _diffattn_group(g, h, slope_ref, q_ref, db_ref, coef_ref, lam_ref, sub_ref, o_ref, vt_ref, kaug_ref,
                    qbd_ref, s_ref, p_ref, bmax_ref, alpha_ref, m_ref, knorm_ref, acc_ref, *, lam_init):
    TQ, TK = ATT_TQ, ATT_TK
    slope = slope_ref[h]
    tiles = range(ATT_TILES)
    tile_idx = [g * ATT_TILES + x for x in tiles]

    def q_rows(x):
        return pl.ds(pl.multiple_of(tile_idx[x] * TQ, TQ), TQ)

    for x in tiles:
        q = q_ref[0, 0, q_rows(x), :]
        lane = lax.broadcasted_iota(jnp.int32, q.shape, 1)
        zero = jnp.zeros_like(q)
        qbd_ref[x, 0:TQ, 0:2 * DIFF_HEAD_DIM] = jnp.where(lane < DIFF_HEAD_DIM, q, zero)
        qbd_ref[x, TQ:2 * TQ, 0:2 * DIFF_HEAD_DIM] = jnp.where(lane >= DIFF_HEAD_DIM, q, zero)
        qbd_ref[x, :, 2 * DIFF_HEAD_DIM:] = jnp.broadcast_to(coef_ref[0], (2 * TQ, 128))
        m_ref[x] = jnp.full(m_ref.shape[1:], NEG_BIG, F32)
        acc_ref[x] = jnp.zeros(acc_ref.shape[1:], F32)

    def block_of(x, t):
        return jnp.maximum(tile_idx[x] - t, 0)

    def stage_scores(x, t, slot, diagonal):
        k = kaug_ref[pl.ds(pl.multiple_of(block_of(x, t) * TK, TK), TK), :]
        s = lax.dot_general(k, qbd_ref[x], (((1,), (1,)), ((), ())), preferred_element_type=F32)
        bmax = None
        for r in range(0, TK, ATT_EXP_ROWS):
            rows = slice(r, r + ATT_EXP_ROWS)
            u = s[rows] + db_ref[0, rows] if diagonal else s[rows]
            s_ref[x, slot, rows] = u
            cmax = jnp.max(u.reshape(ATT_EXP_ROWS // 8, 8, 2 * TQ), axis=0)
            bmax = cmax if bmax is None else jnp.maximum(bmax, cmax)
        bmax_ref[x, slot] = jnp.max(bmax, axis=0, keepdims=True)

    def stage_softmax(x, t, slot, diagonal):
        if diagonal:
            shift = 0.0
        else:
            shift = jnp.where(t < n_steps[x], 0.0, NEG_BIG)
        m = m_ref[x]
        m_new = jnp.maximum(m, bmax_ref[x, slot] + shift)
        alpha = jnp.exp2(m - m_new)
        m_ref[x] = m_new
        alpha_ref[x, slot] = alpha
        ref = m_new - shift
        for r in range(0, TK, ATT_EXP_ROWS):
            rows = slice(r, r + ATT_EXP_ROWS)
            p_ref[x, slot, rows] = jnp.exp2(s_ref[x, slot, rows] - ref).astype(BF16)

    def stage_pv(x, t, slot):
        pv = jnp.dot(vt_ref[block_of(x, t)], p_ref[x, slot], preferred_element_type=F32)
        acc_ref[x] = alpha_ref[x, slot] * acc_ref[x] + pv

    for x in tiles:
        stage_scores(x, 0, 0, True)
    for x in tiles:
        stage_softmax(x, 0, 0, True)
    for x in tiles:
        stage_scores(x, 1, 1, False)

    n_steps = []
    for x in tiles:
        q1sq, q2sq = _max_sq_norms(q_ref[0, 0, q_rows(x), :])
        bound = jnp.sqrt(jnp.maximum(q1sq * knorm_ref[0:1, 0:1], q2sq * knorm_ref[1:2, 0:1])) * 1.001
        m_min = jnp.min(m_ref[x], axis=1, keepdims=True)
        idx = tile_idx[x].astype(F32)
        reach = idx + (ATT_EXP2_ZERO + bound - m_min) / (slope * TK)
        n = jnp.minimum(jnp.floor(reach) + 2.0, idx + 1.0)
        n_steps.append(jnp.max(n).astype(jnp.int32))

    def pipeline_tick(t, slot):
        for x in tiles:
            stage_pv(x, t, slot)
        for x in tiles:
            stage_softmax(x, t + 1, 1 - slot, False)
        for x in tiles:
            stage_scores(x, t + 2, slot, False)

    def pipeline_drain(t, slot):
        for x in tiles:
            stage_pv(x, t, slot)
        for x in tiles:
            stage_softmax(x, t + 1, 1 - slot, False)
        for x in tiles:
            stage_pv(x, t + 1, 1 - slot)

    def by_parity(fn, t):
        for slot in (0, 1):
            @pl.when(jnp.bitwise_and(t, 1) == slot)
            def _():
                fn(t, slot)

    def body(t, carry):
        by_parity(pipeline_tick, t)
        return carry

    n_ticks = jnp.maximum(functools.reduce(jnp.maximum, n_steps), 2)
    lax.fori_loop(0, n_ticks - 2, body, 0)
    by_parity(pipeline_drain, n_ticks - 2)
    lp = lam_ref[...]
    lam = (jnp.exp(jnp.sum(lp[0:1] * lp[1:2], axis=-1, keepdims=True))
           - jnp.exp(jnp.sum(lp[2:3] * lp[3:4], axis=-1, keepdims=True)) + lam_init)
    for x in tiles:
        o = acc_ref[x, 0:DIFF_V_DIM] * (1.0 / acc_ref[x, DIFF_V_DIM:DIFF_V_DIM + 1])
        y = (o[:, :TQ] - lam * o[:, TQ:]).T
        ms = jnp.mean(y * y, axis=-1, keepdims=True)
        y = y * lax.rsqrt(ms + NORM_EPS) * sub_ref[...] * (1.0 - lam_init)
        o_ref[0, q_rows(x)] = y.astype(BF16)


def _diffattn(dq, dk, dv, lam_params, subln, layer_idx):
    B, H, S, w = dq.shape
    lam_init = 0.8 - 0.6 * math.exp(-0.3 * layer_idx)
    slopes, diag_bias, coef = _attn_bias_tables(H)
    NT = ATT_TILES
    seq = pl.BlockSpec((1, 1, S, w), lambda b, h, sl: (b, h, 0, 0))
    grid_spec = pltpu.PrefetchScalarGridSpec(
        num_scalar_prefetch=1,
        grid=(B, H),
        in_specs=[
            seq, seq, seq,
            pl.BlockSpec((1, ATT_TK, 2 * ATT_TQ), lambda b, h, sl: (h, 0, 0)),
            pl.BlockSpec((1, 1, 128), lambda b, h, sl: (h, 0, 0)),
            pl.BlockSpec((4, DIFF_HEAD_DIM), lambda b, h, sl: (0, 0)),
            pl.BlockSpec((1, w), lambda b, h, sl: (0, 0)),
        ],
        out_specs=pl.BlockSpec((1, S, w), lambda b, h, sl: (b, 0, h)),
        scratch_shapes=[
            pltpu.VMEM((S // ATT_TK, w + ATT_SUM_ROWS, ATT_TK), BF16),
            pltpu.VMEM((S, 2 * w), BF16),
            pltpu.VMEM((NT, 2 * ATT_TQ, 2 * w), BF16),
            pltpu.VMEM((NT, 2, ATT_TK, 2 * ATT_TQ), F32),
            pltpu.VMEM((NT, 2, ATT_TK, 2 * ATT_TQ), BF16),
            pltpu.VMEM((NT, 2, 1, 2 * ATT_TQ), F32),
            pltpu.VMEM((NT, 2, 1, 2 * ATT_TQ), F32),
            pltpu.VMEM((NT, 1, 2 * ATT_TQ), F32),
            pltpu.VMEM((8, 128), F32),
            pltpu.VMEM((NT, w + ATT_SUM_ROWS, 2 * ATT_TQ), F32),
        ],
    )
    return pl.pallas_call(
        functools.partial(_diffattn_kernel, lam_init=lam_init),
        grid_spec=grid_spec,
        out_shape=jax.ShapeDtypeStruct((B, S, H * w), BF16),
        compiler_params=_params("arbitrary", "arbitrary"),
        name="diff_attention",
    )(slopes, dq, dk, dv, diag_bias, coef, lam_params, subln.reshape(1, w))


def _merge_kernel(x_ref, mod_ref, yr_ref, yd_ref, gr_ref, gd_ref, wr_ref, wd_ref, wo_ref, o_ref):
    def sigmoid(g):
        return 0.5 * (1.0 + jnp.tanh(0.5 * g))

    br = jnp.dot(yr_ref[0], wr_ref[0], preferred_element_type=F32)
    bd = jnp.dot(yd_ref[0], wd_ref[0], preferred_element_type=F32)
    merged = sigmoid(gr_ref[0].astype(F32)) * br + sigmoid(gd_ref[0].astype(F32)) * bd
    out = jnp.dot(merged.astype(BF16), wo_ref[0], preferred_element_type=F32)
    o_ref[0] = x_ref[0] + mod_ref[0][2:3] * out


def _merge(x, mod, y_ret, y_diff, g_ret, g_diff, w_rb, w_db, w_o, layer):
    B, S, D = x.shape
    tok = pl.BlockSpec((1, PROJ_TM, D), lambda b, t: (b, t, 0))
    return pl.pallas_call(
        _merge_kernel,
        grid=(B, S // PROJ_TM),
        in_specs=[tok, pl.BlockSpec((1, 3, D), lambda b, t: (b, 0, 0)), tok, tok, tok, tok,
                  _resident_slice(w_rb.shape, (layer,)), _resident_slice(w_db.shape, (layer,)),
                  _resident_slice(w_o.shape, (layer,))],
        out_specs=tok,
        out_shape=jax.ShapeDtypeStruct(x.shape, F32),
        compiler_params=_params("parallel", "parallel"),
        name="mixer_merge",
    )(x, mod, y_ret, y_diff, g_ret, g_diff, w_rb, w_db, w_o)


def kernel(x, c, w_ada, b_ada, norm_w, w_ffn_up, w_ffn_down, w_in, ret_gn, lambda_q1, lambda_k1, lambda_q2,
           lambda_k2, diff_subln, w_ret_branch, w_diff_branch, w_out, final_norm):
    B, S, D = x.shape
    assert D == D_MODEL and S % max(FFN_TM, RET_L, ATT_TILES * ATT_TQ) == 0 and ATT_TQ == ATT_TK
    mod_all = _adaln(c, w_ada, b_ada, norm_w).reshape(DEPTH, B, N_SUB, 3, D)
    up, down, w_in_b = w_ffn_up.astype(BF16), w_ffn_down.astype(BF16), w_in.astype(BF16)
    w_rb, w_db, w_o = w_ret_branch.astype(BF16), w_diff_branch.astype(BF16), w_out.astype(BF16)
    for l in range(DEPTH):
        mod = [mod_all[l, :, s] for s in range(N_SUB)]
        x = _ffn(x, mod[0], up, down, l, 0)
        rq, rk, rv, rg, dq, dk, dv, g_ret, g_diff = _inproj(x, mod[1], w_in_b, l)
        y_ret = _retention(rq, rk, rv, rg, ret_gn[l])
        lam_params = jnp.stack([lambda_q1[l], lambda_k1[l], lambda_q2[l], lambda_k2[l]])
        y_diff = _diffattn(dq, dk, dv, lam_params, diff_subln[l], l)
        x = _merge(x, mod[1], y_ret, y_diff, g_ret, g_diff, w_rb, w_db, w_o, l)
        x = _ffn(x, mod[2], up, down, l, 1, final_w=final_norm if l == DEPTH - 1 else None)
    return x
```

```python
import functools
import math

import jax
import jax.numpy as jnp
from jax import lax
from jax.experimental import pallas as pl
from jax.experimental.pallas import tpu as pltpu

DEPTH = 4
D_MODEL = 1024
CHUNK = 64
CHUNK_SHIFT = 6
RET_HEADS = 4
RET_QK_DIM = 128
RET_V_DIM = 256
DIFF_HEADS = 8
DIFF_HEAD_DIM = 64
DIFF_V_DIM = 2 * DIFF_HEAD_DIM
D_FF = 2816
N_SUB = 3
NORM_EPS = 1e-6
IN_WIDTH = 8 * D_MODEL

F32 = jnp.float32
BF16 = jnp.bfloat16

V7X_VMEM_LIMIT_BYTES = 56 * 1024 * 1024
V7X_LANES = 128
V7X_SUBLANES = 8

FFN_TM = 1024
PROJ_TM = 512
MERGE_TM = 1024
FFN_FC = 256
RET_L = 256
ATT_TQ = 512
ATT_TK = 512
ATT_EXP_ROWS = 16
ATT_TILES = 2
ATT_POS_LANES = 3
ATT_SUM_ROWS = 16
ATT_EXP2_ZERO = 160.0
ATT_BOUND_MARGIN = 1.001
NEG_BIG = -1e30
LOG2E = math.log2(math.e)


def _resident(shape):
    nd = len(shape)
    return pl.BlockSpec(shape, lambda *_: (0,) * nd, pipeline_mode=pl.Buffered(1))


def _resident_slice(shape, lead):
    n_lead, nd = len(lead), len(shape)
    index = tuple(lead) + (0,) * (nd - n_lead)
    return pl.BlockSpec((1,) * n_lead + tuple(shape[n_lead:]), lambda *_: index, pipeline_mode=pl.Buffered(1))


def _params(*sem, flags=None):
    return pltpu.CompilerParams(dimension_semantics=sem, vmem_limit_bytes=V7X_VMEM_LIMIT_BYTES, flags=flags)


def _adaln_kernel(c_ref, w_ref, b_ref, nw_ref, o_ref):
    j = pl.program_id(1)
    c = c_ref[...]
    cond = c / (1.0 + jnp.exp(-c))
    r = jnp.dot(cond, w_ref[0], preferred_element_type=F32,
                precision=lax.Precision.HIGHEST) + b_ref[0]
    kind = j % 3
    sub = j // 3
    r = jnp.where(kind == 1, (1.0 + r) * nw_ref[0, 0], r)
    r = jnp.where(jnp.logical_and(kind == 2, sub != 1), 0.5 * r, r)
    o_ref[0] = r


def _adaln(c, w_ada, b_ada, norm_w):
    B = c.shape[0]
    n_tiles = N_SUB * 3
    return pl.pallas_call(
        _adaln_kernel,
        grid=(DEPTH, n_tiles),
        in_specs=[
            pl.BlockSpec((B, D_MODEL), lambda l, j: (0, 0)),
            pl.BlockSpec((1, D_MODEL, D_MODEL), lambda l, j: (l, 0, j)),
            pl.BlockSpec((1, 1, D_MODEL), lambda l, j: (l, 0, j)),
            pl.BlockSpec((1, 1, 1, D_MODEL), lambda l, j: (l, j // 3, 0, 0)),
        ],
        out_specs=pl.BlockSpec((1, B, D_MODEL), lambda l, j: (l, 0, j)),
        out_shape=jax.ShapeDtypeStruct((DEPTH, B, n_tiles * D_MODEL), F32),
        compiler_params=_params("arbitrary", "arbitrary"),
        name="adaln_mod",
    )(c, w_ada, b_ada.reshape(DEPTH, 1, n_tiles * D_MODEL), norm_w.reshape(DEPTH, N_SUB, 1, D_MODEL))


def _modulated_norm(x, mod):
    ms = jnp.mean(x * x, axis=-1, keepdims=True)
    return x * lax.rsqrt(ms + NORM_EPS) * mod[1:2] + mod[0:1]


def _ffn_kernel(x_ref, mod_ref, wup_ref, wdn_ref, *rest, n_chunks, final):
    if final:
        fw_ref, o_ref = rest
    else:
        (o_ref,) = rest
    x = x_ref[0]
    mod = mod_ref[0]
    h = _modulated_norm(x, mod).astype(BF16)
    acc = jnp.zeros(x.shape, F32)
    for c in range(n_chunks):
        cols = slice(c * FFN_FC, (c + 1) * FFN_FC)
        a = jnp.dot(h, wup_ref[0, 0, :, cols], preferred_element_type=F32)
        b = jnp.dot(h, wup_ref[0, 0, :, D_FF + c * FFN_FC:D_FF + (c + 1) * FFN_FC], preferred_element_type=F32)
        act = (0.5 * a) * (1.0 + jnp.tanh(0.5 * a)) * b
        acc = acc + jnp.dot(act.astype(BF16), wdn_ref[0, 0, cols, :], preferred_element_type=F32)
    y = x + mod[2:3] * acc
    if final:
        ms = jnp.mean(y * y, axis=-1, keepdims=True)
        y = y * lax.rsqrt(ms + NORM_EPS) * fw_ref[...]
    o_ref[0] = y


def _ffn(x, mod, wup, wdn, layer, which, final_w=None):
    B, S, D = x.shape
    n_chunks = D_FF // FFN_FC
    final = final_w is not None
    in_specs = [
        pl.BlockSpec((1, FFN_TM, D), lambda b, t: (b, t, 0)),
        pl.BlockSpec((1, 3, D), lambda b, t: (b, 0, 0)),
        _resident_slice(wup.shape, (layer, which)),
        _resident_slice(wdn.shape, (layer, which)),
    ]
    args = [x, mod, wup, wdn]
    if final:
        in_specs.append(_resident((1, D)))
        args.append(final_w.reshape(1, D))
    return pl.pallas_call(
        functools.partial(_ffn_kernel, n_chunks=n_chunks, final=final),
        grid=(B, S // FFN_TM),
        in_specs=in_specs,
        out_specs=pl.BlockSpec((1, FFN_TM, D), lambda b, t: (b, t, 0)),
        out_shape=jax.ShapeDtypeStruct(x.shape, F32),
        compiler_params=_params("parallel", "parallel"),
        name="ffn_final" if final else "ffn",
    )(*args)


def _inproj_kernel(x_ref, mod_ref, w_ref, rq_ref, rk_ref, rv_ref, rg_ref, dq_ref, dk_ref, dv_ref,
                   gr_ref, gd_ref):
    x = x_ref[0]
    h = _modulated_norm(x, mod_ref[0]).astype(BF16)

    def proj(c):
        return jnp.dot(h, w_ref[0, :, c * D_MODEL:(c + 1) * D_MODEL], preferred_element_type=F32)

    u = proj(0).astype(BF16)
    for hh in range(RET_HEADS):
        rq_ref[0, hh] = u[:, hh * RET_QK_DIM:(hh + 1) * RET_QK_DIM]
        rk_ref[0, hh] = u[:, (RET_HEADS + hh) * RET_QK_DIM:(RET_HEADS + hh + 1) * RET_QK_DIM]
    u = proj(1).astype(BF16)
    for hh in range(RET_HEADS):
        rv_ref[0, hh] = u[:, hh * RET_V_DIM:(hh + 1) * RET_V_DIM]
    rg_ref[0] = proj(2).astype(BF16)
    u = (proj(3) * (DIFF_HEAD_DIM ** -0.5 * LOG2E)).astype(BF16)
    for hh in range(DIFF_HEADS):
        dq_ref[0, hh] = u[:, hh * DIFF_V_DIM:(hh + 1) * DIFF_V_DIM]
    u = proj(4).astype(BF16)
    for hh in range(DIFF_HEADS):
        dk_ref[0, hh] = u[:, hh * DIFF_V_DIM:(hh + 1) * DIFF_V_DIM]
    u = proj(5).astype(BF16)
    for hh in range(DIFF_HEADS):
        dv_ref[0, hh] = u[:, hh * DIFF_V_DIM:(hh + 1) * DIFF_V_DIM]
    gr_ref[0] = proj(6).astype(BF16)
    gd_ref[0] = proj(7).astype(BF16)


def _inproj(x, mod, w_in, layer):
    B, S, D = x.shape
    TM = PROJ_TM
    head_spec = lambda nh, w: pl.BlockSpec((1, nh, TM, w), lambda b, t: (b, 0, t, 0))
    tok_spec = pl.BlockSpec((1, TM, D), lambda b, t: (b, t, 0))
    sds = jax.ShapeDtypeStruct
    return pl.pallas_call(
        _inproj_kernel,
        grid=(B, S // TM),
        in_specs=[tok_spec, pl.BlockSpec((1, 3, D), lambda b, t: (b, 0, 0)),
                  _resident_slice(w_in.shape, (layer,))],
        out_specs=[
            head_spec(RET_HEADS, RET_QK_DIM), head_spec(RET_HEADS, RET_QK_DIM),
            head_spec(RET_HEADS, RET_V_DIM), tok_spec,
            head_spec(DIFF_HEADS, DIFF_V_DIM), head_spec(DIFF_HEADS, DIFF_V_DIM),
            head_spec(DIFF_HEADS, DIFF_V_DIM), tok_spec, tok_spec,
        ],
        out_shape=[
            sds((B, RET_HEADS, S, RET_QK_DIM), BF16), sds((B, RET_HEADS, S, RET_QK_DIM), BF16),
            sds((B, RET_HEADS, S, RET_V_DIM), BF16), sds((B, S, D), BF16),
            sds((B, DIFF_HEADS, S, DIFF_V_DIM), BF16), sds((B, DIFF_HEADS, S, DIFF_V_DIM), BF16),
            sds((B, DIFF_HEADS, S, DIFF_V_DIM), BF16), sds((B, S, D), BF16), sds((B, S, D), BF16),
        ],
        compiler_params=_params("parallel", "parallel"),
        name="mixer_inproj",
    )(x, mod, w_in)


def _retention_consts():
    L = RET_L
    gamma = 1.0 - 2.0 ** (-5.0 - jnp.arange(RET_HEADS, dtype=F32))
    log_g = jnp.log(gamma)
    r = jnp.arange(L, dtype=F32)
    scale = RET_QK_DIM ** -0.5
    allowed = (jnp.arange(L)[None, :] // CHUNK) <= (jnp.arange(L)[:, None] // CHUNK)
    intra = jnp.exp(log_g[:, None, None] * jnp.abs(r[:, None] - r[None, :]))
    intra = jnp.where(allowed[None], intra, 0.0) * scale
    q_dec = jnp.exp(log_g[:, None] * r[None, :]) * scale
    k_dec = jnp.exp(log_g[:, None] * (L - r)[None, :])
    blk_dec = jnp.exp(log_g * L)
    q_dec = jnp.broadcast_to(q_dec[:, :, None], (RET_HEADS, L, RET_QK_DIM))
    k_dec = jnp.broadcast_to(k_dec[:, :, None], (RET_HEADS, L, RET_QK_DIM))
    blk_dec = jnp.broadcast_to(blk_dec[:, None, None], (RET_HEADS, 1, RET_V_DIM))
    return intra, q_dec, k_dec, blk_dec


def _retention_kernel(q_ref, k_ref, v_ref, g_ref, w_ref, qd_ref, kd_ref, bd_ref, gn_ref, o_ref, state_ref):
    t = pl.program_id(1)

    @pl.when(t == 0)
    def _():
        state_ref[...] = jnp.zeros_like(state_ref)

    dv = v_ref.shape[-1]
    for h in range(q_ref.shape[1]):
        q = q_ref[0, h]
        k = k_ref[0, h]
        v = v_ref[0, h]
        s = lax.dot_general(q, k, (((1,), (1,)), ((), ())), preferred_element_type=F32)
        p = (s * w_ref[h]).astype(BF16)
        y = jnp.dot(p, v, preferred_element_type=F32)
        state = state_ref[h]
        qd = (q.astype(F32) * qd_ref[h]).astype(BF16)
        y = y + jnp.dot(qd, state.astype(BF16), preferred_element_type=F32)
        kd_t = (k.astype(F32) * kd_ref[h]).T.astype(BF16)
        state_ref[h] = state * bd_ref[h] + jnp.dot(kd_t, v, preferred_element_type=F32)
        ms = jnp.mean(y * y, axis=-1, keepdims=True)
        y = y * lax.rsqrt(ms + NORM_EPS) * gn_ref[h]
        g = g_ref[0, :, h * dv:(h + 1) * dv].astype(F32)
        o_ref[0, :, h * dv:(h + 1) * dv] = (y * (0.5 * g) * (1.0 + jnp.tanh(0.5 * g))).astype(BF16)


def _retention(rq, rk, rv, rg, ret_gn_l):
    B, H, S, dk = rq.shape
    dv = rv.shape[-1]
    L = RET_L
    intra, q_dec, k_dec, blk_dec = _retention_consts()
    blk = lambda w: pl.BlockSpec((1, H, L, w), lambda b, t: (b, 0, t, 0))
    tok = pl.BlockSpec((1, L, H * dv), lambda b, t: (b, t, 0))
    return pl.pallas_call(
        _retention_kernel,
        grid=(B, S // L),
        in_specs=[
            blk(dk), blk(dk), blk(dv), tok,
            _resident((H, L, L)), _resident((H, L, dk)), _resident((H, L, dk)), _resident((H, 1, dv)),
            _resident((H, 1, dv)),
        ],
        out_specs=tok,
        out_shape=jax.ShapeDtypeStruct((B, S, H * dv), BF16),
        scratch_shapes=[pltpu.VMEM((H, dk, dv), F32)],
        compiler_params=_params("parallel", "arbitrary"),
        name="retention",
    )(rq, rk, rv, rg, intra, q_dec, k_dec, blk_dec, ret_gn_l.reshape(H, 1, dv))


def _attn_bias_tables(n_heads):
    T = ATT_TQ
    slopes = 2.0 ** (-8.0 * jnp.arange(1, n_heads + 1, dtype=F32) / n_heads) * LOG2E
    r = jnp.arange(T, dtype=F32)
    a, c = r[:, None], r[None, :]
    allowed = (jnp.arange(T)[:, None] // CHUNK) <= (jnp.arange(T)[None, :] // CHUNK)
    diag = slopes[:, None, None] * (-2.0 * jnp.maximum(a - c, 0.0))[None]
    diag = jnp.where(allowed[None], diag, NEG_BIG)
    diag = jnp.concatenate([diag, diag], axis=-1)
    p0 = slopes.astype(BF16)
    p1 = (slopes - p0.astype(F32)).astype(BF16)
    p2 = (slopes - p0.astype(F32) - p1.astype(F32)).astype(BF16)
    parts = jnp.stack([p0, p1, p2], axis=1).astype(F32)
    coef = jnp.concatenate([parts * CHUNK, parts, jnp.zeros((n_heads, V7X_LANES - 2 * ATT_POS_LANES), F32)], axis=1)
    coef = jnp.broadcast_to(coef.astype(BF16)[:, :, None], (n_heads, V7X_LANES, V7X_LANES))
    return slopes, diag, coef


def _diffattn_kernel(slope_ref, q_ref, k_ref, v_ref, db_ref, coef_ref, lam_ref, sub_ref, o_ref,
                     vt_ref, kaug_ref, qbd_ref, s_ref, p_ref, bmax_ref, alpha_ref, m_ref, knorm_ref, acc_ref, *,
                     lam_init):
    h = pl.program_id(1)
    TK = ATT_TK
    n_kv = vt_ref.shape[0]

    @pl.when(jnp.logical_and(pl.program_id(0) == 0, h == 0))
    def _():
        for c in range(n_kv):
            ones_row = lax.broadcasted_iota(jnp.int32, (ATT_SUM_ROWS, TK), 0) == 0
            vt_ref[c, DIFF_V_DIM:DIFF_V_DIM + ATT_SUM_ROWS] = ones_row.astype(F32).astype(BF16)
            pos = c * TK + lax.broadcasted_iota(jnp.int32, (TK, V7X_LANES), 0)
            lane = lax.broadcasted_iota(jnp.int32, (TK, V7X_LANES), 1)
            feat = jnp.where(lane < ATT_POS_LANES, jnp.right_shift(pos, CHUNK_SHIFT),
                             jnp.where(lane < 2 * ATT_POS_LANES, jnp.bitwise_and(pos, CHUNK - 1), 0))
            kaug_ref[c * TK:(c + 1) * TK, 2 * DIFF_HEAD_DIM:] = feat.astype(F32).astype(BF16)

    k1sq = jnp.zeros((1, TK), F32)
    k2sq = jnp.zeros((1, TK), F32)
    for c in range(n_kv):
        vt_ref[c, 0:DIFF_V_DIM] = v_ref[0, 0, c * TK:(c + 1) * TK, :].astype(F32).T.astype(BF16)
        k = k_ref[0, 0, c * TK:(c + 1) * TK, :]
        kaug_ref[c * TK:(c + 1) * TK, 0:2 * DIFF_HEAD_DIM] = k
        ksq_t = jnp.square(k.astype(F32).T)
        k1sq = jnp.maximum(k1sq, jnp.sum(ksq_t[0:DIFF_HEAD_DIM], axis=0, keepdims=True))
        k2sq = jnp.maximum(k2sq, jnp.sum(ksq_t[DIFF_HEAD_DIM:], axis=0, keepdims=True))
    knorm_ref[0:1] = jnp.broadcast_to(jnp.max(k1sq, axis=1, keepdims=True), (1, V7X_LANES))
    knorm_ref[1:2] = jnp.broadcast_to(jnp.max(k2sq, axis=1, keepdims=True), (1, V7X_LANES))

    def group(g, carry):
        _diffattn_group(g, h, slope_ref, q_ref, db_ref, coef_ref, lam_ref, sub_ref, o_ref, vt_ref, kaug_ref,
                        qbd_ref, s_ref, p_ref, bmax_ref, alpha_ref, m_ref, knorm_ref, acc_ref, lam_init=lam_init)
        return carry

    lax.fori_loop(0, q_ref.shape[2] // (ATT_TILES * ATT_TQ), group, 0)


def _max_sq_norms(x):
    xsq = jnp.square(x.astype(F32))
    first = lax.broadcasted_iota(jnp.int32, xsq.shape, 1) < DIFF_HEAD_DIM
    n1 = jnp.sum(jnp.where(first, xsq, 0.0), axis=1, keepdims=True)
    n2 = jnp.sum(jnp.where(first, 0.0, xsq), axis=1, keepdims=True)
    return jnp.max(n1, axis=0, keepdims=True), jnp.max(n2, axis=0, keepdims=True)


def _diffattn_group(g, h, slope_ref, q_ref, db_ref, coef_ref, lam_ref, sub_ref, o_ref, vt_ref, kaug_ref,
                    qbd_ref, s_ref, p_ref, bmax_ref, alpha_ref, m_ref, knorm_ref, acc_ref, *, lam_init):
    TQ, TK = ATT_TQ, ATT_TK
    slope = slope_ref[h]
    tiles = range(ATT_TILES)
    tile_idx = [g * ATT_TILES + x for x in tiles]

    def q_rows(x):
        return pl.ds(pl.multiple_of(tile_idx[x] * TQ, TQ), TQ)

    for x in tiles:
        q = q_ref[0, 0, q_rows(x), :]
        q_t = q.astype(F32).T
        row = lax.broadcasted_iota(jnp.int32, q_t.shape, 0)
        qbd_ref[x, 0:2 * DIFF_HEAD_DIM, 0:TQ] = jnp.where(row < DIFF_HEAD_DIM, q_t, 0.0).astype(BF16)
        qbd_ref[x, 0:2 * DIFF_HEAD_DIM, TQ:2 * TQ] = jnp.where(row >= DIFF_HEAD_DIM, q_t, 0.0).astype(BF16)
        qbd_ref[x, 2 * DIFF_HEAD_DIM:, :] = jnp.concatenate([coef_ref[0]] * (2 * TQ // V7X_LANES), axis=1)
        m_ref[x] = jnp.full(m_ref.shape[1:], NEG_BIG, F32)
        acc_ref[x] = jnp.zeros(acc_ref.shape[1:], F32)

    def block_of(x, t):
        return jnp.maximum(tile_idx[x] - t, 0)

    def stage_scores(x, t, slot, diagonal):
        k = kaug_ref[pl.ds(pl.multiple_of(block_of(x, t) * TK, TK), TK), :]
        s = jnp.dot(k, qbd_ref[x], preferred_element_type=F32)
        bmax = None
        for r in range(0, TK, ATT_EXP_ROWS):
            rows = slice(r, r + ATT_EXP_ROWS)
            u = s[rows] + db_ref[0, rows] if diagonal else s[rows]
            s_ref[x, slot, rows] = u
            cmax = jnp.max(u.reshape(ATT_EXP_ROWS // V7X_SUBLANES, V7X_SUBLANES, 2 * TQ), axis=0)
            bmax = cmax if bmax is None else jnp.maximum(bmax, cmax)
        bmax_ref[x, slot] = jnp.max(bmax, axis=0, keepdims=True)

    def stage_softmax(x, t, slot, diagonal):
        if diagonal:
            shift = 0.0
        else:
            shift = jnp.where(t < n_steps[x], 0.0, NEG_BIG)
        m = m_ref[x]
        m_new = jnp.maximum(m, bmax_ref[x, slot] + shift)
        alpha = jnp.exp2(m - m_new)
        m_ref[x] = m_new
        alpha_ref[x, slot] = alpha
        ref = m_new - shift
        for r in range(0, TK, ATT_EXP_ROWS):
            rows = slice(r, r + ATT_EXP_ROWS)
            p_ref[x, slot, rows] = jnp.exp2(s_ref[x, slot, rows] - ref).astype(BF16)

    def stage_pv(x, t, slot):
        pv = jnp.dot(vt_ref[block_of(x, t)], p_ref[x, slot], preferred_element_type=F32)
        acc_ref[x] = alpha_ref[x, slot] * acc_ref[x] + pv

    for x in tiles:
        stage_scores(x, 0, 0, True)
    for x in tiles:
        stage_softmax(x, 0, 0, True)
    for x in tiles:
        stage_scores(x, 1, 1, False)

    n_steps = []
    for x in tiles:
        q1sq, q2sq = _max_sq_norms(q_ref[0, 0, q_rows(x), :])
        bound = jnp.sqrt(jnp.maximum(q1sq * knorm_ref[0:1, 0:1], q2sq * knorm_ref[1:2, 0:1])) * ATT_BOUND_MARGIN
        m_min = jnp.min(m_ref[x], axis=1, keepdims=True)
        idx = tile_idx[x].astype(F32)
        reach = idx + (ATT_EXP2_ZERO + bound - m_min) / (slope * TK)
        n = jnp.minimum(jnp.floor(reach) + 2.0, idx + 1.0)
        n_steps.append(jnp.max(n).astype(jnp.int32))

    def pipeline_tick(t, slot):
        for x in tiles:
            stage_pv(x, t, slot)
        for x in tiles:
            stage_softmax(x, t + 1, 1 - slot, False)
        for x in tiles:
            stage_scores(x, t + 2, slot, False)

    def pipeline_drain(t, slot):
        for x in tiles:
            stage_pv(x, t, slot)
        for x in tiles:
            stage_softmax(x, t + 1, 1 - slot, False)
        for x in tiles:
            stage_pv(x, t + 1, 1 - slot)

    def by_parity(fn, t):
        for slot in (0, 1):
            @pl.when(jnp.bitwise_and(t, 1) == slot)
            def _():
                fn(t, slot)

    def body(t, carry):
        by_parity(pipeline_tick, t)
        return carry

    n_ticks = jnp.maximum(functools.reduce(jnp.maximum, n_steps), 2)
    lax.fori_loop(0, n_ticks - 2, body, 0)
    by_parity(pipeline_drain, n_ticks - 2)
    lp = lam_ref[...]
    lam = (jnp.exp(jnp.sum(lp[0:1] * lp[1:2], axis=-1, keepdims=True))
           - jnp.exp(jnp.sum(lp[2:3] * lp[3:4], axis=-1, keepdims=True)) + lam_init)
    for x in tiles:
        o = acc_ref[x, 0:DIFF_V_DIM] * (1.0 / acc_ref[x, DIFF_V_DIM:DIFF_V_DIM + 1])
        y = (o[:, :TQ] - lam * o[:, TQ:]).T
        ms = jnp.mean(y * y, axis=-1, keepdims=True)
        y = y * lax.rsqrt(ms + NORM_EPS) * sub_ref[...] * (1.0 - lam_init)
        o_ref[0, q_rows(x)] = y.astype(BF16)


def _diffattn(dq, dk, dv, lam_params, subln, layer_idx):
    B, H, S, w = dq.shape
    lam_init = 0.8 - 0.6 * math.exp(-0.3 * layer_idx)
    slopes, diag_bias, coef = _attn_bias_tables(H)
    NT = ATT_TILES
    seq = pl.BlockSpec((1, 1, S, w), lambda b, h, sl: (b, h, 0, 0))
    grid_spec = pltpu.PrefetchScalarGridSpec(
        num_scalar_prefetch=1,
        grid=(B, H),
        in_specs=[
            seq, seq, seq,
            pl.BlockSpec((1, ATT_TK, 2 * ATT_TQ), lambda b, h, sl: (h, 0, 0)),
            pl.BlockSpec((1, V7X_LANES, V7X_LANES), lambda b, h, sl: (h, 0, 0)),
            pl.BlockSpec((4, DIFF_HEAD_DIM), lambda b, h, sl: (0, 0)),
            pl.BlockSpec((1, w), lambda b, h, sl: (0, 0)),
        ],
        out_specs=pl.BlockSpec((1, S, w), lambda b, h, sl: (b, 0, h)),
        scratch_shapes=[
            pltpu.VMEM((S // ATT_TK, w + ATT_SUM_ROWS, ATT_TK), BF16),
            pltpu.VMEM((S, 2 * w), BF16),
            pltpu.VMEM((NT, 2 * w, 2 * ATT_TQ), BF16),
            pltpu.VMEM((NT, 2, ATT_TK, 2 * ATT_TQ), F32),
            pltpu.VMEM((NT, 2, ATT_TK, 2 * ATT_TQ), BF16),
            pltpu.VMEM((NT, 2, 1, 2 * ATT_TQ), F32),
            pltpu.VMEM((NT, 2, 1, 2 * ATT_TQ), F32),
            pltpu.VMEM((NT, 1, 2 * ATT_TQ), F32),
            pltpu.VMEM((V7X_SUBLANES, V7X_LANES), F32),
            pltpu.VMEM((NT, w + ATT_SUM_ROWS, 2 * ATT_TQ), F32),
        ],
    )
    return pl.pallas_call(
        functools.partial(_diffattn_kernel, lam_init=lam_init),
        grid_spec=grid_spec,
        out_shape=jax.ShapeDtypeStruct((B, S, H * w), BF16),
        compiler_params=_params("arbitrary", "arbitrary"),
        name="diff_attention",
    )(slopes, dq, dk, dv, diag_bias, coef, lam_params, subln.reshape(1, w))


def _merge_kernel(x_ref, mod_ref, yr_ref, yd_ref, gr_ref, gd_ref, wr_ref, wd_ref, wo_ref, o_ref):
    def sigmoid(g):
        return 0.5 * (1.0 + jnp.tanh(0.5 * g))

    br = jnp.dot(yr_ref[0], wr_ref[0], preferred_element_type=F32)
    bd = jnp.dot(yd_ref[0], wd_ref[0], preferred_element_type=F32)
    merged = sigmoid(gr_ref[0].astype(F32)) * br + sigmoid(gd_ref[0].astype(F32)) * bd
    out = jnp.dot(merged.astype(BF16), wo_ref[0], preferred_element_type=F32)
    o_ref[0] = x_ref[0] + mod_ref[0][2:3] * out


def _merge(x, mod, y_ret, y_diff, g_ret, g_diff, w_rb, w_db, w_o, layer):
    B, S, D = x.shape
    tok = pl.BlockSpec((1, MERGE_TM, D), lambda b, t: (b, t, 0))
    return pl.pallas_call(
        _merge_kernel,
        grid=(B, S // MERGE_TM),
        in_specs=[tok, pl.BlockSpec((1, 3, D), lambda b, t: (b, 0, 0)), tok, tok, tok, tok,
                  _resident_slice(w_rb.shape, (layer,)), _resident_slice(w_db.shape, (layer,)),
                  _resident_slice(w_o.shape, (layer,))],
        out_specs=tok,
        out_shape=jax.ShapeDtypeStruct(x.shape, F32),
        compiler_params=_params("parallel", "parallel"),
        name="mixer_merge",
    )(x, mod, y_ret, y_diff, g_ret, g_diff, w_rb, w_db, w_o)


def kernel(x, c, w_ada, b_ada, norm_w, w_ffn_up, w_ffn_down, w_in, ret_gn, lambda_q1, lambda_k1, lambda_q2,
           lambda_k2, diff_subln, w_ret_branch, w_diff_branch, w_out, final_norm):
    B, S, D = x.shape
    assert D == D_MODEL and S % max(FFN_TM, RET_L, ATT_TILES * ATT_TQ) == 0 and ATT_TQ == ATT_TK
    mod_all = _adaln(c, w_ada, b_ada, norm_w).reshape(DEPTH, B, N_SUB, 3, D)
    up, down, w_in_b = w_ffn_up.astype(BF16), w_ffn_down.astype(BF16), w_in.astype(BF16)
    w_rb, w_db, w_o = w_ret_branch.astype(BF16), w_diff_branch.astype(BF16), w_out.astype(BF16)
    for l in range(DEPTH):
        mod = [mod_all[l, :, s] for s in range(N_SUB)]
        x = _ffn(x, mod[0], up, down, l, 0)
        rq, rk, rv, rg, dq, dk, dv, g_ret, g_diff = _inproj(x, mod[1], w_in_b, l)
        y_ret = _retention(rq, rk, rv, rg, ret_gn[l])
        lam_params = jnp.stack([lambda_q1[l], lambda_k1[l], lambda_q2[l], lambda_k2[l]])
        y_diff = _diffattn(dq, dk, dv, lam_params, diff_subln[l], l)
        x = _merge(x, mod[1], y_ret, y_diff, g_ret, g_diff, w_rb, w_db, w_o, l)
        x = _ffn(x, mod[2], up, down, l, 1, final_w=final_norm if l == DEPTH - 1 else None)
    return x
```

```python
import functools
import math

import jax
import jax.numpy as jnp
from jax import lax
from jax.experimental import pallas as pl
from jax.experimental.pallas import tpu as pltpu

DEPTH = 4
D_MODEL = 1024
CHUNK = 64
CHUNK_SHIFT = 6
RET_HEADS = 4
RET_QK_DIM = 128
RET_V_DIM = 256
DIFF_HEADS = 8
DIFF_HEAD_DIM = 64
DIFF_V_DIM = 2 * DIFF_HEAD_DIM
D_FF = 2816
N_SUB = 3
NORM_EPS = 1e-6
IN_WIDTH = 8 * D_MODEL

F32 = jnp.float32
BF16 = jnp.bfloat16

V7X_VMEM_LIMIT_BYTES = 56 * 1024 * 1024
V7X_LANES = 128
V7X_SUBLANES = 8

FFN_TM = 1024
PROJ_TM = 512
MERGE_TM = 1024
FFN_FC = 256
RET_L = 256
ATT_TQ = 512
ATT_KQ = 1
ATT_TK = ATT_KQ * ATT_TQ
ATT_EXP_ROWS = 16
ATT_TILES = 2
ATT_POS_LANES = 3
ATT_SUM_ROWS = 16
ATT_EXP2_ZERO = 160.0
ATT_BOUND_MARGIN = 1.001
NEG_BIG = -1e30
LOG2E = math.log2(math.e)


def _resident(shape):
    nd = len(shape)
    return pl.BlockSpec(shape, lambda *_: (0,) * nd, pipeline_mode=pl.Buffered(1))


def _resident_slice(shape, lead):
    n_lead, nd = len(lead), len(shape)
    index = tuple(lead) + (0,) * (nd - n_lead)
    return pl.BlockSpec((1,) * n_lead + tuple(shape[n_lead:]), lambda *_: index, pipeline_mode=pl.Buffered(1))


def _params(*sem, flags=None):
    return pltpu.CompilerParams(dimension_semantics=sem, vmem_limit_bytes=V7X_VMEM_LIMIT_BYTES, flags=flags)


def _adaln_kernel(c_ref, w_ref, b_ref, nw_ref, o_ref):
    j = pl.program_id(1)
    c = c_ref[...]
    cond = c / (1.0 + jnp.exp(-c))
    r = jnp.dot(cond, w_ref[0], preferred_element_type=F32,
                precision=lax.Precision.HIGHEST) + b_ref[0]
    kind = j % 3
    sub = j // 3
    r = jnp.where(kind == 1, (1.0 + r) * nw_ref[0, 0], r)
    r = jnp.where(jnp.logical_and(kind == 2, sub != 1), 0.5 * r, r)
    o_ref[0] = r


def _adaln(c, w_ada, b_ada, norm_w):
    B = c.shape[0]
    n_tiles = N_SUB * 3
    return pl.pallas_call(
        _adaln_kernel,
        grid=(DEPTH, n_tiles),
        in_specs=[
            pl.BlockSpec((B, D_MODEL), lambda l, j: (0, 0)),
            pl.BlockSpec((1, D_MODEL, D_MODEL), lambda l, j: (l, 0, j)),
            pl.BlockSpec((1, 1, D_MODEL), lambda l, j: (l, 0, j)),
            pl.BlockSpec((1, 1, 1, D_MODEL), lambda l, j: (l, j // 3, 0, 0)),
        ],
        out_specs=pl.BlockSpec((1, B, D_MODEL), lambda l, j: (l, 0, j)),
        out_shape=jax.ShapeDtypeStruct((DEPTH, B, n_tiles * D_MODEL), F32),
        compiler_params=_params("arbitrary", "arbitrary"),
        name="adaln_mod",
    )(c, w_ada, b_ada.reshape(DEPTH, 1, n_tiles * D_MODEL), norm_w.reshape(DEPTH, N_SUB, 1, D_MODEL))


def _modulated_norm(x, mod):
    ms = jnp.mean(x * x, axis=-1, keepdims=True)
    return x * lax.rsqrt(ms + NORM_EPS) * mod[1:2] + mod[0:1]


def _ffn_kernel(x_ref, mod_ref, wup_ref, wdn_ref, *rest, n_chunks, final):
    if final:
        fw_ref, o_ref = rest
    else:
        (o_ref,) = rest
    x = x_ref[0]
    mod = mod_ref[0]
    h = _modulated_norm(x, mod).astype(BF16)
    acc = jnp.zeros(x.shape, F32)
    for c in range(n_chunks):
        cols = slice(c * FFN_FC, (c + 1) * FFN_FC)
        a = jnp.dot(h, wup_ref[0, 0, :, cols], preferred_element_type=F32)
        b = jnp.dot(h, wup_ref[0, 0, :, D_FF + c * FFN_FC:D_FF + (c + 1) * FFN_FC], preferred_element_type=F32)
        act = (0.5 * a) * (1.0 + jnp.tanh(0.5 * a)) * b
        acc = acc + jnp.dot(act.astype(BF16), wdn_ref[0, 0, cols, :], preferred_element_type=F32)
    y = x + mod[2:3] * acc
    if final:
        ms = jnp.mean(y * y, axis=-1, keepdims=True)
        y = y * lax.rsqrt(ms + NORM_EPS) * fw_ref[...]
    o_ref[0] = y


def _ffn(x, mod, wup, wdn, layer, which, final_w=None):
    B, S, D = x.shape
    n_chunks = D_FF // FFN_FC
    final = final_w is not None
    in_specs = [
        pl.BlockSpec((1, FFN_TM, D), lambda b, t: (b, t, 0)),
        pl.BlockSpec((1, 3, D), lambda b, t: (b, 0, 0)),
        _resident_slice(wup.shape, (layer, which)),
        _resident_slice(wdn.shape, (layer, which)),
    ]
    args = [x, mod, wup, wdn]
    if final:
        in_specs.append(_resident((1, D)))
        args.append(final_w.reshape(1, D))
    return pl.pallas_call(
        functools.partial(_ffn_kernel, n_chunks=n_chunks, final=final),
        grid=(B, S // FFN_TM),
        in_specs=in_specs,
        out_specs=pl.BlockSpec((1, FFN_TM, D), lambda b, t: (b, t, 0)),
        out_shape=jax.ShapeDtypeStruct(x.shape, F32),
        compiler_params=_params("parallel", "parallel"),
        name="ffn_final" if final else "ffn",
    )(*args)


def _inproj_kernel(x_ref, mod_ref, w_ref, rq_ref, rk_ref, rv_ref, rg_ref, dq_ref, dk_ref, dv_ref,
                   gr_ref, gd_ref):
    x = x_ref[0]
    h = _modulated_norm(x, mod_ref[0]).astype(BF16)

    def proj(c):
        return jnp.dot(h, w_ref[0, :, c * D_MODEL:(c + 1) * D_MODEL], preferred_element_type=F32)

    u = proj(0).astype(BF16)
    for hh in range(RET_HEADS):
        rq_ref[0, hh] = u[:, hh * RET_QK_DIM:(hh + 1) * RET_QK_DIM]
        rk_ref[0, hh] = u[:, (RET_HEADS + hh) * RET_QK_DIM:(RET_HEADS + hh + 1) * RET_QK_DIM]
    u = proj(1).astype(BF16)
    for hh in range(RET_HEADS):
        rv_ref[0, hh] = u[:, hh * RET_V_DIM:(hh + 1) * RET_V_DIM]
    rg_ref[0] = proj(2).astype(BF16)
    u = (proj(3) * (DIFF_HEAD_DIM ** -0.5 * LOG2E)).astype(BF16)
    for hh in range(DIFF_HEADS):
        dq_ref[0, hh] = u[:, hh * DIFF_V_DIM:(hh + 1) * DIFF_V_DIM]
    u = proj(4).astype(BF16)
    for hh in range(DIFF_HEADS):
        dk_ref[0, hh] = u[:, hh * DIFF_V_DIM:(hh + 1) * DIFF_V_DIM]
    u = proj(5).astype(BF16)
    for hh in range(DIFF_HEADS):
        dv_ref[0, hh] = u[:, hh * DIFF_V_DIM:(hh + 1) * DIFF_V_DIM]
    gr_ref[0] = proj(6).astype(BF16)
    gd_ref[0] = proj(7).astype(BF16)


def _inproj(x, mod, w_in, layer):
    B, S, D = x.shape
    TM = PROJ_TM
    head_spec = lambda nh, w: pl.BlockSpec((1, nh, TM, w), lambda b, t: (b, 0, t, 0))
    tok_spec = pl.BlockSpec((1, TM, D), lambda b, t: (b, t, 0))
    sds = jax.ShapeDtypeStruct
    return pl.pallas_call(
        _inproj_kernel,
        grid=(B, S // TM),
        in_specs=[tok_spec, pl.BlockSpec((1, 3, D), lambda b, t: (b, 0, 0)),
                  _resident_slice(w_in.shape, (layer,))],
        out_specs=[
            head_spec(RET_HEADS, RET_QK_DIM), head_spec(RET_HEADS, RET_QK_DIM),
            head_spec(RET_HEADS, RET_V_DIM), tok_spec,
            head_spec(DIFF_HEADS, DIFF_V_DIM), head_spec(DIFF_HEADS, DIFF_V_DIM),
            head_spec(DIFF_HEADS, DIFF_V_DIM), tok_spec, tok_spec,
        ],
        out_shape=[
            sds((B, RET_HEADS, S, RET_QK_DIM), BF16), sds((B, RET_HEADS, S, RET_QK_DIM), BF16),
            sds((B, RET_HEADS, S, RET_V_DIM), BF16), sds((B, S, D), BF16),
            sds((B, DIFF_HEADS, S, DIFF_V_DIM), BF16), sds((B, DIFF_HEADS, S, DIFF_V_DIM), BF16),
            sds((B, DIFF_HEADS, S, DIFF_V_DIM), BF16), sds((B, S, D), BF16), sds((B, S, D), BF16),
        ],
        compiler_params=_params("parallel", "parallel"),
        name="mixer_inproj",
    )(x, mod, w_in)


def _retention_consts():
    L = RET_L
    gamma = 1.0 - 2.0 ** (-5.0 - jnp.arange(RET_HEADS, dtype=F32))
    log_g = jnp.log(gamma)
    r = jnp.arange(L, dtype=F32)
    scale = RET_QK_DIM ** -0.5
    allowed = (jnp.arange(L)[None, :] // CHUNK) <= (jnp.arange(L)[:, None] // CHUNK)
    intra = jnp.exp(log_g[:, None, None] * jnp.abs(r[:, None] - r[None, :]))
    intra = jnp.where(allowed[None], intra, 0.0) * scale
    q_dec = jnp.exp(log_g[:, None] * r[None, :]) * scale
    k_dec = jnp.exp(log_g[:, None] * (L - r)[None, :])
    blk_dec = jnp.exp(log_g * L)
    q_dec = jnp.broadcast_to(q_dec[:, :, None], (RET_HEADS, L, RET_QK_DIM))
    k_dec = jnp.broadcast_to(k_dec[:, :, None], (RET_HEADS, L, RET_QK_DIM))
    blk_dec = jnp.broadcast_to(blk_dec[:, None, None], (RET_HEADS, 1, RET_V_DIM))
    return intra, q_dec, k_dec, blk_dec


def _retention_kernel(q_ref, k_ref, v_ref, g_ref, w_ref, qd_ref, kd_ref, bd_ref, gn_ref, o_ref, state_ref):
    t = pl.program_id(1)

    @pl.when(t == 0)
    def _():
        state_ref[...] = jnp.zeros_like(state_ref)

    dv = v_ref.shape[-1]
    for h in range(q_ref.shape[1]):
        q = q_ref[0, h]
        k = k_ref[0, h]
        v = v_ref[0, h]
        s = lax.dot_general(q, k, (((1,), (1,)), ((), ())), preferred_element_type=F32)
        p = (s * w_ref[h]).astype(BF16)
        y = jnp.dot(p, v, preferred_element_type=F32)
        state = state_ref[h]
        qd = (q.astype(F32) * qd_ref[h]).astype(BF16)
        y = y + jnp.dot(qd, state.astype(BF16), preferred_element_type=F32)
        kd_t = (k.astype(F32) * kd_ref[h]).T.astype(BF16)
        state_ref[h] = state * bd_ref[h] + jnp.dot(kd_t, v, preferred_element_type=F32)
        ms = jnp.mean(y * y, axis=-1, keepdims=True)
        y = y * lax.rsqrt(ms + NORM_EPS) * gn_ref[h]
        g = g_ref[0, :, h * dv:(h + 1) * dv].astype(F32)
        o_ref[0, :, h * dv:(h + 1) * dv] = (y * (0.5 * g) * (1.0 + jnp.tanh(0.5 * g))).astype(BF16)


def _retention(rq, rk, rv, rg, ret_gn_l):
    B, H, S, dk = rq.shape
    dv = rv.shape[-1]
    L = RET_L
    intra, q_dec, k_dec, blk_dec = _retention_consts()
    blk = lambda w: pl.BlockSpec((1, H, L, w), lambda b, t: (b, 0, t, 0))
    tok = pl.BlockSpec((1, L, H * dv), lambda b, t: (b, t, 0))
    return pl.pallas_call(
        _retention_kernel,
        grid=(B, S // L),
        in_specs=[
            blk(dk), blk(dk), blk(dv), tok,
            _resident((H, L, L)), _resident((H, L, dk)), _resident((H, L, dk)), _resident((H, 1, dv)),
            _resident((H, 1, dv)),
        ],
        out_specs=tok,
        out_shape=jax.ShapeDtypeStruct((B, S, H * dv), BF16),
        scratch_shapes=[pltpu.VMEM((H, dk, dv), F32)],
        compiler_params=_params("parallel", "arbitrary"),
        name="retention",
    )(rq, rk, rv, rg, intra, q_dec, k_dec, blk_dec, ret_gn_l.reshape(H, 1, dv))


def _attn_bias_tables(n_heads):
    slopes = 2.0 ** (-8.0 * jnp.arange(1, n_heads + 1, dtype=F32) / n_heads) * LOG2E
    a = jnp.arange(ATT_TK)[None, :, None]
    c = (jnp.arange(ATT_KQ) * ATT_TQ)[:, None, None] + jnp.arange(ATT_TQ)[None, None, :]
    allowed = (a // CHUNK) <= (c // CHUNK)
    diag = slopes[:, None, None, None] * (-2.0 * jnp.maximum(a - c, 0).astype(F32))[None]
    diag = jnp.where(allowed[None], diag, NEG_BIG)
    diag = jnp.concatenate([diag, diag], axis=-1)
    p0 = slopes.astype(BF16)
    p1 = (slopes - p0.astype(F32)).astype(BF16)
    p2 = (slopes - p0.astype(F32) - p1.astype(F32)).astype(BF16)
    parts = jnp.stack([p0, p1, p2], axis=1).astype(F32)
    coef = jnp.concatenate([parts * CHUNK, parts, jnp.zeros((n_heads, V7X_LANES - 2 * ATT_POS_LANES), F32)], axis=1)
    coef = jnp.broadcast_to(coef.astype(BF16)[:, :, None], (n_heads, V7X_LANES, V7X_LANES))
    return slopes, diag, coef


def _diffattn_kernel(slope_ref, q_ref, k_ref, v_ref, db_ref, coef_ref, lam_ref, sub_ref, o_ref,
                     vt_ref, kaug_ref, qbd_ref, s_ref, p_ref, bmax_ref, alpha_ref, m_ref, knorm_ref, acc_ref,
                     fin_ref, *,
                     lam_init):
    h = pl.program_id(1)
    TK = ATT_TK
    n_kv = vt_ref.shape[0]

    @pl.when(jnp.logical_and(pl.program_id(0) == 0, h == 0))
    def _():
        for c in range(n_kv):
            ones_row = lax.broadcasted_iota(jnp.int32, (ATT_SUM_ROWS, TK), 0) == 0
            vt_ref[c, DIFF_V_DIM:DIFF_V_DIM + ATT_SUM_ROWS] = ones_row.astype(F32).astype(BF16)
            pos = c * TK + lax.broadcasted_iota(jnp.int32, (TK, V7X_LANES), 0)
            lane = lax.broadcasted_iota(jnp.int32, (TK, V7X_LANES), 1)
            feat = jnp.where(lane < ATT_POS_LANES, jnp.right_shift(pos, CHUNK_SHIFT),
                             jnp.where(lane < 2 * ATT_POS_LANES, jnp.bitwise_and(pos, CHUNK - 1), 0))
            kaug_ref[c * TK:(c + 1) * TK, 2 * DIFF_HEAD_DIM:] = feat.astype(F32).astype(BF16)

    k1sq = jnp.zeros((1, TK), F32)
    k2sq = jnp.zeros((1, TK), F32)
    for c in range(n_kv):
        vt_ref[c, 0:DIFF_V_DIM] = v_ref[0, 0, c * TK:(c + 1) * TK, :].astype(F32).T.astype(BF16)
        k = k_ref[0, 0, c * TK:(c + 1) * TK, :]
        kaug_ref[c * TK:(c + 1) * TK, 0:2 * DIFF_HEAD_DIM] = k
        ksq_t = jnp.square(k.astype(F32).T)
        k1sq = jnp.maximum(k1sq, jnp.sum(ksq_t[0:DIFF_HEAD_DIM], axis=0, keepdims=True))
        k2sq = jnp.maximum(k2sq, jnp.sum(ksq_t[DIFF_HEAD_DIM:], axis=0, keepdims=True))
    knorm_ref[0:1] = jnp.broadcast_to(jnp.max(k1sq, axis=1, keepdims=True), (1, V7X_LANES))
    knorm_ref[1:2] = jnp.broadcast_to(jnp.max(k2sq, axis=1, keepdims=True), (1, V7X_LANES))

    def group(g, carry):
        _diffattn_group(g, h, slope_ref, q_ref, db_ref, coef_ref, lam_ref, sub_ref, o_ref, vt_ref, kaug_ref,
                        qbd_ref, s_ref, p_ref, bmax_ref, alpha_ref, m_ref, knorm_ref, acc_ref, fin_ref,
                        lam_init=lam_init)
        return carry

    fin_ref[...] = jnp.zeros(fin_ref.shape, F32)
    fin_ref[:, DIFF_V_DIM:DIFF_V_DIM + 1] = jnp.ones((ATT_TILES, 1, fin_ref.shape[2]), F32)
    n_groups = q_ref.shape[2] // (ATT_TILES * ATT_TQ)
    lax.fori_loop(0, n_groups, group, 0)
    _diffattn_finish(n_groups - 1, lam_ref, sub_ref, o_ref, fin_ref, lam_init=lam_init)


def _max_sq_norms(x):
    xsq = jnp.square(x.astype(F32))
    first = lax.broadcasted_iota(jnp.int32, xsq.shape, 1) < DIFF_HEAD_DIM
    n1 = jnp.sum(jnp.where(first, xsq, 0.0), axis=1, keepdims=True)
    n2 = jnp.sum(jnp.where(first, 0.0, xsq), axis=1, keepdims=True)
    return jnp.max(n1, axis=0, keepdims=True), jnp.max(n2, axis=0, keepdims=True)


def _diffattn_group(g, h, slope_ref, q_ref, db_ref, coef_ref, lam_ref, sub_ref, o_ref, vt_ref, kaug_ref,
                    qbd_ref, s_ref, p_ref, bmax_ref, alpha_ref, m_ref, knorm_ref, acc_ref, fin_ref, *, lam_init):
    TQ, TK = ATT_TQ, ATT_TK
    slope = slope_ref[h]
    tiles = range(ATT_TILES)
    tile_idx = [g * ATT_TILES + x for x in tiles]
    diag_blk = [g * (ATT_TILES // ATT_KQ) + x // ATT_KQ for x in tiles]

    def q_rows(x):
        return pl.ds(pl.multiple_of(tile_idx[x] * TQ, TQ), TQ)

    for x in tiles:
        q = q_ref[0, 0, q_rows(x), :]
        q_t = q.astype(F32).T
        row = lax.broadcasted_iota(jnp.int32, q_t.shape, 0)
        qbd_ref[x, 0:2 * DIFF_HEAD_DIM, 0:TQ] = jnp.where(row < DIFF_HEAD_DIM, q_t, 0.0).astype(BF16)
        qbd_ref[x, 0:2 * DIFF_HEAD_DIM, TQ:2 * TQ] = jnp.where(row >= DIFF_HEAD_DIM, q_t, 0.0).astype(BF16)
        qbd_ref[x, 2 * DIFF_HEAD_DIM:, :] = jnp.concatenate([coef_ref[0]] * (2 * TQ // V7X_LANES), axis=1)
        m_ref[x] = jnp.full(m_ref.shape[1:], NEG_BIG, F32)
        acc_ref[x] = jnp.zeros(acc_ref.shape[1:], F32)

    def block_of(x, t):
        return jnp.maximum(diag_blk[x] - t, 0)

    def stage_scores(x, t, slot, diagonal):
        k = kaug_ref[pl.ds(pl.multiple_of(block_of(x, t) * TK, TK), TK), :]
        s = jnp.dot(k, qbd_ref[x], preferred_element_type=F32)
        bmax = None
        for r in range(0, TK, ATT_EXP_ROWS):
            rows = slice(r, r + ATT_EXP_ROWS)
            u = s[rows] + db_ref[0, x % ATT_KQ, rows] if diagonal else s[rows]
            s_ref[x, slot, rows] = u
            cmax = jnp.max(u.reshape(ATT_EXP_ROWS // V7X_SUBLANES, V7X_SUBLANES, 2 * TQ), axis=0)
            bmax = cmax if bmax is None else jnp.maximum(bmax, cmax)
        bmax_ref[x, slot] = jnp.max(bmax, axis=0, keepdims=True)

    def stage_softmax(x, t, slot, diagonal):
        if diagonal:
            shift = 0.0
        else:
            shift = jnp.where(t < n_steps[x], 0.0, NEG_BIG)
        m = m_ref[x]
        m_new = jnp.maximum(m, bmax_ref[x, slot] + shift)
        alpha = jnp.exp2(m - m_new)
        m_ref[x] = m_new
        alpha_ref[x, slot] = alpha
        ref = m_new - shift
        for r in range(0, TK, ATT_EXP_ROWS):
            rows = slice(r, r + ATT_EXP_ROWS)
            p_ref[x, slot, rows] = jnp.exp2(s_ref[x, slot, rows] - ref).astype(BF16)

    def stage_pv(x, t, slot, last=False):
        pv = jnp.dot(vt_ref[block_of(x, t)], p_ref[x, slot], preferred_element_type=F32)
        out_ref = fin_ref if last else acc_ref
        out_ref[x] = alpha_ref[x, slot] * acc_ref[x] + pv

    _diffattn_finish(jnp.maximum(g - 1, 0), lam_ref, sub_ref, o_ref, fin_ref, lam_init=lam_init)
    for x in tiles:
        stage_scores(x, 0, 0, True)
    for x in tiles:
        stage_softmax(x, 0, 0, True)
    for x in tiles:
        stage_scores(x, 1, 1, False)

    n_steps = []
    for x in tiles:
        q1sq, q2sq = _max_sq_norms(q_ref[0, 0, q_rows(x), :])
        bound = jnp.sqrt(jnp.maximum(q1sq * knorm_ref[0:1, 0:1], q2sq * knorm_ref[1:2, 0:1])) * ATT_BOUND_MARGIN
        m_min = jnp.min(m_ref[x], axis=1, keepdims=True)
        blk = diag_blk[x].astype(F32)
        reach = blk + (ATT_EXP2_ZERO + bound - m_min) / (slope * TK)
        n = jnp.minimum(jnp.floor(reach) + 2.0, blk + 1.0)
        n_steps.append(jnp.max(n).astype(jnp.int32))

    def pipeline_tick(t, slot):
        for x in tiles:
            stage_pv(x, t, slot)
        for x in tiles:
            stage_softmax(x, t + 1, 1 - slot, False)
        for x in tiles:
            stage_scores(x, t + 2, slot, False)

    def pipeline_drain(t, slot):
        for x in tiles:
            stage_pv(x, t, slot)
        for x in tiles:
            stage_softmax(x, t + 1, 1 - slot, False)
        for x in tiles:
            stage_pv(x, t + 1, 1 - slot, last=True)

    def by_parity(fn, t):
        for slot in (0, 1):
            @pl.when(jnp.bitwise_and(t, 1) == slot)
            def _():
                fn(t, slot)

    def body(t, carry):
        by_parity(pipeline_tick, t)
        return carry

    n_ticks = jnp.maximum(functools.reduce(jnp.maximum, n_steps), 2)
    lax.fori_loop(0, n_ticks - 2, body, 0)
    by_parity(pipeline_drain, n_ticks - 2)


def _diffattn_finish(g, lam_ref, sub_ref, o_ref, fin_ref, *, lam_init):
    TQ = ATT_TQ
    lp = lam_ref[...]
    lam = (jnp.exp(jnp.sum(lp[0:1] * lp[1:2], axis=-1, keepdims=True))
           - jnp.exp(jnp.sum(lp[2:3] * lp[3:4], axis=-1, keepdims=True)) + lam_init)
    for x in range(ATT_TILES):
        o = fin_ref[x, 0:DIFF_V_DIM] * (1.0 / fin_ref[x, DIFF_V_DIM:DIFF_V_DIM + 1])
        y = (o[:, :TQ] - lam * o[:, TQ:]).T
        ms = jnp.mean(y * y, axis=-1, keepdims=True)
        y = y * lax.rsqrt(ms + NORM_EPS) * sub_ref[...] * (1.0 - lam_init)
        o_ref[0, pl.ds(pl.multiple_of((g * ATT_TILES + x) * TQ, TQ), TQ)] = y.astype(BF16)


def _diffattn(dq, dk, dv, lam_params, subln, layer_idx):
    B, H, S, w = dq.shape
    lam_init = 0.8 - 0.6 * math.exp(-0.3 * layer_idx)
    slopes, diag_bias, coef = _attn_bias_tables(H)
    NT = ATT_TILES
    seq = pl.BlockSpec((1, 1, S, w), lambda b, h, sl: (b, h, 0, 0))
    grid_spec = pltpu.PrefetchScalarGridSpec(
        num_scalar_prefetch=1,
        grid=(B, H),
        in_specs=[
            seq, seq, seq,
            pl.BlockSpec((1, ATT_KQ, ATT_TK, 2 * ATT_TQ), lambda b, h, sl: (h, 0, 0, 0)),
            pl.BlockSpec((1, V7X_LANES, V7X_LANES), lambda b, h, sl: (h, 0, 0)),
            pl.BlockSpec((4, DIFF_HEAD_DIM), lambda b, h, sl: (0, 0)),
            pl.BlockSpec((1, w), lambda b, h, sl: (0, 0)),
        ],
        out_specs=pl.BlockSpec((1, S, w), lambda b, h, sl: (b, 0, h)),
        scratch_shapes=[
            pltpu.VMEM((S // ATT_TK, w + ATT_SUM_ROWS, ATT_TK), BF16),
            pltpu.VMEM((S, 2 * w), BF16),
            pltpu.VMEM((NT, 2 * w, 2 * ATT_TQ), BF16),
            pltpu.VMEM((NT, 2, ATT_TK, 2 * ATT_TQ), F32),
            pltpu.VMEM((NT, 2, ATT_TK, 2 * ATT_TQ), BF16),
            pltpu.VMEM((NT, 2, 1, 2 * ATT_TQ), F32),
            pltpu.VMEM((NT, 2, 1, 2 * ATT_TQ), F32),
            pltpu.VMEM((NT, 1, 2 * ATT_TQ), F32),
            pltpu.VMEM((V7X_SUBLANES, V7X_LANES), F32),
            pltpu.VMEM((NT, w + ATT_SUM_ROWS, 2 * ATT_TQ), F32),
            pltpu.VMEM((NT, w + ATT_SUM_ROWS, 2 * ATT_TQ), F32),
        ],
    )
    return pl.pallas_call(
        functools.partial(_diffattn_kernel, lam_init=lam_init),
        grid_spec=grid_spec,
        out_shape=jax.ShapeDtypeStruct((B, S, H * w), BF16),
        compiler_params=_params("arbitrary", "arbitrary"),
        name="diff_attention",
    )(slopes, dq, dk, dv, diag_bias, coef, lam_params, subln.reshape(1, w))


def _merge_kernel(x_ref, mod_ref, yr_ref, yd_ref, gr_ref, gd_ref, wr_ref, wd_ref, wo_ref, o_ref):
    def sigmoid(g):
        return 0.5 * (1.0 + jnp.tanh(0.5 * g))

    br = jnp.dot(yr_ref[0], wr_ref[0], preferred_element_type=F32)
    bd = jnp.dot(yd_ref[0], wd_ref[0], preferred_element_type=F32)
    merged = sigmoid(gr_ref[0].astype(F32)) * br + sigmoid(gd_ref[0].astype(F32)) * bd
    out = jnp.dot(merged.astype(BF16), wo_ref[0], preferred_element_type=F32)
    o_ref[0] = x_ref[0] + mod_ref[0][2:3] * out


def _merge(x, mod, y_ret, y_diff, g_ret, g_diff, w_rb, w_db, w_o, layer):
    B, S, D = x.shape
    tok = pl.BlockSpec((1, MERGE_TM, D), lambda b, t: (b, t, 0))
    return pl.pallas_call(
        _merge_kernel,
        grid=(B, S // MERGE_TM),
        in_specs=[tok, pl.BlockSpec((1, 3, D), lambda b, t: (b, 0, 0)), tok, tok, tok, tok,
                  _resident_slice(w_rb.shape, (layer,)), _resident_slice(w_db.shape, (layer,)),
                  _resident_slice(w_o.shape, (layer,))],
        out_specs=tok,
        out_shape=jax.ShapeDtypeStruct(x.shape, F32),
        compiler_params=_params("parallel", "parallel"),
        name="mixer_merge",
    )(x, mod, y_ret, y_diff, g_ret, g_diff, w_rb, w_db, w_o)


def kernel(x, c, w_ada, b_ada, norm_w, w_ffn_up, w_ffn_down, w_in, ret_gn, lambda_q1, lambda_k1, lambda_q2,
           lambda_k2, diff_subln, w_ret_branch, w_diff_branch, w_out, final_norm):
    B, S, D = x.shape
    assert D == D_MODEL and S % max(FFN_TM, MERGE_TM, RET_L, ATT_TILES * ATT_TQ) == 0
    assert ATT_TK == ATT_KQ * ATT_TQ and ATT_TILES % ATT_KQ == 0
    mod_all = _adaln(c, w_ada, b_ada, norm_w).reshape(DEPTH, B, N_SUB, 3, D)
    up, down, w_in_b = w_ffn_up.astype(BF16), w_ffn_down.astype(BF16), w_in.astype(BF16)
    w_rb, w_db, w_o = w_ret_branch.astype(BF16), w_diff_branch.astype(BF16), w_out.astype(BF16)
    for l in range(DEPTH):
        mod = [mod_all[l, :, s] for s in range(N_SUB)]
        x = _ffn(x, mod[0], up, down, l, 0)
        rq, rk, rv, rg, dq, dk, dv, g_ret, g_diff = _inproj(x, mod[1], w_in_b, l)
        y_ret = _retention(rq, rk, rv, rg, ret_gn[l])
        lam_params = jnp.stack([lambda_q1[l], lambda_k1[l], lambda_q2[l], lambda_k2[l]])
        y_diff = _diffattn(dq, dk, dv, lam_params, diff_subln[l], l)
        x = _merge(x, mod[1], y_ret, y_diff, g_ret, g_diff, w_rb, w_db, w_o, l)
        x = _ffn(x, mod[2], up, down, l, 1, final_w=final_norm if l == DEPTH - 1 else None)
    return x
```

```python
import functools
import math

import jax
import jax.numpy as jnp
from jax import lax
from jax.experimental import pallas as pl
from jax.experimental.pallas import tpu as pltpu

DEPTH = 4
D_MODEL = 1024
CHUNK = 64
CHUNK_SHIFT = 6
RET_HEADS = 4
RET_QK_DIM = 128
RET_V_DIM = 256
DIFF_HEADS = 8
DIFF_HEAD_DIM = 64
DIFF_V_DIM = 2 * DIFF_HEAD_DIM
D_FF = 2816
N_SUB = 3
NORM_EPS = 1e-6

F32 = jnp.float32
BF16 = jnp.bfloat16

V7X_VMEM_LIMIT_BYTES = 56 * 1024 * 1024
V7X_LANES = 128
V7X_SUBLANES = 8

FFN_TM = 1024
PROJ_TM = 512
MERGE_TM = 1024
FFN_FC = 256
RET_L = 256
ATT_TQ = 512
ATT_KQ = 1
ATT_TK = ATT_KQ * ATT_TQ
ATT_EXP_ROWS = 16
ATT_TILES = 2
ATT_POS_LANES = 3
ATT_SUM_ROWS = 16
ATT_EXP2_ZERO = 160.0
ATT_BOUND_MARGIN = 1.001
NEG_BIG = -1e30
LOG2E = math.log2(math.e)


def _resident(shape):
    nd = len(shape)
    return pl.BlockSpec(shape, lambda *_: (0,) * nd, pipeline_mode=pl.Buffered(1))


def _resident_slice(shape, lead):
    n_lead, nd = len(lead), len(shape)
    index = tuple(lead) + (0,) * (nd - n_lead)
    return pl.BlockSpec((1,) * n_lead + tuple(shape[n_lead:]), lambda *_: index, pipeline_mode=pl.Buffered(1))


def _params(*sem):
    return pltpu.CompilerParams(dimension_semantics=sem, vmem_limit_bytes=V7X_VMEM_LIMIT_BYTES)


def _adaln_kernel(c_ref, w_ref, b_ref, nw_ref, o_ref):
    j = pl.program_id(1)
    c = c_ref[...]
    cond = c / (1.0 + jnp.exp(-c))
    r = jnp.dot(cond, w_ref[0], preferred_element_type=F32,
                precision=lax.Precision.HIGHEST) + b_ref[0]
    kind = j % 3
    sub = j // 3
    r = jnp.where(kind == 1, (1.0 + r) * nw_ref[0, 0], r)
    r = jnp.where(jnp.logical_and(kind == 2, sub != 1), 0.5 * r, r)
    o_ref[0] = r


def _adaln(c, w_ada, b_ada, norm_w):
    B = c.shape[0]
    n_tiles = N_SUB * 3
    return pl.pallas_call(
        _adaln_kernel,
        grid=(DEPTH, n_tiles),
        in_specs=[
            pl.BlockSpec((B, D_MODEL), lambda l, j: (0, 0)),
            pl.BlockSpec((1, D_MODEL, D_MODEL), lambda l, j: (l, 0, j)),
            pl.BlockSpec((1, 1, D_MODEL), lambda l, j: (l, 0, j)),
            pl.BlockSpec((1, 1, 1, D_MODEL), lambda l, j: (l, j // 3, 0, 0)),
        ],
        out_specs=pl.BlockSpec((1, B, D_MODEL), lambda l, j: (l, 0, j)),
        out_shape=jax.ShapeDtypeStruct((DEPTH, B, n_tiles * D_MODEL), F32),
        compiler_params=_params("arbitrary", "arbitrary"),
        name="adaln_mod",
    )(c, w_ada, b_ada.reshape(DEPTH, 1, n_tiles * D_MODEL), norm_w.reshape(DEPTH, N_SUB, 1, D_MODEL))


def _modulated_norm(x, mod):
    ms = jnp.mean(x * x, axis=-1, keepdims=True)
    return x * lax.rsqrt(ms + NORM_EPS) * mod[1:2] + mod[0:1]


def _ffn_kernel(x_ref, mod_ref, wup_ref, wdn_ref, *rest, n_chunks, final):
    if final:
        fw_ref, o_ref = rest
    else:
        (o_ref,) = rest
    x = x_ref[0]
    mod = mod_ref[0]
    h = _modulated_norm(x, mod).astype(BF16)
    acc = jnp.zeros(x.shape, F32)
    for c in range(n_chunks):
        cols = slice(c * FFN_FC, (c + 1) * FFN_FC)
        a = jnp.dot(h, wup_ref[0, 0, :, cols], preferred_element_type=F32)
        b = jnp.dot(h, wup_ref[0, 0, :, D_FF + c * FFN_FC:D_FF + (c + 1) * FFN_FC], preferred_element_type=F32)
        act = (0.5 * a) * (1.0 + jnp.tanh(0.5 * a)) * b
        acc = acc + jnp.dot(act.astype(BF16), wdn_ref[0, 0, cols, :], preferred_element_type=F32)
    y = x + mod[2:3] * acc
    if final:
        ms = jnp.mean(y * y, axis=-1, keepdims=True)
        y = y * lax.rsqrt(ms + NORM_EPS) * fw_ref[...]
    o_ref[0] = y


def _ffn(x, mod, wup, wdn, layer, which, final_w=None):
    B, S, D = x.shape
    n_chunks = D_FF // FFN_FC
    final = final_w is not None
    in_specs = [
        pl.BlockSpec((1, FFN_TM, D), lambda b, t: (b, t, 0)),
        pl.BlockSpec((1, 3, D), lambda b, t: (b, 0, 0)),
        _resident_slice(wup.shape, (layer, which)),
        _resident_slice(wdn.shape, (layer, which)),
    ]
    args = [x, mod, wup, wdn]
    if final:
        in_specs.append(_resident((1, D)))
        args.append(final_w.reshape(1, D))
    return pl.pallas_call(
        functools.partial(_ffn_kernel, n_chunks=n_chunks, final=final),
        grid=(B, S // FFN_TM),
        in_specs=in_specs,
        out_specs=pl.BlockSpec((1, FFN_TM, D), lambda b, t: (b, t, 0)),
        out_shape=jax.ShapeDtypeStruct(x.shape, F32),
        compiler_params=_params("parallel", "parallel"),
        name="ffn_final" if final else "ffn",
    )(*args)


def _inproj_kernel(x_ref, mod_ref, w_ref, rq_ref, rk_ref, rv_ref, rg_ref, dq_ref, dk_ref, dv_ref,
                   gr_ref, gd_ref):
    x = x_ref[0]
    h = _modulated_norm(x, mod_ref[0]).astype(BF16)

    def proj(c):
        return jnp.dot(h, w_ref[0, :, c * D_MODEL:(c + 1) * D_MODEL], preferred_element_type=F32)

    u = proj(0).astype(BF16)
    for hh in range(RET_HEADS):
        rq_ref[0, hh] = u[:, hh * RET_QK_DIM:(hh + 1) * RET_QK_DIM]
        rk_ref[0, hh] = u[:, (RET_HEADS + hh) * RET_QK_DIM:(RET_HEADS + hh + 1) * RET_QK_DIM]
    u = proj(1).astype(BF16)
    for hh in range(RET_HEADS):
        rv_ref[0, hh] = u[:, hh * RET_V_DIM:(hh + 1) * RET_V_DIM]
    rg_ref[0] = proj(2).astype(BF16)
    u = (proj(3) * (DIFF_HEAD_DIM ** -0.5 * LOG2E)).astype(BF16)
    for hh in range(DIFF_HEADS):
        dq_ref[0, hh] = u[:, hh * DIFF_V_DIM:(hh + 1) * DIFF_V_DIM]
    u = proj(4).astype(BF16)
    for hh in range(DIFF_HEADS):
        dk_ref[0, hh] = u[:, hh * DIFF_V_DIM:(hh + 1) * DIFF_V_DIM]
    u = proj(5).astype(BF16)
    for hh in range(DIFF_HEADS):
        dv_ref[0, hh] = u[:, hh * DIFF_V_DIM:(hh + 1) * DIFF_V_DIM]
    gr_ref[0] = proj(6).astype(BF16)
    gd_ref[0] = proj(7).astype(BF16)


def _inproj(x, mod, w_in, layer):
    B, S, D = x.shape
    TM = PROJ_TM
    head_spec = lambda nh, w: pl.BlockSpec((1, nh, TM, w), lambda b, t: (b, 0, t, 0))
    tok_spec = pl.BlockSpec((1, TM, D), lambda b, t: (b, t, 0))
    sds = jax.ShapeDtypeStruct
    return pl.pallas_call(
        _inproj_kernel,
        grid=(B, S // TM),
        in_specs=[tok_spec, pl.BlockSpec((1, 3, D), lambda b, t: (b, 0, 0)),
                  _resident_slice(w_in.shape, (layer,))],
        out_specs=[
            head_spec(RET_HEADS, RET_QK_DIM), head_spec(RET_HEADS, RET_QK_DIM),
            head_spec(RET_HEADS, RET_V_DIM), tok_spec,
            head_spec(DIFF_HEADS, DIFF_V_DIM), head_spec(DIFF_HEADS, DIFF_V_DIM),
            head_spec(DIFF_HEADS, DIFF_V_DIM), tok_spec, tok_spec,
        ],
        out_shape=[
            sds((B, RET_HEADS, S, RET_QK_DIM), BF16), sds((B, RET_HEADS, S, RET_QK_DIM), BF16),
            sds((B, RET_HEADS, S, RET_V_DIM), BF16), sds((B, S, D), BF16),
            sds((B, DIFF_HEADS, S, DIFF_V_DIM), BF16), sds((B, DIFF_HEADS, S, DIFF_V_DIM), BF16),
            sds((B, DIFF_HEADS, S, DIFF_V_DIM), BF16), sds((B, S, D), BF16), sds((B, S, D), BF16),
        ],
        compiler_params=_params("parallel", "parallel"),
        name="mixer_inproj",
    )(x, mod, w_in)


def _retention_consts():
    L = RET_L
    gamma = 1.0 - 2.0 ** (-5.0 - jnp.arange(RET_HEADS, dtype=F32))
    log_g = jnp.log(gamma)
    r = jnp.arange(L, dtype=F32)
    scale = RET_QK_DIM ** -0.5
    allowed = (jnp.arange(L)[None, :] // CHUNK) <= (jnp.arange(L)[:, None] // CHUNK)
    intra = jnp.exp(log_g[:, None, None] * jnp.abs(r[:, None] - r[None, :]))
    intra = jnp.where(allowed[None], intra, 0.0) * scale
    q_dec = jnp.exp(log_g[:, None] * r[None, :]) * scale
    k_dec = jnp.exp(log_g[:, None] * (L - r)[None, :])
    blk_dec = jnp.exp(log_g * L)
    q_dec = jnp.broadcast_to(q_dec[:, :, None], (RET_HEADS, L, RET_QK_DIM))
    k_dec = jnp.broadcast_to(k_dec[:, :, None], (RET_HEADS, L, RET_QK_DIM))
    blk_dec = jnp.broadcast_to(blk_dec[:, None, None], (RET_HEADS, 1, RET_V_DIM))
    return intra, q_dec, k_dec, blk_dec


def _retention_kernel(q_ref, k_ref, v_ref, g_ref, w_ref, qd_ref, kd_ref, bd_ref, gn_ref, o_ref, state_ref):
    t = pl.program_id(1)

    @pl.when(t == 0)
    def _():
        state_ref[...] = jnp.zeros_like(state_ref)

    dv = v_ref.shape[-1]
    for h in range(q_ref.shape[1]):
        q = q_ref[0, h]
        k = k_ref[0, h]
        v = v_ref[0, h]
        s = lax.dot_general(q, k, (((1,), (1,)), ((), ())), preferred_element_type=F32)
        p = (s * w_ref[h]).astype(BF16)
        y = jnp.dot(p, v, preferred_element_type=F32)
        state = state_ref[h]
        qd = (q.astype(F32) * qd_ref[h]).astype(BF16)
        y = y + jnp.dot(qd, state.astype(BF16), preferred_element_type=F32)
        kd_t = (k.astype(F32) * kd_ref[h]).T.astype(BF16)
        state_ref[h] = state * bd_ref[h] + jnp.dot(kd_t, v, preferred_element_type=F32)
        ms = jnp.mean(y * y, axis=-1, keepdims=True)
        y = y * lax.rsqrt(ms + NORM_EPS) * gn_ref[h]
        g = g_ref[0, :, h * dv:(h + 1) * dv].astype(F32)
        o_ref[0, :, h * dv:(h + 1) * dv] = (y * (0.5 * g) * (1.0 + jnp.tanh(0.5 * g))).astype(BF16)


def _retention(rq, rk, rv, rg, ret_gn_l):
    B, H, S, dk = rq.shape
    dv = rv.shape[-1]
    L = RET_L
    intra, q_dec, k_dec, blk_dec = _retention_consts()
    blk = lambda w: pl.BlockSpec((1, H, L, w), lambda b, t: (b, 0, t, 0))
    tok = pl.BlockSpec((1, L, H * dv), lambda b, t: (b, t, 0))
    return pl.pallas_call(
        _retention_kernel,
        grid=(B, S // L),
        in_specs=[
            blk(dk), blk(dk), blk(dv), tok,
            _resident((H, L, L)), _resident((H, L, dk)), _resident((H, L, dk)), _resident((H, 1, dv)),
            _resident((H, 1, dv)),
        ],
        out_specs=tok,
        out_shape=jax.ShapeDtypeStruct((B, S, H * dv), BF16),
        scratch_shapes=[pltpu.VMEM((H, dk, dv), F32)],
        compiler_params=_params("parallel", "arbitrary"),
        name="retention",
    )(rq, rk, rv, rg, intra, q_dec, k_dec, blk_dec, ret_gn_l.reshape(H, 1, dv))


def _attn_bias_tables(n_heads):
    slopes = 2.0 ** (-8.0 * jnp.arange(1, n_heads + 1, dtype=F32) / n_heads) * LOG2E
    a = jnp.arange(ATT_TK)[None, :, None]
    c = (jnp.arange(ATT_KQ) * ATT_TQ)[:, None, None] + jnp.arange(ATT_TQ)[None, None, :]
    allowed = (a // CHUNK) <= (c // CHUNK)
    diag = slopes[:, None, None, None] * (-2.0 * jnp.maximum(a - c, 0).astype(F32))[None]
    diag = jnp.where(allowed[None], diag, NEG_BIG)
    diag = jnp.concatenate([diag, diag], axis=-1)
    p0 = slopes.astype(BF16)
    p1 = (slopes - p0.astype(F32)).astype(BF16)
    p2 = (slopes - p0.astype(F32) - p1.astype(F32)).astype(BF16)
    parts = jnp.stack([p0, p1, p2], axis=1).astype(F32)
    coef = jnp.concatenate([parts * CHUNK, parts, jnp.zeros((n_heads, V7X_LANES - 2 * ATT_POS_LANES), F32)], axis=1)
    coef = jnp.broadcast_to(coef.astype(BF16)[:, :, None], (n_heads, V7X_LANES, V7X_LANES))
    return slopes, diag, coef


def _diffattn_kernel(slope_ref, q_ref, k_ref, v_ref, db_ref, coef_ref, lam_ref, sub_ref, o_ref,
                     vt_ref, kaug_ref, qbd_ref, s_ref, p_ref, bmax_ref, alpha_ref, m_ref, knorm_ref, acc_ref,
                     fin_ref, *, lam_init):
    h = pl.program_id(1)
    TK = ATT_TK
    n_kv = vt_ref.shape[0]

    @pl.when(jnp.logical_and(pl.program_id(0) == 0, h == 0))
    def _():
        for c in range(n_kv):
            ones_row = lax.broadcasted_iota(jnp.int32, (ATT_SUM_ROWS, TK), 0) == 0
            vt_ref[c, DIFF_V_DIM:DIFF_V_DIM + ATT_SUM_ROWS] = ones_row.astype(F32).astype(BF16)
            pos = c * TK + lax.broadcasted_iota(jnp.int32, (TK, V7X_LANES), 0)
            lane = lax.broadcasted_iota(jnp.int32, (TK, V7X_LANES), 1)
            feat = jnp.where(lane < ATT_POS_LANES, jnp.right_shift(pos, CHUNK_SHIFT),
                             jnp.where(lane < 2 * ATT_POS_LANES, jnp.bitwise_and(pos, CHUNK - 1), 0))
            kaug_ref[c * TK:(c + 1) * TK, 2 * DIFF_HEAD_DIM:] = feat.astype(F32).astype(BF16)

    k1sq = jnp.zeros((1, TK), F32)
    k2sq = jnp.zeros((1, TK), F32)
    for c in range(n_kv):
        vt_ref[c, 0:DIFF_V_DIM] = v_ref[0, 0, c * TK:(c + 1) * TK, :].astype(F32).T.astype(BF16)
        k = k_ref[0, 0, c * TK:(c + 1) * TK, :]
        kaug_ref[c * TK:(c + 1) * TK, 0:2 * DIFF_HEAD_DIM] = k
        ksq_t = jnp.square(k.astype(F32).T)
        k1sq = jnp.maximum(k1sq, jnp.sum(ksq_t[0:DIFF_HEAD_DIM], axis=0, keepdims=True))
        k2sq = jnp.maximum(k2sq, jnp.sum(ksq_t[DIFF_HEAD_DIM:], axis=0, keepdims=True))
    knorm_ref[0:1] = jnp.broadcast_to(jnp.max(k1sq, axis=1, keepdims=True), (1, V7X_LANES))
    knorm_ref[1:2] = jnp.broadcast_to(jnp.max(k2sq, axis=1, keepdims=True), (1, V7X_LANES))

    def group(g, carry):
        _diffattn_group(g, h, slope_ref, q_ref, db_ref, coef_ref, lam_ref, sub_ref, o_ref, vt_ref, kaug_ref,
                        qbd_ref, s_ref, p_ref, bmax_ref, alpha_ref, m_ref, knorm_ref, acc_ref, fin_ref,
                        lam_init=lam_init)
        return carry

    fin_ref[...] = jnp.zeros(fin_ref.shape, F32)
    fin_ref[:, DIFF_V_DIM:DIFF_V_DIM + 1] = jnp.ones((ATT_TILES, 1, fin_ref.shape[2]), F32)
    n_groups = q_ref.shape[2] // (ATT_TILES * ATT_TQ)
    lax.fori_loop(0, n_groups, group, 0)
    _diffattn_finish(n_groups - 1, lam_ref, sub_ref, o_ref, fin_ref, lam_init=lam_init)


def _max_sq_norms(x):
    xsq = jnp.square(x.astype(F32))
    first = lax.broadcasted_iota(jnp.int32, xsq.shape, 1) < DIFF_HEAD_DIM
    n1 = jnp.sum(jnp.where(first, xsq, 0.0), axis=1, keepdims=True)
    n2 = jnp.sum(jnp.where(first, 0.0, xsq), axis=1, keepdims=True)
    return jnp.max(n1, axis=0, keepdims=True), jnp.max(n2, axis=0, keepdims=True)


def _diffattn_group(g, h, slope_ref, q_ref, db_ref, coef_ref, lam_ref, sub_ref, o_ref, vt_ref, kaug_ref,
                    qbd_ref, s_ref, p_ref, bmax_ref, alpha_ref, m_ref, knorm_ref, acc_ref, fin_ref, *, lam_init):
    TQ, TK = ATT_TQ, ATT_TK
    slope = slope_ref[h]
    tiles = range(ATT_TILES)
    tile_idx = [g * ATT_TILES + x for x in tiles]
    diag_blk = [g * (ATT_TILES // ATT_KQ) + x // ATT_KQ for x in tiles]

    def q_rows(x):
        return pl.ds(pl.multiple_of(tile_idx[x] * TQ, TQ), TQ)

    for x in tiles:
        q = q_ref[0, 0, q_rows(x), :]
        q_t = q.astype(F32).T
        row = lax.broadcasted_iota(jnp.int32, q_t.shape, 0)
        qbd_ref[x, 0:2 * DIFF_HEAD_DIM, 0:TQ] = jnp.where(row < DIFF_HEAD_DIM, q_t, 0.0).astype(BF16)
        qbd_ref[x, 0:2 * DIFF_HEAD_DIM, TQ:2 * TQ] = jnp.where(row >= DIFF_HEAD_DIM, q_t, 0.0).astype(BF16)
        qbd_ref[x, 2 * DIFF_HEAD_DIM:, :] = jnp.concatenate([coef_ref[0]] * (2 * TQ // V7X_LANES), axis=1)
        m_ref[x] = jnp.full(m_ref.shape[1:], NEG_BIG, F32)
        acc_ref[x] = jnp.zeros(acc_ref.shape[1:], F32)

    def block_of(x, t):
        return jnp.maximum(diag_blk[x] - t, 0)

    def stage_scores(x, t, slot, diagonal):
        k = kaug_ref[pl.ds(pl.multiple_of(block_of(x, t) * TK, TK), TK), :]
        s = jnp.dot(k, qbd_ref[x], preferred_element_type=F32)
        bmax = None
        for r in range(0, TK, ATT_EXP_ROWS):
            rows = slice(r, r + ATT_EXP_ROWS)
            u = s[rows] + db_ref[0, x % ATT_KQ, rows] if diagonal else s[rows]
            s_ref[x, slot, rows] = u
            cmax = jnp.max(u.reshape(ATT_EXP_ROWS // V7X_SUBLANES, V7X_SUBLANES, 2 * TQ), axis=0)
            bmax = cmax if bmax is None else jnp.maximum(bmax, cmax)
        bmax_ref[x, slot] = jnp.max(bmax, axis=0, keepdims=True)

    def stage_softmax(x, t, slot, diagonal):
        if diagonal:
            shift = 0.0
        else:
            shift = jnp.where(t < n_steps[x], 0.0, NEG_BIG)
        m = m_ref[x]
        m_new = jnp.maximum(m, bmax_ref[x, slot] + shift)
        alpha = jnp.exp2(m - m_new)
        m_ref[x] = m_new
        alpha_ref[x, slot] = alpha
        ref = m_new - shift
        for r in range(0, TK, ATT_EXP_ROWS):
            rows = slice(r, r + ATT_EXP_ROWS)
            p_ref[x, slot, rows] = jnp.exp2(s_ref[x, slot, rows] - ref).astype(BF16)

    def stage_pv(x, t, slot, last=False):
        pv = jnp.dot(vt_ref[block_of(x, t)], p_ref[x, slot], preferred_element_type=F32)
        out_ref = fin_ref if last else acc_ref
        out_ref[x] = alpha_ref[x, slot] * acc_ref[x] + pv

    _diffattn_finish(jnp.maximum(g - 1, 0), lam_ref, sub_ref, o_ref, fin_ref, lam_init=lam_init)
    for x in tiles:
        stage_scores(x, 0, 0, True)
    for x in tiles:
        stage_scores(x, 1, 1, False)
        stage_softmax(x, 0, 0, True)

    n_steps = []
    for x in tiles:
        q1sq, q2sq = _max_sq_norms(q_ref[0, 0, q_rows(x), :])
        bound = jnp.sqrt(jnp.maximum(q1sq * knorm_ref[0:1, 0:1], q2sq * knorm_ref[1:2, 0:1])) * ATT_BOUND_MARGIN
        m_min = jnp.min(m_ref[x], axis=1, keepdims=True)
        blk = diag_blk[x].astype(F32)
        reach = blk + (ATT_EXP2_ZERO + bound - m_min) / (slope * TK)
        n = jnp.minimum(jnp.floor(reach) + 2.0, blk + 1.0)
        n_steps.append(jnp.max(n).astype(jnp.int32))

    def pipeline_tick(t, slot):
        for x in tiles:
            stage_scores(x, t + 2, slot, False)
            stage_softmax(x, t + 1, 1 - slot, False)
            stage_pv(x, t, slot)

    def pipeline_drain(t, slot):
        for x in tiles:
            stage_softmax(x, t + 1, 1 - slot, False)
            stage_pv(x, t, slot)
        for x in tiles:
            stage_pv(x, t + 1, 1 - slot, last=True)

    def by_parity(fn, t):
        for slot in (0, 1):
            @pl.when(jnp.bitwise_and(t, 1) == slot)
            def _():
                fn(t, slot)

    def body(t, carry):
        by_parity(pipeline_tick, t)
        return carry

    n_ticks = jnp.maximum(functools.reduce(jnp.maximum, n_steps), 2)
    lax.fori_loop(0, n_ticks - 2, body, 0)
    by_parity(pipeline_drain, n_ticks - 2)


def _diffattn_finish(g, lam_ref, sub_ref, o_ref, fin_ref, *, lam_init):
    TQ = ATT_TQ
    lp = lam_ref[...]
    lam = (jnp.exp(jnp.sum(lp[0:1] * lp[1:2], axis=-1, keepdims=True))
           - jnp.exp(jnp.sum(lp[2:3] * lp[3:4], axis=-1, keepdims=True)) + lam_init)
    for x in range(ATT_TILES):
        o = fin_ref[x, 0:DIFF_V_DIM] * (1.0 / fin_ref[x, DIFF_V_DIM:DIFF_V_DIM + 1])
        y = (o[:, :TQ] - lam * o[:, TQ:]).T
        ms = jnp.mean(y * y, axis=-1, keepdims=True)
        y = y * lax.rsqrt(ms + NORM_EPS) * sub_ref[...] * (1.0 - lam_init)
        o_ref[0, pl.ds(pl.multiple_of((g * ATT_TILES + x) * TQ, TQ), TQ)] = y.astype(BF16)


def _diffattn(dq, dk, dv, lam_params, subln, layer_idx):
    B, H, S, w = dq.shape
    lam_init = 0.8 - 0.6 * math.exp(-0.3 * layer_idx)
    slopes, diag_bias, coef = _attn_bias_tables(H)
    NT = ATT_TILES
    seq = pl.BlockSpec((1, 1, S, w), lambda b, h, sl: (b, h, 0, 0))
    grid_spec = pltpu.PrefetchScalarGridSpec(
        num_scalar_prefetch=1,
        grid=(B, H),
        in_specs=[
            seq, seq, seq,
            pl.BlockSpec((1, ATT_KQ, ATT_TK, 2 * ATT_TQ), lambda b, h, sl: (h, 0, 0, 0)),
            pl.BlockSpec((1, V7X_LANES, V7X_LANES), lambda b, h, sl: (h, 0, 0)),
            pl.BlockSpec((4, DIFF_HEAD_DIM), lambda b, h, sl: (0, 0)),
            pl.BlockSpec((1, w), lambda b, h, sl: (0, 0)),
        ],
        out_specs=pl.BlockSpec((1, S, w), lambda b, h, sl: (b, 0, h)),
        scratch_shapes=[
            pltpu.VMEM((S // ATT_TK, w + ATT_SUM_ROWS, ATT_TK), BF16),
            pltpu.VMEM((S, 2 * w), BF16),
            pltpu.VMEM((NT, 2 * w, 2 * ATT_TQ), BF16),
            pltpu.VMEM((NT, 2, ATT_TK, 2 * ATT_TQ), F32),
            pltpu.VMEM((NT, 2, ATT_TK, 2 * ATT_TQ), BF16),
            pltpu.VMEM((NT, 2, 1, 2 * ATT_TQ), F32),
            pltpu.VMEM((NT, 2, 1, 2 * ATT_TQ), F32),
            pltpu.VMEM((NT, 1, 2 * ATT_TQ), F32),
            pltpu.VMEM((V7X_SUBLANES, V7X_LANES), F32),
            pltpu.VMEM((NT, w + ATT_SUM_ROWS, 2 * ATT_TQ), F32),
            pltpu.VMEM((NT, w + ATT_SUM_ROWS, 2 * ATT_TQ), F32),
        ],
    )
    return pl.pallas_call(
        functools.partial(_diffattn_kernel, lam_init=lam_init),
        grid_spec=grid_spec,
        out_shape=jax.ShapeDtypeStruct((B, S, H * w), BF16),
        compiler_params=_params("arbitrary", "arbitrary"),
        name="diff_attention",
    )(slopes, dq, dk, dv, diag_bias, coef, lam_params, subln.reshape(1, w))


def _merge_kernel(x_ref, mod_ref, yr_ref, yd_ref, gr_ref, gd_ref, wr_ref, wd_ref, wo_ref, o_ref):
    def sigmoid(g):
        return 0.5 * (1.0 + jnp.tanh(0.5 * g))

    br = jnp.dot(yr_ref[0], wr_ref[0], preferred_element_type=F32)
    bd = jnp.dot(yd_ref[0], wd_ref[0], preferred_element_type=F32)
    merged = sigmoid(gr_ref[0].astype(F32)) * br + sigmoid(gd_ref[0].astype(F32)) * bd
    out = jnp.dot(merged.astype(BF16), wo_ref[0], preferred_element_type=F32)
    o_ref[0] = x_ref[0] + mod_ref[0][2:3] * out


def _merge(x, mod, y_ret, y_diff, g_ret, g_diff, w_rb, w_db, w_o, layer):
    B, S, D = x.shape
    tok = pl.BlockSpec((1, MERGE_TM, D), lambda b, t: (b, t, 0))
    return pl.pallas_call(
        _merge_kernel,
        grid=(B, S // MERGE_TM),
        in_specs=[tok, pl.BlockSpec((1, 3, D), lambda b, t: (b, 0, 0)), tok, tok, tok, tok,
                  _resident_slice(w_rb.shape, (layer,)), _resident_slice(w_db.shape, (layer,)),
                  _resident_slice(w_o.shape, (layer,))],
        out_specs=tok,
        out_shape=jax.ShapeDtypeStruct(x.shape, F32),
        compiler_params=_params("parallel", "parallel"),
        name="mixer_merge",
    )(x, mod, y_ret, y_diff, g_ret, g_diff, w_rb, w_db, w_o)


def kernel(x, c, w_ada, b_ada, norm_w, w_ffn_up, w_ffn_down, w_in, ret_gn, lambda_q1, lambda_k1, lambda_q2,
           lambda_k2, diff_subln, w_ret_branch, w_diff_branch, w_out, final_norm):
    B, S, D = x.shape
    assert D == D_MODEL and S % max(FFN_TM, MERGE_TM, RET_L, ATT_TILES * ATT_TQ) == 0
    assert ATT_TK == ATT_KQ * ATT_TQ and ATT_TILES % ATT_KQ == 0
    mod_all = _adaln(c, w_ada, b_ada, norm_w).reshape(DEPTH, B, N_SUB, 3, D)
    up, down, w_in_b = w_ffn_up.astype(BF16), w_ffn_down.astype(BF16), w_in.astype(BF16)
    w_rb, w_db, w_o = w_ret_branch.astype(BF16), w_diff_branch.astype(BF16), w_out.astype(BF16)
    for l in range(DEPTH):
        mod = [mod_all[l, :, s] for s in range(N_SUB)]
        x = _ffn(x, mod[0], up, down, l, 0)
        rq, rk, rv, rg, dq, dk, dv, g_ret, g_diff = _inproj(x, mod[1], w_in_b, l)
        y_ret = _retention(rq, rk, rv, rg, ret_gn[l])
        lam_params = jnp.stack([lambda_q1[l], lambda_k1[l], lambda_q2[l], lambda_k2[l]])
        y_diff = _diffattn(dq, dk, dv, lam_params, diff_subln[l], l)
        x = _merge(x, mod[1], y_ret, y_diff, g_ret, g_diff, w_rb, w_db, w_o, l)
        x = _ffn(x, mod[2], up, down, l, 1, final_w=final_norm if l == DEPTH - 1 else None)
    return x
```

```python
import functools
import math

import jax
import jax.numpy as jnp
from jax import lax
from jax.experimental import pallas as pl
from jax.experimental.pallas import tpu as pltpu

DEPTH = 4
D_MODEL = 1024
CHUNK = 64
CHUNK_SHIFT = 6
RET_HEADS = 4
RET_QK_DIM = 128
RET_V_DIM = 256
DIFF_HEADS = 8
DIFF_HEAD_DIM = 64
DIFF_V_DIM = 2 * DIFF_HEAD_DIM
D_FF = 2816
N_SUB = 3
NORM_EPS = 1e-6

F32 = jnp.float32
BF16 = jnp.bfloat16

V7X_VMEM_LIMIT_BYTES = 56 * 1024 * 1024
V7X_LANES = 128
V7X_SUBLANES = 8

FFN_TM = 1024
PROJ_TM = 512
MERGE_TM = 1024
FFN_FC = 256
RET_L = 256
ATT_TQ = 256
ATT_KQ = 2
ATT_TK = ATT_KQ * ATT_TQ
ATT_EXP_ROWS = 16
ATT_TILES = 4
ATT_POS_LANES = 3
ATT_SUM_ROWS = 16
ATT_EXP2_ZERO = 160.0
ATT_BOUND_MARGIN = 1.001
NEG_BIG = -1e30
LOG2E = math.log2(math.e)


def _resident(shape):
    nd = len(shape)
    return pl.BlockSpec(shape, lambda *_: (0,) * nd, pipeline_mode=pl.Buffered(1))


def _resident_slice(shape, lead):
    n_lead, nd = len(lead), len(shape)
    index = tuple(lead) + (0,) * (nd - n_lead)
    return pl.BlockSpec((1,) * n_lead + tuple(shape[n_lead:]), lambda *_: index, pipeline_mode=pl.Buffered(1))


def _params(*sem):
    return pltpu.CompilerParams(dimension_semantics=sem, vmem_limit_bytes=V7X_VMEM_LIMIT_BYTES)


def _adaln_kernel(c_ref, w_ref, b_ref, nw_ref, o_ref):
    j = pl.program_id(1)
    c = c_ref[...]
    cond = c / (1.0 + jnp.exp(-c))
    r = jnp.dot(cond, w_ref[0], preferred_element_type=F32,
                precision=lax.Precision.HIGHEST) + b_ref[0]
    kind = j % 3
    sub = j // 3
    r = jnp.where(kind == 1, (1.0 + r) * nw_ref[0, 0], r)
    r = jnp.where(jnp.logical_and(kind == 2, sub != 1), 0.5 * r, r)
    o_ref[0] = r


def _adaln(c, w_ada, b_ada, norm_w):
    B = c.shape[0]
    n_tiles = N_SUB * 3
    return pl.pallas_call(
        _adaln_kernel,
        grid=(DEPTH, n_tiles),
        in_specs=[
            pl.BlockSpec((B, D_MODEL), lambda l, j: (0, 0)),
            pl.BlockSpec((1, D_MODEL, D_MODEL), lambda l, j: (l, 0, j)),
            pl.BlockSpec((1, 1, D_MODEL), lambda l, j: (l, 0, j)),
            pl.BlockSpec((1, 1, 1, D_MODEL), lambda l, j: (l, j // 3, 0, 0)),
        ],
        out_specs=pl.BlockSpec((1, B, D_MODEL), lambda l, j: (l, 0, j)),
        out_shape=jax.ShapeDtypeStruct((DEPTH, B, n_tiles * D_MODEL), F32),
        compiler_params=_params("arbitrary", "arbitrary"),
        name="adaln_mod",
    )(c, w_ada, b_ada.reshape(DEPTH, 1, n_tiles * D_MODEL), norm_w.reshape(DEPTH, N_SUB, 1, D_MODEL))


def _modulated_norm(x, mod):
    ms = jnp.mean(x * x, axis=-1, keepdims=True)
    return x * lax.rsqrt(ms + NORM_EPS) * mod[1:2] + mod[0:1]


def _ffn_kernel(x_ref, mod_ref, wup_ref, wdn_ref, *rest, n_chunks, final):
    if final:
        fw_ref, o_ref = rest
    else:
        (o_ref,) = rest
    x = x_ref[0]
    mod = mod_ref[0]
    h = _modulated_norm(x, mod).astype(BF16)
    acc = jnp.zeros(x.shape, F32)
    for c in range(n_chunks):
        cols = slice(c * FFN_FC, (c + 1) * FFN_FC)
        a = jnp.dot(h, wup_ref[0, 0, :, cols], preferred_element_type=F32)
        b = jnp.dot(h, wup_ref[0, 0, :, D_FF + c * FFN_FC:D_FF + (c + 1) * FFN_FC], preferred_element_type=F32)
        act = (0.5 * a) * (1.0 + jnp.tanh(0.5 * a)) * b
        acc = acc + jnp.dot(act.astype(BF16), wdn_ref[0, 0, cols, :], preferred_element_type=F32)
    y = x + mod[2:3] * acc
    if final:
        ms = jnp.mean(y * y, axis=-1, keepdims=True)
        y = y * lax.rsqrt(ms + NORM_EPS) * fw_ref[...]
    o_ref[0] = y


def _ffn(x, mod, wup, wdn, layer, which, final_w=None):
    B, S, D = x.shape
    n_chunks = D_FF // FFN_FC
    final = final_w is not None
    in_specs = [
        pl.BlockSpec((1, FFN_TM, D), lambda b, t: (b, t, 0)),
        pl.BlockSpec((1, 3, D), lambda b, t: (b, 0, 0)),
        _resident_slice(wup.shape, (layer, which)),
        _resident_slice(wdn.shape, (layer, which)),
    ]
    args = [x, mod, wup, wdn]
    if final:
        in_specs.append(_resident((1, D)))
        args.append(final_w.reshape(1, D))
    return pl.pallas_call(
        functools.partial(_ffn_kernel, n_chunks=n_chunks, final=final),
        grid=(B, S // FFN_TM),
        in_specs=in_specs,
        out_specs=pl.BlockSpec((1, FFN_TM, D), lambda b, t: (b, t, 0)),
        out_shape=jax.ShapeDtypeStruct(x.shape, F32),
        compiler_params=_params("parallel", "parallel"),
        name="ffn_final" if final else "ffn",
    )(*args)


def _inproj_kernel(x_ref, mod_ref, w_ref, rq_ref, rk_ref, rv_ref, rg_ref, dq_ref, dk_ref, dv_ref,
                   gr_ref, gd_ref):
    x = x_ref[0]
    h = _modulated_norm(x, mod_ref[0]).astype(BF16)

    def proj(c):
        return jnp.dot(h, w_ref[0, :, c * D_MODEL:(c + 1) * D_MODEL], preferred_element_type=F32)

    u = proj(0).astype(BF16)
    for hh in range(RET_HEADS):
        rq_ref[0, hh] = u[:, hh * RET_QK_DIM:(hh + 1) * RET_QK_DIM]
        rk_ref[0, hh] = u[:, (RET_HEADS + hh) * RET_QK_DIM:(RET_HEADS + hh + 1) * RET_QK_DIM]
    u = proj(1).astype(BF16)
    for hh in range(RET_HEADS):
        rv_ref[0, hh] = u[:, hh * RET_V_DIM:(hh + 1) * RET_V_DIM]
    rg_ref[0] = proj(2).astype(BF16)
    u = (proj(3) * (DIFF_HEAD_DIM ** -0.5 * LOG2E)).astype(BF16)
    for hh in range(DIFF_HEADS):
        dq_ref[0, hh] = u[:, hh * DIFF_V_DIM:(hh + 1) * DIFF_V_DIM]
    u = proj(4).astype(BF16)
    for hh in range(DIFF_HEADS):
        dk_ref[0, hh] = u[:, hh * DIFF_V_DIM:(hh + 1) * DIFF_V_DIM]
    u = proj(5).astype(BF16)
    for hh in range(DIFF_HEADS):
        dv_ref[0, hh] = u[:, hh * DIFF_V_DIM:(hh + 1) * DIFF_V_DIM]
    gr_ref[0] = proj(6).astype(BF16)
    gd_ref[0] = proj(7).astype(BF16)


def _inproj(x, mod, w_in, layer):
    B, S, D = x.shape
    TM = PROJ_TM
    head_spec = lambda nh, w: pl.BlockSpec((1, nh, TM, w), lambda b, t: (b, 0, t, 0))
    tok_spec = pl.BlockSpec((1, TM, D), lambda b, t: (b, t, 0))
    sds = jax.ShapeDtypeStruct
    return pl.pallas_call(
        _inproj_kernel,
        grid=(B, S // TM),
        in_specs=[tok_spec, pl.BlockSpec((1, 3, D), lambda b, t: (b, 0, 0)),
                  _resident_slice(w_in.shape, (layer,))],
        out_specs=[
            head_spec(RET_HEADS, RET_QK_DIM), head_spec(RET_HEADS, RET_QK_DIM),
            head_spec(RET_HEADS, RET_V_DIM), tok_spec,
            head_spec(DIFF_HEADS, DIFF_V_DIM), head_spec(DIFF_HEADS, DIFF_V_DIM),
            head_spec(DIFF_HEADS, DIFF_V_DIM), tok_spec, tok_spec,
        ],
        out_shape=[
            sds((B, RET_HEADS, S, RET_QK_DIM), BF16), sds((B, RET_HEADS, S, RET_QK_DIM), BF16),
            sds((B, RET_HEADS, S, RET_V_DIM), BF16), sds((B, S, D), BF16),
            sds((B, DIFF_HEADS, S, DIFF_V_DIM), BF16), sds((B, DIFF_HEADS, S, DIFF_V_DIM), BF16),
            sds((B, DIFF_HEADS, S, DIFF_V_DIM), BF16), sds((B, S, D), BF16), sds((B, S, D), BF16),
        ],
        compiler_params=_params("parallel", "parallel"),
        name="mixer_inproj",
    )(x, mod, w_in)


def _retention_consts():
    L = RET_L
    gamma = 1.0 - 2.0 ** (-5.0 - jnp.arange(RET_HEADS, dtype=F32))
    log_g = jnp.log(gamma)
    r = jnp.arange(L, dtype=F32)
    scale = RET_QK_DIM ** -0.5
    allowed = (jnp.arange(L)[None, :] // CHUNK) <= (jnp.arange(L)[:, None] // CHUNK)
    intra = jnp.exp(log_g[:, None, None] * jnp.abs(r[:, None] - r[None, :]))
    intra = jnp.where(allowed[None], intra, 0.0) * scale
    q_dec = jnp.exp(log_g[:, None] * r[None, :]) * scale
    k_dec = jnp.exp(log_g[:, None] * (L - r)[None, :])
    blk_dec = jnp.exp(log_g * L)
    q_dec = jnp.broadcast_to(q_dec[:, :, None], (RET_HEADS, L, RET_QK_DIM))
    k_dec = jnp.broadcast_to(k_dec[:, :, None], (RET_HEADS, L, RET_QK_DIM))
    blk_dec = jnp.broadcast_to(blk_dec[:, None, None], (RET_HEADS, 1, RET_V_DIM))
    return intra, q_dec, k_dec, blk_dec


def _retention_kernel(q_ref, k_ref, v_ref, g_ref, w_ref, qd_ref, kd_ref, bd_ref, gn_ref, o_ref, state_ref):
    t = pl.program_id(1)

    @pl.when(t == 0)
    def _():
        state_ref[...] = jnp.zeros_like(state_ref)

    dv = v_ref.shape[-1]
    for h in range(q_ref.shape[1]):
        q = q_ref[0, h]
        k = k_ref[0, h]
        v = v_ref[0, h]
        s = lax.dot_general(q, k, (((1,), (1,)), ((), ())), preferred_element_type=F32)
        p = (s * w_ref[h]).astype(BF16)
        y = jnp.dot(p, v, preferred_element_type=F32)
        state = state_ref[h]
        qd = (q.astype(F32) * qd_ref[h]).astype(BF16)
        y = y + jnp.dot(qd, state.astype(BF16), preferred_element_type=F32)
        kd_t = (k.astype(F32) * kd_ref[h]).T.astype(BF16)
        state_ref[h] = state * bd_ref[h] + jnp.dot(kd_t, v, preferred_element_type=F32)
        ms = jnp.mean(y * y, axis=-1, keepdims=True)
        y = y * lax.rsqrt(ms + NORM_EPS) * gn_ref[h]
        g = g_ref[0, :, h * dv:(h + 1) * dv].astype(F32)
        o_ref[0, :, h * dv:(h + 1) * dv] = (y * (0.5 * g) * (1.0 + jnp.tanh(0.5 * g))).astype(BF16)


def _retention(rq, rk, rv, rg, ret_gn_l):
    B, H, S, dk = rq.shape
    dv = rv.shape[-1]
    L = RET_L
    intra, q_dec, k_dec, blk_dec = _retention_consts()
    blk = lambda w: pl.BlockSpec((1, H, L, w), lambda b, t: (b, 0, t, 0))
    tok = pl.BlockSpec((1, L, H * dv), lambda b, t: (b, t, 0))
    return pl.pallas_call(
        _retention_kernel,
        grid=(B, S // L),
        in_specs=[
            blk(dk), blk(dk), blk(dv), tok,
            _resident((H, L, L)), _resident((H, L, dk)), _resident((H, L, dk)), _resident((H, 1, dv)),
            _resident((H, 1, dv)),
        ],
        out_specs=tok,
        out_shape=jax.ShapeDtypeStruct((B, S, H * dv), BF16),
        scratch_shapes=[pltpu.VMEM((H, dk, dv), F32)],
        compiler_params=_params("parallel", "arbitrary"),
        name="retention",
    )(rq, rk, rv, rg, intra, q_dec, k_dec, blk_dec, ret_gn_l.reshape(H, 1, dv))


def _attn_bias_tables(n_heads):
    slopes = 2.0 ** (-8.0 * jnp.arange(1, n_heads + 1, dtype=F32) / n_heads) * LOG2E
    a = jnp.arange(ATT_TK)[None, :, None]
    c = (jnp.arange(ATT_KQ) * ATT_TQ)[:, None, None] + jnp.arange(ATT_TQ)[None, None, :]
    allowed = (a // CHUNK) <= (c // CHUNK)
    diag = slopes[:, None, None, None] * (-2.0 * jnp.maximum(a - c, 0).astype(F32))[None]
    diag = jnp.where(allowed[None], diag, NEG_BIG)
    diag = jnp.concatenate([diag, diag], axis=-1)
    p0 = slopes.astype(BF16)
    p1 = (slopes - p0.astype(F32)).astype(BF16)
    p2 = (slopes - p0.astype(F32) - p1.astype(F32)).astype(BF16)
    parts = jnp.stack([p0, p1, p2], axis=1).astype(F32)
    coef = jnp.concatenate([parts * CHUNK, parts, jnp.zeros((n_heads, V7X_LANES - 2 * ATT_POS_LANES), F32)], axis=1)
    coef = jnp.broadcast_to(coef.astype(BF16)[:, :, None], (n_heads, V7X_LANES, V7X_LANES))
    return slopes, diag, coef


def _diffattn_kernel(slope_ref, q_ref, k_ref, v_ref, db_ref, coef_ref, lam_ref, sub_ref, o_ref,
                     vt_ref, kaug_ref, qbd_ref, s_ref, p_ref, bmax_ref, alpha_ref, m_ref, knorm_ref, acc_ref,
                     fin_ref, *, lam_init):
    h = pl.program_id(1)
    TK = ATT_TK
    n_kv = vt_ref.shape[0]

    @pl.when(jnp.logical_and(pl.program_id(0) == 0, h == 0))
    def _():
        for c in range(n_kv):
            ones_row = lax.broadcasted_iota(jnp.int32, (ATT_SUM_ROWS, TK), 0) == 0
            vt_ref[c, DIFF_V_DIM:DIFF_V_DIM + ATT_SUM_ROWS] = ones_row.astype(F32).astype(BF16)
            pos = c * TK + lax.broadcasted_iota(jnp.int32, (TK, V7X_LANES), 0)
            lane = lax.broadcasted_iota(jnp.int32, (TK, V7X_LANES), 1)
            feat = jnp.where(lane < ATT_POS_LANES, jnp.right_shift(pos, CHUNK_SHIFT),
                             jnp.where(lane < 2 * ATT_POS_LANES, jnp.bitwise_and(pos, CHUNK - 1), 0))
            kaug_ref[c * TK:(c + 1) * TK, 2 * DIFF_HEAD_DIM:] = feat.astype(F32).astype(BF16)

    k1sq = jnp.zeros((1, TK), F32)
    k2sq = jnp.zeros((1, TK), F32)
    for c in range(n_kv):
        vt_ref[c, 0:DIFF_V_DIM] = v_ref[0, 0, c * TK:(c + 1) * TK, :].astype(F32).T.astype(BF16)
        k = k_ref[0, 0, c * TK:(c + 1) * TK, :]
        kaug_ref[c * TK:(c + 1) * TK, 0:2 * DIFF_HEAD_DIM] = k
        ksq_t = jnp.square(k.astype(F32).T)
        k1sq = jnp.maximum(k1sq, jnp.sum(ksq_t[0:DIFF_HEAD_DIM], axis=0, keepdims=True))
        k2sq = jnp.maximum(k2sq, jnp.sum(ksq_t[DIFF_HEAD_DIM:], axis=0, keepdims=True))
    knorm_ref[0:1] = jnp.broadcast_to(jnp.max(k1sq, axis=1, keepdims=True), (1, V7X_LANES))
    knorm_ref[1:2] = jnp.broadcast_to(jnp.max(k2sq, axis=1, keepdims=True), (1, V7X_LANES))

    def group(g, carry):
        _diffattn_group(g, h, slope_ref, q_ref, db_ref, coef_ref, lam_ref, sub_ref, o_ref, vt_ref, kaug_ref,
                        qbd_ref, s_ref, p_ref, bmax_ref, alpha_ref, m_ref, knorm_ref, acc_ref, fin_ref,
                        lam_init=lam_init)
        return carry

    fin_ref[...] = jnp.zeros(fin_ref.shape, F32)
    fin_ref[:, DIFF_V_DIM:DIFF_V_DIM + 1] = jnp.ones((ATT_TILES, 1, fin_ref.shape[2]), F32)
    n_groups = q_ref.shape[2] // (ATT_TILES * ATT_TQ)
    lax.fori_loop(0, n_groups, group, 0)
    _diffattn_finish(n_groups - 1, lam_ref, sub_ref, o_ref, fin_ref, lam_init=lam_init)


def _max_sq_norms(x):
    xsq = jnp.square(x.astype(F32))
    first = lax.broadcasted_iota(jnp.int32, xsq.shape, 1) < DIFF_HEAD_DIM
    n1 = jnp.sum(jnp.where(first, xsq, 0.0), axis=1, keepdims=True)
    n2 = jnp.sum(jnp.where(first, 0.0, xsq), axis=1, keepdims=True)
    return jnp.max(n1, axis=0, keepdims=True), jnp.max(n2, axis=0, keepdims=True)


def _diffattn_group(g, h, slope_ref, q_ref, db_ref, coef_ref, lam_ref, sub_ref, o_ref, vt_ref, kaug_ref,
                    qbd_ref, s_ref, p_ref, bmax_ref, alpha_ref, m_ref, knorm_ref, acc_ref, fin_ref, *, lam_init):
    TQ, TK = ATT_TQ, ATT_TK
    slope = slope_ref[h]
    tiles = range(ATT_TILES)
    tile_idx = [g * ATT_TILES + x for x in tiles]
    diag_blk = [g * (ATT_TILES // ATT_KQ) + x // ATT_KQ for x in tiles]

    def q_rows(x):
        return pl.ds(pl.multiple_of(tile_idx[x] * TQ, TQ), TQ)

    for x in tiles:
        q = q_ref[0, 0, q_rows(x), :]
        q_t = q.astype(F32).T
        row = lax.broadcasted_iota(jnp.int32, q_t.shape, 0)
        qbd_ref[x, 0:2 * DIFF_HEAD_DIM, 0:TQ] = jnp.where(row < DIFF_HEAD_DIM, q_t, 0.0).astype(BF16)
        qbd_ref[x, 0:2 * DIFF_HEAD_DIM, TQ:2 * TQ] = jnp.where(row >= DIFF_HEAD_DIM, q_t, 0.0).astype(BF16)
        qbd_ref[x, 2 * DIFF_HEAD_DIM:, :] = jnp.concatenate([coef_ref[0]] * (2 * TQ // V7X_LANES), axis=1)
        m_ref[x] = jnp.full(m_ref.shape[1:], NEG_BIG, F32)
        acc_ref[x] = jnp.zeros(acc_ref.shape[1:], F32)

    def block_of(x, t):
        return jnp.maximum(diag_blk[x] - t, 0)

    def stage_scores(x, t, slot, diagonal):
        k = kaug_ref[pl.ds(pl.multiple_of(block_of(x, t) * TK, TK), TK), :]
        s = jnp.dot(k, qbd_ref[x], preferred_element_type=F32)
        bmax = None
        for r in range(0, TK, ATT_EXP_ROWS):
            rows = slice(r, r + ATT_EXP_ROWS)
            u = s[rows] + db_ref[0, x % ATT_KQ, rows] if diagonal else s[rows]
            s_ref[x, slot, rows] = u
            cmax = jnp.max(u.reshape(ATT_EXP_ROWS // V7X_SUBLANES, V7X_SUBLANES, 2 * TQ), axis=0)
            bmax = cmax if bmax is None else jnp.maximum(bmax, cmax)
        bmax_ref[x, slot] = jnp.max(bmax, axis=0, keepdims=True)

    def stage_softmax(x, t, slot, diagonal):
        if diagonal:
            shift = 0.0
        else:
            shift = jnp.where(t < n_steps[x], 0.0, NEG_BIG)
        m = m_ref[x]
        m_new = jnp.maximum(m, bmax_ref[x, slot] + shift)
        alpha = jnp.exp2(m - m_new)
        m_ref[x] = m_new
        alpha_ref[x, slot] = alpha
        ref = m_new - shift
        for r in range(0, TK, ATT_EXP_ROWS):
            rows = slice(r, r + ATT_EXP_ROWS)
            p_ref[x, slot, rows] = jnp.exp2(s_ref[x, slot, rows] - ref).astype(BF16)

    def stage_pv(x, t, slot, last=False):
        pv = jnp.dot(vt_ref[block_of(x, t)], p_ref[x, slot], preferred_element_type=F32)
        out_ref = fin_ref if last else acc_ref
        out_ref[x] = alpha_ref[x, slot] * acc_ref[x] + pv

    _diffattn_finish(jnp.maximum(g - 1, 0), lam_ref, sub_ref, o_ref, fin_ref, lam_init=lam_init)
    for x in tiles:
        stage_scores(x, 0, 0, True)
    for x in tiles:
        stage_softmax(x, 0, 0, True)
    for x in tiles:
        stage_scores(x, 1, 1, False)

    n_steps = []
    for x in tiles:
        q1sq, q2sq = _max_sq_norms(q_ref[0, 0, q_rows(x), :])
        bound = jnp.sqrt(jnp.maximum(q1sq * knorm_ref[0:1, 0:1], q2sq * knorm_ref[1:2, 0:1])) * ATT_BOUND_MARGIN
        m_min = jnp.min(m_ref[x], axis=1, keepdims=True)
        blk = diag_blk[x].astype(F32)
        reach = blk + (ATT_EXP2_ZERO + bound - m_min) / (slope * TK)
        n = jnp.minimum(jnp.floor(reach) + 2.0, blk + 1.0)
        n_steps.append(jnp.max(n).astype(jnp.int32))

    def pipeline_tick(t, slot):
        for x in tiles:
            stage_pv(x, t, slot)
        for x in tiles:
            stage_softmax(x, t + 1, 1 - slot, False)
        for x in tiles:
            stage_scores(x, t + 2, slot, False)

    def pipeline_drain(t, slot):
        for x in tiles:
            stage_pv(x, t, slot)
        for x in tiles:
            stage_softmax(x, t + 1, 1 - slot, False)
        for x in tiles:
            stage_pv(x, t + 1, 1 - slot, last=True)

    def by_parity(fn, t):
        for slot in (0, 1):
            @pl.when(jnp.bitwise_and(t, 1) == slot)
            def _():
                fn(t, slot)

    def body(t, carry):
        by_parity(pipeline_tick, t)
        return carry

    n_ticks = jnp.maximum(functools.reduce(jnp.maximum, n_steps), 2)
    lax.fori_loop(0, n_ticks - 2, body, 0)
    by_parity(pipeline_drain, n_ticks - 2)


def _diffattn_finish(g, lam_ref, sub_ref, o_ref, fin_ref, *, lam_init):
    TQ = ATT_TQ
    lp = lam_ref[...]
    lam = (jnp.exp(jnp.sum(lp[0:1] * lp[1:2], axis=-1, keepdims=True))
           - jnp.exp(jnp.sum(lp[2:3] * lp[3:4], axis=-1, keepdims=True)) + lam_init)
    for x in range(ATT_TILES):
        o = fin_ref[x, 0:DIFF_V_DIM] * (1.0 / fin_ref[x, DIFF_V_DIM:DIFF_V_DIM + 1])
        y = (o[:, :TQ] - lam * o[:, TQ:]).T
        ms = jnp.mean(y * y, axis=-1, keepdims=True)
        y = y * lax.rsqrt(ms + NORM_EPS) * sub_ref[...] * (1.0 - lam_init)
        o_ref[0, pl.ds(pl.multiple_of((g * ATT_TILES + x) * TQ, TQ), TQ)] = y.astype(BF16)


def _diffattn(dq, dk, dv, lam_params, subln, layer_idx):
    B, H, S, w = dq.shape
    lam_init = 0.8 - 0.6 * math.exp(-0.3 * layer_idx)
    slopes, diag_bias, coef = _attn_bias_tables(H)
    NT = ATT_TILES
    seq = pl.BlockSpec((1, 1, S, w), lambda b, h, sl: (b, h, 0, 0))
    grid_spec = pltpu.PrefetchScalarGridSpec(
        num_scalar_prefetch=1,
        grid=(B, H),
        in_specs=[
            seq, seq, seq,
            pl.BlockSpec((1, ATT_KQ, ATT_TK, 2 * ATT_TQ), lambda b, h, sl: (h, 0, 0, 0)),
            pl.BlockSpec((1, V7X_LANES, V7X_LANES), lambda b, h, sl: (h, 0, 0)),
            pl.BlockSpec((4, DIFF_HEAD_DIM), lambda b, h, sl: (0, 0)),
            pl.BlockSpec((1, w), lambda b, h, sl: (0, 0)),
        ],
        out_specs=pl.BlockSpec((1, S, w), lambda b, h, sl: (b, 0, h)),
        scratch_shapes=[
            pltpu.VMEM((S // ATT_TK, w + ATT_SUM_ROWS, ATT_TK), BF16),
            pltpu.VMEM((S, 2 * w), BF16),
            pltpu.VMEM((NT, 2 * w, 2 * ATT_TQ), BF16),
            pltpu.VMEM((NT, 2, ATT_TK, 2 * ATT_TQ), F32),
            pltpu.VMEM((NT, 2, ATT_TK, 2 * ATT_TQ), BF16),
            pltpu.VMEM((NT, 2, 1, 2 * ATT_TQ), F32),
            pltpu.VMEM((NT, 2, 1, 2 * ATT_TQ), F32),
            pltpu.VMEM((NT, 1, 2 * ATT_TQ), F32),
            pltpu.VMEM((V7X_SUBLANES, V7X_LANES), F32),
            pltpu.VMEM((NT, w + ATT_SUM_ROWS, 2 * ATT_TQ), F32),
            pltpu.VMEM((NT, w + ATT_SUM_ROWS, 2 * ATT_TQ), F32),
        ],
    )
    return pl.pallas_call(
        functools.partial(_diffattn_kernel, lam_init=lam_init),
        grid_spec=grid_spec,
        out_shape=jax.ShapeDtypeStruct((B, S, H * w), BF16),
        compiler_params=_params("arbitrary", "arbitrary"),
        name="diff_attention",
    )(slopes, dq, dk, dv, diag_bias, coef, lam_params, subln.reshape(1, w))


def _merge_kernel(x_ref, mod_ref, yr_ref, yd_ref, gr_ref, gd_ref, wr_ref, wd_ref, wo_ref, o_ref):
    def sigmoid(g):
        return 0.5 * (1.0 + jnp.tanh(0.5 * g))

    br = jnp.dot(yr_ref[0], wr_ref[0], preferred_element_type=F32)
    bd = jnp.dot(yd_ref[0], wd_ref[0], preferred_element_type=F32)
    merged = sigmoid(gr_ref[0].astype(F32)) * br + sigmoid(gd_ref[0].astype(F32)) * bd
    out = jnp.dot(merged.astype(BF16), wo_ref[0], preferred_element_type=F32)
    o_ref[0] = x_ref[0] + mod_ref[0][2:3] * out


def _merge(x, mod, y_ret, y_diff, g_ret, g_diff, w_rb, w_db, w_o, layer):
    B, S, D = x.shape
    tok = pl.BlockSpec((1, MERGE_TM, D), lambda b, t: (b, t, 0))
    return pl.pallas_call(
        _merge_kernel,
        grid=(B, S // MERGE_TM),
        in_specs=[tok, pl.BlockSpec((1, 3, D), lambda b, t: (b, 0, 0)), tok, tok, tok, tok,
                  _resident_slice(w_rb.shape, (layer,)), _resident_slice(w_db.shape, (layer,)),
                  _resident_slice(w_o.shape, (layer,))],
        out_specs=tok,
        out_shape=jax.ShapeDtypeStruct(x.shape, F32),
        compiler_params=_params("parallel", "parallel"),
        name="mixer_merge",
    )(x, mod, y_ret, y_diff, g_ret, g_diff, w_rb, w_db, w_o)


def kernel(x, c, w_ada, b_ada, norm_w, w_ffn_up, w_ffn_down, w_in, ret_gn, lambda_q1, lambda_k1, lambda_q2,
           lambda_k2, diff_subln, w_ret_branch, w_diff_branch, w_out, final_norm):
    B, S, D = x.shape
    assert D == D_MODEL and S % max(FFN_TM, MERGE_TM, RET_L, ATT_TILES * ATT_TQ) == 0
    assert ATT_TK == ATT_KQ * ATT_TQ and ATT_TILES % ATT_KQ == 0
    mod_all = _adaln(c, w_ada, b_ada, norm_w).reshape(DEPTH, B, N_SUB, 3, D)
    up, down, w_in_b = w_ffn_up.astype(BF16), w_ffn_down.astype(BF16), w_in.astype(BF16)
    w_rb, w_db, w_o = w_ret_branch.astype(BF16), w_diff_branch.astype(BF16), w_out.astype(BF16)
    for l in range(DEPTH):
        mod = [mod_all[l, :, s] for s in range(N_SUB)]
        x = _ffn(x, mod[0], up, down, l, 0)
        rq, rk, rv, rg, dq, dk, dv, g_ret, g_diff = _inproj(x, mod[1], w_in_b, l)
        y_ret = _retention(rq, rk, rv, rg, ret_gn[l])
        lam_params = jnp.stack([lambda_q1[l], lambda_k1[l], lambda_q2[l], lambda_k2[l]])
        y_diff = _diffattn(dq, dk, dv, lam_params, diff_subln[l], l)
        x = _merge(x, mod[1], y_ret, y_diff, g_ret, g_diff, w_rb, w_db, w_o, l)
        x = _ffn(x, mod[2], up, down, l, 1, final_w=final_norm if l == DEPTH - 1 else None)
    return x
```

```python
import functools
import math

import jax
import jax.numpy as jnp
from jax import lax
from jax.experimental import pallas as pl
from jax.experimental.pallas import tpu as pltpu

DEPTH = 4
D_MODEL = 1024
CHUNK = 64
CHUNK_SHIFT = 6
RET_HEADS = 4
RET_QK_DIM = 128
RET_V_DIM = 256
DIFF_HEADS = 8
DIFF_HEAD_DIM = 64
DIFF_V_DIM = 2 * DIFF_HEAD_DIM
D_FF = 2816
N_SUB = 3
NORM_EPS = 1e-6

F32 = jnp.float32
BF16 = jnp.bfloat16

V7X_VMEM_LIMIT_BYTES = 56 * 1024 * 1024
V7X_LANES = 128
V7X_SUBLANES = 8

FFN_TM = 1024
PROJ_TM = 512
MERGE_TM = 1024
FFN_FC = 256
RET_L = 256
ATT_TQ = 128
ATT_KQ = 4
ATT_TK = ATT_KQ * ATT_TQ
ATT_EXP_ROWS = 16
ATT_TILES = 8
ATT_POS_LANES = 3
ATT_SUM_ROWS = 16
ATT_EXP2_ZERO = 160.0
ATT_BOUND_MARGIN = 1.001
NEG_BIG = -1e30
LOG2E = math.log2(math.e)


def _resident(shape):
    nd = len(shape)
    return pl.BlockSpec(shape, lambda *_: (0,) * nd, pipeline_mode=pl.Buffered(1))


def _resident_slice(shape, lead):
    n_lead, nd = len(lead), len(shape)
    index = tuple(lead) + (0,) * (nd - n_lead)
    return pl.BlockSpec((1,) * n_lead + tuple(shape[n_lead:]), lambda *_: index, pipeline_mode=pl.Buffered(1))


def _params(*sem):
    return pltpu.CompilerParams(dimension_semantics=sem, vmem_limit_bytes=V7X_VMEM_LIMIT_BYTES)


def _adaln_kernel(c_ref, w_ref, b_ref, nw_ref, o_ref):
    j = pl.program_id(1)
    c = c_ref[...]
    cond = c / (1.0 + jnp.exp(-c))
    r = jnp.dot(cond, w_ref[0], preferred_element_type=F32,
                precision=lax.Precision.HIGHEST) + b_ref[0]
    kind = j % 3
    sub = j // 3
    r = jnp.where(kind == 1, (1.0 + r) * nw_ref[0, 0], r)
    r = jnp.where(jnp.logical_and(kind == 2, sub != 1), 0.5 * r, r)
    o_ref[0] = r


def _adaln(c, w_ada, b_ada, norm_w):
    B = c.shape[0]
    n_tiles = N_SUB * 3
    return pl.pallas_call(
        _adaln_kernel,
        grid=(DEPTH, n_tiles),
        in_specs=[
            pl.BlockSpec((B, D_MODEL), lambda l, j: (0, 0)),
            pl.BlockSpec((1, D_MODEL, D_MODEL), lambda l, j: (l, 0, j)),
            pl.BlockSpec((1, 1, D_MODEL), lambda l, j: (l, 0, j)),
            pl.BlockSpec((1, 1, 1, D_MODEL), lambda l, j: (l, j // 3, 0, 0)),
        ],
        out_specs=pl.BlockSpec((1, B, D_MODEL), lambda l, j: (l, 0, j)),
        out_shape=jax.ShapeDtypeStruct((DEPTH, B, n_tiles * D_MODEL), F32),
        compiler_params=_params("arbitrary", "arbitrary"),
        name="adaln_mod",
    )(c, w_ada, b_ada.reshape(DEPTH, 1, n_tiles * D_MODEL), norm_w.reshape(DEPTH, N_SUB, 1, D_MODEL))


def _modulated_norm(x, mod):
    ms = jnp.mean(x * x, axis=-1, keepdims=True)
    return x * lax.rsqrt(ms + NORM_EPS) * mod[1:2] + mod[0:1]


def _ffn_kernel(x_ref, mod_ref, wup_ref, wdn_ref, *rest, n_chunks, final):
    if final:
        fw_ref, o_ref = rest
    else:
        (o_ref,) = rest
    x = x_ref[0]
    mod = mod_ref[0]
    h = _modulated_norm(x, mod).astype(BF16)
    acc = jnp.zeros(x.shape, F32)
    for c in range(n_chunks):
        cols = slice(c * FFN_FC, (c + 1) * FFN_FC)
        a = jnp.dot(h, wup_ref[0, 0, :, cols], preferred_element_type=F32)
        b = jnp.dot(h, wup_ref[0, 0, :, D_FF + c * FFN_FC:D_FF + (c + 1) * FFN_FC], preferred_element_type=F32)
        act = (0.5 * a) * (1.0 + jnp.tanh(0.5 * a)) * b
        acc = acc + jnp.dot(act.astype(BF16), wdn_ref[0, 0, cols, :], preferred_element_type=F32)
    y = x + mod[2:3] * acc
    if final:
        ms = jnp.mean(y * y, axis=-1, keepdims=True)
        y = y * lax.rsqrt(ms + NORM_EPS) * fw_ref[...]
    o_ref[0] = y


def _ffn(x, mod, wup, wdn, layer, which, final_w=None):
    B, S, D = x.shape
    n_chunks = D_FF // FFN_FC
    final = final_w is not None
    in_specs = [
        pl.BlockSpec((1, FFN_TM, D), lambda b, t: (b, t, 0)),
        pl.BlockSpec((1, 3, D), lambda b, t: (b, 0, 0)),
        _resident_slice(wup.shape, (layer, which)),
        _resident_slice(wdn.shape, (layer, which)),
    ]
    args = [x, mod, wup, wdn]
    if final:
        in_specs.append(_resident((1, D)))
        args.append(final_w.reshape(1, D))
    return pl.pallas_call(
        functools.partial(_ffn_kernel, n_chunks=n_chunks, final=final),
        grid=(B, S // FFN_TM),
        in_specs=in_specs,
        out_specs=pl.BlockSpec((1, FFN_TM, D), lambda b, t: (b, t, 0)),
        out_shape=jax.ShapeDtypeStruct(x.shape, F32),
        compiler_params=_params("parallel", "parallel"),
        name="ffn_final" if final else "ffn",
    )(*args)


def _inproj_kernel(x_ref, mod_ref, w_ref, rq_ref, rk_ref, rv_ref, rg_ref, dq_ref, dk_ref, dv_ref,
                   gr_ref, gd_ref):
    x = x_ref[0]
    h = _modulated_norm(x, mod_ref[0]).astype(BF16)

    def proj(c):
        return jnp.dot(h, w_ref[0, :, c * D_MODEL:(c + 1) * D_MODEL], preferred_element_type=F32)

    u = proj(0).astype(BF16)
    for hh in range(RET_HEADS):
        rq_ref[0, hh] = u[:, hh * RET_QK_DIM:(hh + 1) * RET_QK_DIM]
        rk_ref[0, hh] = u[:, (RET_HEADS + hh) * RET_QK_DIM:(RET_HEADS + hh + 1) * RET_QK_DIM]
    u = proj(1).astype(BF16)
    for hh in range(RET_HEADS):
        rv_ref[0, hh] = u[:, hh * RET_V_DIM:(hh + 1) * RET_V_DIM]
    rg_ref[0] = proj(2).astype(BF16)
    u = (proj(3) * (DIFF_HEAD_DIM ** -0.5 * LOG2E)).astype(BF16)
    for hh in range(DIFF_HEADS):
        dq_ref[0, hh] = u[:, hh * DIFF_V_DIM:(hh + 1) * DIFF_V_DIM]
    u = proj(4).astype(BF16)
    for hh in range(DIFF_HEADS):
        dk_ref[0, hh] = u[:, hh * DIFF_V_DIM:(hh + 1) * DIFF_V_DIM]
    u = proj(5).astype(BF16)
    for hh in range(DIFF_HEADS):
        dv_ref[0, hh] = u[:, hh * DIFF_V_DIM:(hh + 1) * DIFF_V_DIM]
    gr_ref[0] = proj(6).astype(BF16)
    gd_ref[0] = proj(7).astype(BF16)


def _inproj(x, mod, w_in, layer):
    B, S, D = x.shape
    TM = PROJ_TM
    head_spec = lambda nh, w: pl.BlockSpec((1, nh, TM, w), lambda b, t: (b, 0, t, 0))
    tok_spec = pl.BlockSpec((1, TM, D), lambda b, t: (b, t, 0))
    sds = jax.ShapeDtypeStruct
    return pl.pallas_call(
        _inproj_kernel,
        grid=(B, S // TM),
        in_specs=[tok_spec, pl.BlockSpec((1, 3, D), lambda b, t: (b, 0, 0)),
                  _resident_slice(w_in.shape, (layer,))],
        out_specs=[
            head_spec(RET_HEADS, RET_QK_DIM), head_spec(RET_HEADS, RET_QK_DIM),
            head_spec(RET_HEADS, RET_V_DIM), tok_spec,
            head_spec(DIFF_HEADS, DIFF_V_DIM), head_spec(DIFF_HEADS, DIFF_V_DIM),
            head_spec(DIFF_HEADS, DIFF_V_DIM), tok_spec, tok_spec,
        ],
        out_shape=[
            sds((B, RET_HEADS, S, RET_QK_DIM), BF16), sds((B, RET_HEADS, S, RET_QK_DIM), BF16),
            sds((B, RET_HEADS, S, RET_V_DIM), BF16), sds((B, S, D), BF16),
            sds((B, DIFF_HEADS, S, DIFF_V_DIM), BF16), sds((B, DIFF_HEADS, S, DIFF_V_DIM), BF16),
            sds((B, DIFF_HEADS, S, DIFF_V_DIM), BF16), sds((B, S, D), BF16), sds((B, S, D), BF16),
        ],
        compiler_params=_params("parallel", "parallel"),
        name="mixer_inproj",
    )(x, mod, w_in)


def _retention_consts():
    L = RET_L
    gamma = 1.0 - 2.0 ** (-5.0 - jnp.arange(RET_HEADS, dtype=F32))
    log_g = jnp.log(gamma)
    r = jnp.arange(L, dtype=F32)
    scale = RET_QK_DIM ** -0.5
    allowed = (jnp.arange(L)[None, :] // CHUNK) <= (jnp.arange(L)[:, None] // CHUNK)
    intra = jnp.exp(log_g[:, None, None] * jnp.abs(r[:, None] - r[None, :]))
    intra = jnp.where(allowed[None], intra, 0.0) * scale
    q_dec = jnp.exp(log_g[:, None] * r[None, :]) * scale
    k_dec = jnp.exp(log_g[:, None] * (L - r)[None, :])
    blk_dec = jnp.exp(log_g * L)
    q_dec = jnp.broadcast_to(q_dec[:, :, None], (RET_HEADS, L, RET_QK_DIM))
    k_dec = jnp.broadcast_to(k_dec[:, :, None], (RET_HEADS, L, RET_QK_DIM))
    blk_dec = jnp.broadcast_to(blk_dec[:, None, None], (RET_HEADS, 1, RET_V_DIM))
    return intra, q_dec, k_dec, blk_dec


def _retention_kernel(q_ref, k_ref, v_ref, g_ref, w_ref, qd_ref, kd_ref, bd_ref, gn_ref, o_ref, state_ref):
    t = pl.program_id(1)

    @pl.when(t == 0)
    def _():
        state_ref[...] = jnp.zeros_like(state_ref)

    dv = v_ref.shape[-1]
    for h in range(q_ref.shape[1]):
        q = q_ref[0, h]
        k = k_ref[0, h]
        v = v_ref[0, h]
        s = lax.dot_general(q, k, (((1,), (1,)), ((), ())), preferred_element_type=F32)
        p = (s * w_ref[h]).astype(BF16)
        y = jnp.dot(p, v, preferred_element_type=F32)
        state = state_ref[h]
        qd = (q.astype(F32) * qd_ref[h]).astype(BF16)
        y = y + jnp.dot(qd, state.astype(BF16), preferred_element_type=F32)
        kd_t = (k.astype(F32) * kd_ref[h]).T.astype(BF16)
        state_ref[h] = state * bd_ref[h] + jnp.dot(kd_t, v, preferred_element_type=F32)
        ms = jnp.mean(y * y, axis=-1, keepdims=True)
        y = y * lax.rsqrt(ms + NORM_EPS) * gn_ref[h]
        g = g_ref[0, :, h * dv:(h + 1) * dv].astype(F32)
        o_ref[0, :, h * dv:(h + 1) * dv] = (y * (0.5 * g) * (1.0 + jnp.tanh(0.5 * g))).astype(BF16)


def _retention(rq, rk, rv, rg, ret_gn_l):
    B, H, S, dk = rq.shape
    dv = rv.shape[-1]
    L = RET_L
    intra, q_dec, k_dec, blk_dec = _retention_consts()
    blk = lambda w: pl.BlockSpec((1, H, L, w), lambda b, t: (b, 0, t, 0))
    tok = pl.BlockSpec((1, L, H * dv), lambda b, t: (b, t, 0))
    return pl.pallas_call(
        _retention_kernel,
        grid=(B, S // L),
        in_specs=[
            blk(dk), blk(dk), blk(dv), tok,
            _resident((H, L, L)), _resident((H, L, dk)), _resident((H, L, dk)), _resident((H, 1, dv)),
            _resident((H, 1, dv)),
        ],
        out_specs=tok,
        out_shape=jax.ShapeDtypeStruct((B, S, H * dv), BF16),
        scratch_shapes=[pltpu.VMEM((H, dk, dv), F32)],
        compiler_params=_params("parallel", "arbitrary"),
        name="retention",
    )(rq, rk, rv, rg, intra, q_dec, k_dec, blk_dec, ret_gn_l.reshape(H, 1, dv))


def _attn_bias_tables(n_heads):
    slopes = 2.0 ** (-8.0 * jnp.arange(1, n_heads + 1, dtype=F32) / n_heads) * LOG2E
    a = jnp.arange(ATT_TK)[None, :, None]
    c = (jnp.arange(ATT_KQ) * ATT_TQ)[:, None, None] + jnp.arange(ATT_TQ)[None, None, :]
    allowed = (a // CHUNK) <= (c // CHUNK)
    diag = slopes[:, None, None, None] * (-2.0 * jnp.maximum(a - c, 0).astype(F32))[None]
    diag = jnp.where(allowed[None], diag, NEG_BIG)
    diag = jnp.concatenate([diag, diag], axis=-1)
    p0 = slopes.astype(BF16)
    p1 = (slopes - p0.astype(F32)).astype(BF16)
    p2 = (slopes - p0.astype(F32) - p1.astype(F32)).astype(BF16)
    parts = jnp.stack([p0, p1, p2], axis=1).astype(F32)
    coef = jnp.concatenate([parts * CHUNK, parts, jnp.zeros((n_heads, V7X_LANES - 2 * ATT_POS_LANES), F32)], axis=1)
    coef = jnp.broadcast_to(coef.astype(BF16)[:, :, None], (n_heads, V7X_LANES, V7X_LANES))
    return slopes, diag, coef


def _diffattn_kernel(slope_ref, q_ref, k_ref, v_ref, db_ref, coef_ref, lam_ref, sub_ref, o_ref,
                     vt_ref, kaug_ref, qbd_ref, s_ref, p_ref, bmax_ref, alpha_ref, m_ref, knorm_ref, acc_ref,
                     fin_ref, *, lam_init):
    h = pl.program_id(1)
    TK = ATT_TK
    n_kv = vt_ref.shape[0]

    @pl.when(jnp.logical_and(pl.program_id(0) == 0, h == 0))
    def _():
        for c in range(n_kv):
            ones_row = lax.broadcasted_iota(jnp.int32, (ATT_SUM_ROWS, TK), 0) == 0
            vt_ref[c, DIFF_V_DIM:DIFF_V_DIM + ATT_SUM_ROWS] = ones_row.astype(F32).astype(BF16)
            pos = c * TK + lax.broadcasted_iota(jnp.int32, (TK, V7X_LANES), 0)
            lane = lax.broadcasted_iota(jnp.int32, (TK, V7X_LANES), 1)
            feat = jnp.where(lane < ATT_POS_LANES, jnp.right_shift(pos, CHUNK_SHIFT),
                             jnp.where(lane < 2 * ATT_POS_LANES, jnp.bitwise_and(pos, CHUNK - 1), 0))
            kaug_ref[c * TK:(c + 1) * TK, 2 * DIFF_HEAD_DIM:] = feat.astype(F32).astype(BF16)

    k1sq = jnp.zeros((1, TK), F32)
    k2sq = jnp.zeros((1, TK), F32)
    for c in range(n_kv):
        vt_ref[c, 0:DIFF_V_DIM] = v_ref[0, 0, c * TK:(c + 1) * TK, :].astype(F32).T.astype(BF16)
        k = k_ref[0, 0, c * TK:(c + 1) * TK, :]
        kaug_ref[c * TK:(c + 1) * TK, 0:2 * DIFF_HEAD_DIM] = k
        ksq_t = jnp.square(k.astype(F32).T)
        k1sq = jnp.maximum(k1sq, jnp.sum(ksq_t[0:DIFF_HEAD_DIM], axis=0, keepdims=True))
        k2sq = jnp.maximum(k2sq, jnp.sum(ksq_t[DIFF_HEAD_DIM:], axis=0, keepdims=True))
    knorm_ref[0:1] = jnp.broadcast_to(jnp.max(k1sq, axis=1, keepdims=True), (1, V7X_LANES))
    knorm_ref[1:2] = jnp.broadcast_to(jnp.max(k2sq, axis=1, keepdims=True), (1, V7X_LANES))

    def group(g, carry):
        _diffattn_group(g, h, slope_ref, q_ref, db_ref, coef_ref, lam_ref, sub_ref, o_ref, vt_ref, kaug_ref,
                        qbd_ref, s_ref, p_ref, bmax_ref, alpha_ref, m_ref, knorm_ref, acc_ref, fin_ref,
                        lam_init=lam_init)
        return carry

    fin_ref[...] = jnp.zeros(fin_ref.shape, F32)
    fin_ref[:, DIFF_V_DIM:DIFF_V_DIM + 1] = jnp.ones((ATT_TILES, 1, fin_ref.shape[2]), F32)
    n_groups = q_ref.shape[2] // (ATT_TILES * ATT_TQ)
    lax.fori_loop(0, n_groups, group, 0)
    _diffattn_finish(n_groups - 1, lam_ref, sub_ref, o_ref, fin_ref, lam_init=lam_init)


def _max_sq_norms(x):
    xsq = jnp.square(x.astype(F32))
    first = lax.broadcasted_iota(jnp.int32, xsq.shape, 1) < DIFF_HEAD_DIM
    n1 = jnp.sum(jnp.where(first, xsq, 0.0), axis=1, keepdims=True)
    n2 = jnp.sum(jnp.where(first, 0.0, xsq), axis=1, keepdims=True)
    return jnp.max(n1, axis=0, keepdims=True), jnp.max(n2, axis=0, keepdims=True)


def _diffattn_group(g, h, slope_ref, q_ref, db_ref, coef_ref, lam_ref, sub_ref, o_ref, vt_ref, kaug_ref,
                    qbd_ref, s_ref, p_ref, bmax_ref, alpha_ref, m_ref, knorm_ref, acc_ref, fin_ref, *, lam_init):
    TQ, TK = ATT_TQ, ATT_TK
    slope = slope_ref[h]
    tiles = range(ATT_TILES)
    tile_idx = [g * ATT_TILES + x for x in tiles]
    diag_blk = [g * (ATT_TILES // ATT_KQ) + x // ATT_KQ for x in tiles]

    def q_rows(x):
        return pl.ds(pl.multiple_of(tile_idx[x] * TQ, TQ), TQ)

    for x in tiles:
        q = q_ref[0, 0, q_rows(x), :]
        q_t = q.astype(F32).T
        row = lax.broadcasted_iota(jnp.int32, q_t.shape, 0)
        qbd_ref[x, 0:2 * DIFF_HEAD_DIM, 0:TQ] = jnp.where(row < DIFF_HEAD_DIM, q_t, 0.0).astype(BF16)
        qbd_ref[x, 0:2 * DIFF_HEAD_DIM, TQ:2 * TQ] = jnp.where(row >= DIFF_HEAD_DIM, q_t, 0.0).astype(BF16)
        qbd_ref[x, 2 * DIFF_HEAD_DIM:, :] = jnp.concatenate([coef_ref[0]] * (2 * TQ // V7X_LANES), axis=1)
        m_ref[x] = jnp.full(m_ref.shape[1:], NEG_BIG, F32)
        acc_ref[x] = jnp.zeros(acc_ref.shape[1:], F32)

    def block_of(x, t):
        return jnp.maximum(diag_blk[x] - t, 0)

    def stage_scores(x, t, slot, diagonal):
        k = kaug_ref[pl.ds(pl.multiple_of(block_of(x, t) * TK, TK), TK), :]
        s = jnp.dot(k, qbd_ref[x], preferred_element_type=F32)
        bmax = None
        for r in range(0, TK, ATT_EXP_ROWS):
            rows = slice(r, r + ATT_EXP_ROWS)
            u = s[rows] + db_ref[0, x % ATT_KQ, rows] if diagonal else s[rows]
            s_ref[x, slot, rows] = u
            cmax = jnp.max(u.reshape(ATT_EXP_ROWS // V7X_SUBLANES, V7X_SUBLANES, 2 * TQ), axis=0)
            bmax = cmax if bmax is None else jnp.maximum(bmax, cmax)
        bmax_ref[x, slot] = jnp.max(bmax, axis=0, keepdims=True)

    def stage_softmax(x, t, slot, diagonal):
        if diagonal:
            shift = 0.0
        else:
            shift = jnp.where(t < n_steps[x], 0.0, NEG_BIG)
        m = m_ref[x]
        m_new = jnp.maximum(m, bmax_ref[x, slot] + shift)
        alpha = jnp.exp2(m - m_new)
        m_ref[x] = m_new
        alpha_ref[x, slot] = alpha
        ref = m_new - shift
        for r in range(0, TK, ATT_EXP_ROWS):
            rows = slice(r, r + ATT_EXP_ROWS)
            p_ref[x, slot, rows] = jnp.exp2(s_ref[x, slot, rows] - ref).astype(BF16)

    def stage_pv(x, t, slot, last=False):
        pv = jnp.dot(vt_ref[block_of(x, t)], p_ref[x, slot], preferred_element_type=F32)
        out_ref = fin_ref if last else acc_ref
        out_ref[x] = alpha_ref[x, slot] * acc_ref[x] + pv

    _diffattn_finish(jnp.maximum(g - 1, 0), lam_ref, sub_ref, o_ref, fin_ref, lam_init=lam_init)
    for x in tiles:
        stage_scores(x, 0, 0, True)
    for x in tiles:
        stage_softmax(x, 0, 0, True)
    for x in tiles:
        stage_scores(x, 1, 1, False)

    n_steps = []
    for x in tiles:
        q1sq, q2sq = _max_sq_norms(q_ref[0, 0, q_rows(x), :])
        bound = jnp.sqrt(jnp.maximum(q1sq * knorm_ref[0:1, 0:1], q2sq * knorm_ref[1:2, 0:1])) * ATT_BOUND_MARGIN
        m_min = jnp.min(m_ref[x], axis=1, keepdims=True)
        blk = diag_blk[x].astype(F32)
        reach = blk + (ATT_EXP2_ZERO + bound - m_min) / (slope * TK)
        n = jnp.minimum(jnp.floor(reach) + 2.0, blk + 1.0)
        n_steps.append(jnp.max(n).astype(jnp.int32))

    def pipeline_tick(t, slot):
        for x in tiles:
            stage_pv(x, t, slot)
        for x in tiles:
            stage_softmax(x, t + 1, 1 - slot, False)
        for x in tiles:
            stage_scores(x, t + 2, slot, False)

    def pipeline_drain(t, slot):
        for x in tiles:
            stage_pv(x, t, slot)
        for x in tiles:
            stage_softmax(x, t + 1, 1 - slot, False)
        for x in tiles:
            stage_pv(x, t + 1, 1 - slot, last=True)

    def by_parity(fn, t):
        for slot in (0, 1):
            @pl.when(jnp.bitwise_and(t, 1) == slot)
            def _():
                fn(t, slot)

    def body(t, carry):
        by_parity(pipeline_tick, t)
        return carry

    n_ticks = jnp.maximum(functools.reduce(jnp.maximum, n_steps), 2)
    lax.fori_loop(0, n_ticks - 2, body, 0)
    by_parity(pipeline_drain, n_ticks - 2)


def _diffattn_finish(g, lam_ref, sub_ref, o_ref, fin_ref, *, lam_init):
    TQ = ATT_TQ
    lp = lam_ref[...]
    lam = (jnp.exp(jnp.sum(lp[0:1] * lp[1:2], axis=-1, keepdims=True))
           - jnp.exp(jnp.sum(lp[2:3] * lp[3:4], axis=-1, keepdims=True)) + lam_init)
    for x in range(ATT_TILES):
        o = fin_ref[x, 0:DIFF_V_DIM] * (1.0 / fin_ref[x, DIFF_V_DIM:DIFF_V_DIM + 1])
        y = (o[:, :TQ] - lam * o[:, TQ:]).T
        ms = jnp.mean(y * y, axis=-1, keepdims=True)
        y = y * lax.rsqrt(ms + NORM_EPS) * sub_ref[...] * (1.0 - lam_init)
        o_ref[0, pl.ds(pl.multiple_of((g * ATT_TILES + x) * TQ, TQ), TQ)] = y.astype(BF16)


def _diffattn(dq, dk, dv, lam_params, subln, layer_idx):
    B, H, S, w = dq.shape
    lam_init = 0.8 - 0.6 * math.exp(-0.3 * layer_idx)
    slopes, diag_bias, coef = _attn_bias_tables(H)
    NT = ATT_TILES
    seq = pl.BlockSpec((1, 1, S, w), lambda b, h, sl: (b, h, 0, 0))
    grid_spec = pltpu.PrefetchScalarGridSpec(
        num_scalar_prefetch=1,
        grid=(B, H),
        in_specs=[
            seq, seq, seq,
            pl.BlockSpec((1, ATT_KQ, ATT_TK, 2 * ATT_TQ), lambda b, h, sl: (h, 0, 0, 0)),
            pl.BlockSpec((1, V7X_LANES, V7X_LANES), lambda b, h, sl: (h, 0, 0)),
            pl.BlockSpec((4, DIFF_HEAD_DIM), lambda b, h, sl: (0, 0)),
            pl.BlockSpec((1, w), lambda b, h, sl: (0, 0)),
        ],
        out_specs=pl.BlockSpec((1, S, w), lambda b, h, sl: (b, 0, h)),
        scratch_shapes=[
            pltpu.VMEM((S // ATT_TK, w + ATT_SUM_ROWS, ATT_TK), BF16),
            pltpu.VMEM((S, 2 * w), BF16),
            pltpu.VMEM((NT, 2 * w, 2 * ATT_TQ), BF16),
            pltpu.VMEM((NT, 2, ATT_TK, 2 * ATT_TQ), F32),
            pltpu.VMEM((NT, 2, ATT_TK, 2 * ATT_TQ), BF16),
            pltpu.VMEM((NT, 2, 1, 2 * ATT_TQ), F32),
            pltpu.VMEM((NT, 2, 1, 2 * ATT_TQ), F32),
            pltpu.VMEM((NT, 1, 2 * ATT_TQ), F32),
            pltpu.VMEM((V7X_SUBLANES, V7X_LANES), F32),
            pltpu.VMEM((NT, w + ATT_SUM_ROWS, 2 * ATT_TQ), F32),
            pltpu.VMEM((NT, w + ATT_SUM_ROWS, 2 * ATT_TQ), F32),
        ],
    )
    return pl.pallas_call(
        functools.partial(_diffattn_kernel, lam_init=lam_init),
        grid_spec=grid_spec,
        out_shape=jax.ShapeDtypeStruct((B, S, H * w), BF16),
        compiler_params=_params("arbitrary", "arbitrary"),
        name="diff_attention",
    )(slopes, dq, dk, dv, diag_bias, coef, lam_params, subln.reshape(1, w))


def _merge_kernel(x_ref, mod_ref, yr_ref, yd_ref, gr_ref, gd_ref, wr_ref, wd_ref, wo_ref, o_ref):
    def sigmoid(g):
        return 0.5 * (1.0 + jnp.tanh(0.5 * g))

    br = jnp.dot(yr_ref[0], wr_ref[0], preferred_element_type=F32)
    bd = jnp.dot(yd_ref[0], wd_ref[0], preferred_element_type=F32)
    merged = sigmoid(gr_ref[0].astype(F32)) * br + sigmoid(gd_ref[0].astype(F32)) * bd
    out = jnp.dot(merged.astype(BF16), wo_ref[0], preferred_element_type=F32)
    o_ref[0] = x_ref[0] + mod_ref[0][2:3] * out


def _merge(x, mod, y_ret, y_diff, g_ret, g_diff, w_rb, w_db, w_o, layer):
    B, S, D = x.shape
    tok = pl.BlockSpec((1, MERGE_TM, D), lambda b, t: (b, t, 0))
    return pl.pallas_call(
        _merge_kernel,
        grid=(B, S // MERGE_TM),
        in_specs=[tok, pl.BlockSpec((1, 3, D), lambda b, t: (b, 0, 0)), tok, tok, tok, tok,
                  _resident_slice(w_rb.shape, (layer,)), _resident_slice(w_db.shape, (layer,)),
                  _resident_slice(w_o.shape, (layer,))],
        out_specs=tok,
        out_shape=jax.ShapeDtypeStruct(x.shape, F32),
        compiler_params=_params("parallel", "parallel"),
        name="mixer_merge",
    )(x, mod, y_ret, y_diff, g_ret, g_diff, w_rb, w_db, w_o)


def kernel(x, c, w_ada, b_ada, norm_w, w_ffn_up, w_ffn_down, w_in, ret_gn, lambda_q1, lambda_k1, lambda_q2,
           lambda_k2, diff_subln, w_ret_branch, w_diff_branch, w_out, final_norm):
    B, S, D = x.shape
    assert D == D_MODEL and S % max(FFN_TM, MERGE_TM, RET_L, ATT_TILES * ATT_TQ) == 0
    assert ATT_TK == ATT_KQ * ATT_TQ and ATT_TILES % ATT_KQ == 0
    mod_all = _adaln(c, w_ada, b_ada, norm_w).reshape(DEPTH, B, N_SUB, 3, D)
    up, down, w_in_b = w_ffn_up.astype(BF16), w_ffn_down.astype(BF16), w_in.astype(BF16)
    w_rb, w_db, w_o = w_ret_branch.astype(BF16), w_diff_branch.astype(BF16), w_out.astype(BF16)
    for l in range(DEPTH):
        mod = [mod_all[l, :, s] for s in range(N_SUB)]
        x = _ffn(x, mod[0], up, down, l, 0)
        rq, rk, rv, rg, dq, dk, dv, g_ret, g_diff = _inproj(x, mod[1], w_in_b, l)
        y_ret = _retention(rq, rk, rv, rg, ret_gn[l])
        lam_params = jnp.stack([lambda_q1[l], lambda_k1[l], lambda_q2[l], lambda_k2[l]])
        y_diff = _diffattn(dq, dk, dv, lam_params, diff_subln[l], l)
        x = _merge(x, mod[1], y_ret, y_diff, g_ret, g_diff, w_rb, w_db, w_o, l)
        x = _ffn(x, mod[2], up, down, l, 1, final_w=final_norm if l == DEPTH - 1 else None)
    return x
```

```python
import functools
import math

import jax
import jax.numpy as jnp
from jax import lax
from jax.experimental import pallas as pl
from jax.experimental.pallas import tpu as pltpu

DEPTH = 4
D_MODEL = 1024
CHUNK = 64
CHUNK_SHIFT = 6
RET_HEADS = 4
RET_QK_DIM = 128
RET_V_DIM = 256
DIFF_HEADS = 8
DIFF_HEAD_DIM = 64
DIFF_V_DIM = 2 * DIFF_HEAD_DIM
D_FF = 2816
N_SUB = 3
NORM_EPS = 1e-6

F32 = jnp.float32
BF16 = jnp.bfloat16

V7X_VMEM_LIMIT_BYTES = 56 * 1024 * 1024
V7X_LANES = 128
V7X_SUBLANES = 8

FFN_TM = 1024
PROJ_TM = 512
MERGE_TM = 1024
FFN_FC = 256
RET_L = 256
ATT_TQ = 256
ATT_KQ = 2
ATT_TK = ATT_KQ * ATT_TQ
ATT_EXP_ROWS = 32
ATT_TILES = 4
ATT_POS_LANES = 3
ATT_SUM_ROWS = 16
ATT_EXP2_ZERO = 160.0
ATT_BOUND_MARGIN = 1.001
NEG_BIG = -1e30
LOG2E = math.log2(math.e)


def _resident(shape):
    nd = len(shape)
    return pl.BlockSpec(shape, lambda *_: (0,) * nd, pipeline_mode=pl.Buffered(1))


def _resident_slice(shape, lead):
    n_lead, nd = len(lead), len(shape)
    index = tuple(lead) + (0,) * (nd - n_lead)
    return pl.BlockSpec((1,) * n_lead + tuple(shape[n_lead:]), lambda *_: index, pipeline_mode=pl.Buffered(1))


def _params(*sem):
    return pltpu.CompilerParams(dimension_semantics=sem, vmem_limit_bytes=V7X_VMEM_LIMIT_BYTES)


def _adaln_kernel(c_ref, w_ref, b_ref, nw_ref, o_ref):
    j = pl.program_id(1)
    c = c_ref[...]
    cond = c / (1.0 + jnp.exp(-c))
    r = jnp.dot(cond, w_ref[0], preferred_element_type=F32,
                precision=lax.Precision.HIGHEST) + b_ref[0]
    kind = j % 3
    sub = j // 3
    r = jnp.where(kind == 1, (1.0 + r) * nw_ref[0, 0], r)
    r = jnp.where(jnp.logical_and(kind == 2, sub != 1), 0.5 * r, r)
    o_ref[0] = r


def _adaln(c, w_ada, b_ada, norm_w):
    B = c.shape[0]
    n_tiles = N_SUB * 3
    return pl.pallas_call(
        _adaln_kernel,
        grid=(DEPTH, n_tiles),
        in_specs=[
            pl.BlockSpec((B, D_MODEL), lambda l, j: (0, 0)),
            pl.BlockSpec((1, D_MODEL, D_MODEL), lambda l, j: (l, 0, j)),
            pl.BlockSpec((1, 1, D_MODEL), lambda l, j: (l, 0, j)),
            pl.BlockSpec((1, 1, 1, D_MODEL), lambda l, j: (l, j // 3, 0, 0)),
        ],
        out_specs=pl.BlockSpec((1, B, D_MODEL), lambda l, j: (l, 0, j)),
        out_shape=jax.ShapeDtypeStruct((DEPTH, B, n_tiles * D_MODEL), F32),
        compiler_params=_params("arbitrary", "arbitrary"),
        name="adaln_mod",
    )(c, w_ada, b_ada.reshape(DEPTH, 1, n_tiles * D_MODEL), norm_w.reshape(DEPTH, N_SUB, 1, D_MODEL))


def _modulated_norm(x, mod):
    ms = jnp.mean(x * x, axis=-1, keepdims=True)
    return x * lax.rsqrt(ms + NORM_EPS) * mod[1:2] + mod[0:1]


def _ffn_kernel(x_ref, mod_ref, wup_ref, wdn_ref, *rest, n_chunks, final):
    if final:
        fw_ref, o_ref = rest
    else:
        (o_ref,) = rest
    x = x_ref[0]
    mod = mod_ref[0]
    h = _modulated_norm(x, mod).astype(BF16)
    acc = jnp.zeros(x.shape, F32)
    for c in range(n_chunks):
        cols = slice(c * FFN_FC, (c + 1) * FFN_FC)
        a = jnp.dot(h, wup_ref[0, 0, :, cols], preferred_element_type=F32)
        b = jnp.dot(h, wup_ref[0, 0, :, D_FF + c * FFN_FC:D_FF + (c + 1) * FFN_FC], preferred_element_type=F32)
        act = (0.5 * a) * (1.0 + jnp.tanh(0.5 * a)) * b
        acc = acc + jnp.dot(act.astype(BF16), wdn_ref[0, 0, cols, :], preferred_element_type=F32)
    y = x + mod[2:3] * acc
    if final:
        ms = jnp.mean(y * y, axis=-1, keepdims=True)
        y = y * lax.rsqrt(ms + NORM_EPS) * fw_ref[...]
    o_ref[0] = y


def _ffn(x, mod, wup, wdn, layer, which, final_w=None):
    B, S, D = x.shape
    n_chunks = D_FF // FFN_FC
    final = final_w is not None
    in_specs = [
        pl.BlockSpec((1, FFN_TM, D), lambda b, t: (b, t, 0)),
        pl.BlockSpec((1, 3, D), lambda b, t: (b, 0, 0)),
        _resident_slice(wup.shape, (layer, which)),
        _resident_slice(wdn.shape, (layer, which)),
    ]
    args = [x, mod, wup, wdn]
    if final:
        in_specs.append(_resident((1, D)))
        args.append(final_w.reshape(1, D))
    return pl.pallas_call(
        functools.partial(_ffn_kernel, n_chunks=n_chunks, final=final),
        grid=(B, S // FFN_TM),
        in_specs=in_specs,
        out_specs=pl.BlockSpec((1, FFN_TM, D), lambda b, t: (b, t, 0)),
        out_shape=jax.ShapeDtypeStruct(x.shape, F32),
        compiler_params=_params("parallel", "parallel"),
        name="ffn_final" if final else "ffn",
    )(*args)


def _inproj_kernel(x_ref, mod_ref, w_ref, rq_ref, rk_ref, rv_ref, rg_ref, dq_ref, dk_ref, dv_ref,
                   gr_ref, gd_ref):
    x = x_ref[0]
    h = _modulated_norm(x, mod_ref[0]).astype(BF16)

    def proj(c):
        return jnp.dot(h, w_ref[0, :, c * D_MODEL:(c + 1) * D_MODEL], preferred_element_type=F32)

    u = proj(0).astype(BF16)
    for hh in range(RET_HEADS):
        rq_ref[0, hh] = u[:, hh * RET_QK_DIM:(hh + 1) * RET_QK_DIM]
        rk_ref[0, hh] = u[:, (RET_HEADS + hh) * RET_QK_DIM:(RET_HEADS + hh + 1) * RET_QK_DIM]
    u = proj(1).astype(BF16)
    for hh in range(RET_HEADS):
        rv_ref[0, hh] = u[:, hh * RET_V_DIM:(hh + 1) * RET_V_DIM]
    rg_ref[0] = proj(2).astype(BF16)
    u = (proj(3) * (DIFF_HEAD_DIM ** -0.5 * LOG2E)).astype(BF16)
    for hh in range(DIFF_HEADS):
        dq_ref[0, hh] = u[:, hh * DIFF_V_DIM:(hh + 1) * DIFF_V_DIM]
    u = proj(4).astype(BF16)
    for hh in range(DIFF_HEADS):
        dk_ref[0, hh] = u[:, hh * DIFF_V_DIM:(hh + 1) * DIFF_V_DIM]
    u = proj(5).astype(BF16)
    for hh in range(DIFF_HEADS):
        dv_ref[0, hh] = u[:, hh * DIFF_V_DIM:(hh + 1) * DIFF_V_DIM]
    gr_ref[0] = proj(6).astype(BF16)
    gd_ref[0] = proj(7).astype(BF16)


def _inproj(x, mod, w_in, layer):
    B, S, D = x.shape
    TM = PROJ_TM
    head_spec = lambda nh, w: pl.BlockSpec((1, nh, TM, w), lambda b, t: (b, 0, t, 0))
    tok_spec = pl.BlockSpec((1, TM, D), lambda b, t: (b, t, 0))
    sds = jax.ShapeDtypeStruct
    return pl.pallas_call(
        _inproj_kernel,
        grid=(B, S // TM),
        in_specs=[tok_spec, pl.BlockSpec((1, 3, D), lambda b, t: (b, 0, 0)),
                  _resident_slice(w_in.shape, (layer,))],
        out_specs=[
            head_spec(RET_HEADS, RET_QK_DIM), head_spec(RET_HEADS, RET_QK_DIM),
            head_spec(RET_HEADS, RET_V_DIM), tok_spec,
            head_spec(DIFF_HEADS, DIFF_V_DIM), head_spec(DIFF_HEADS, DIFF_V_DIM),
            head_spec(DIFF_HEADS, DIFF_V_DIM), tok_spec, tok_spec,
        ],
        out_shape=[
            sds((B, RET_HEADS, S, RET_QK_DIM), BF16), sds((B, RET_HEADS, S, RET_QK_DIM), BF16),
            sds((B, RET_HEADS, S, RET_V_DIM), BF16), sds((B, S, D), BF16),
            sds((B, DIFF_HEADS, S, DIFF_V_DIM), BF16), sds((B, DIFF_HEADS, S, DIFF_V_DIM), BF16),
            sds((B, DIFF_HEADS, S, DIFF_V_DIM), BF16), sds((B, S, D), BF16), sds((B, S, D), BF16),
        ],
        compiler_params=_params("parallel", "parallel"),
        name="mixer_inproj",
    )(x, mod, w_in)


def _retention_consts():
    L = RET_L
    gamma = 1.0 - 2.0 ** (-5.0 - jnp.arange(RET_HEADS, dtype=F32))
    log_g = jnp.log(gamma)
    r = jnp.arange(L, dtype=F32)
    scale = RET_QK_DIM ** -0.5
    allowed = (jnp.arange(L)[None, :] // CHUNK) <= (jnp.arange(L)[:, None] // CHUNK)
    intra = jnp.exp(log_g[:, None, None] * jnp.abs(r[:, None] - r[None, :]))
    intra = jnp.where(allowed[None], intra, 0.0) * scale
    q_dec = jnp.exp(log_g[:, None] * r[None, :]) * scale
    k_dec = jnp.exp(log_g[:, None] * (L - r)[None, :])
    blk_dec = jnp.exp(log_g * L)
    q_dec = jnp.broadcast_to(q_dec[:, :, None], (RET_HEADS, L, RET_QK_DIM))
    k_dec = jnp.broadcast_to(k_dec[:, :, None], (RET_HEADS, L, RET_QK_DIM))
    blk_dec = jnp.broadcast_to(blk_dec[:, None, None], (RET_HEADS, 1, RET_V_DIM))
    return intra, q_dec, k_dec, blk_dec


def _retention_kernel(q_ref, k_ref, v_ref, g_ref, w_ref, qd_ref, kd_ref, bd_ref, gn_ref, o_ref, state_ref):
    t = pl.program_id(1)

    @pl.when(t == 0)
    def _():
        state_ref[...] = jnp.zeros_like(state_ref)

    dv = v_ref.shape[-1]
    for h in range(q_ref.shape[1]):
        q = q_ref[0, h]
        k = k_ref[0, h]
        v = v_ref[0, h]
        s = lax.dot_general(q, k, (((1,), (1,)), ((), ())), preferred_element_type=F32)
        p = (s * w_ref[h]).astype(BF16)
        y = jnp.dot(p, v, preferred_element_type=F32)
        state = state_ref[h]
        qd = (q.astype(F32) * qd_ref[h]).astype(BF16)
        y = y + jnp.dot(qd, state.astype(BF16), preferred_element_type=F32)
        kd_t = (k.astype(F32) * kd_ref[h]).T.astype(BF16)
        state_ref[h] = state * bd_ref[h] + jnp.dot(kd_t, v, preferred_element_type=F32)
        ms = jnp.mean(y * y, axis=-1, keepdims=True)
        y = y * lax.rsqrt(ms + NORM_EPS) * gn_ref[h]
        g = g_ref[0, :, h * dv:(h + 1) * dv].astype(F32)
        o_ref[0, :, h * dv:(h + 1) * dv] = (y * (0.5 * g) * (1.0 + jnp.tanh(0.5 * g))).astype(BF16)


def _retention(rq, rk, rv, rg, ret_gn_l):
    B, H, S, dk = rq.shape
    dv = rv.shape[-1]
    L = RET_L
    intra, q_dec, k_dec, blk_dec = _retention_consts()
    blk = lambda w: pl.BlockSpec((1, H, L, w), lambda b, t: (b, 0, t, 0))
    tok = pl.BlockSpec((1, L, H * dv), lambda b, t: (b, t, 0))
    return pl.pallas_call(
        _retention_kernel,
        grid=(B, S // L),
        in_specs=[
            blk(dk), blk(dk), blk(dv), tok,
            _resident((H, L, L)), _resident((H, L, dk)), _resident((H, L, dk)), _resident((H, 1, dv)),
            _resident((H, 1, dv)),
        ],
        out_specs=tok,
        out_shape=jax.ShapeDtypeStruct((B, S, H * dv), BF16),
        scratch_shapes=[pltpu.VMEM((H, dk, dv), F32)],
        compiler_params=_params("parallel", "arbitrary"),
        name="retention",
    )(rq, rk, rv, rg, intra, q_dec, k_dec, blk_dec, ret_gn_l.reshape(H, 1, dv))


def _attn_bias_tables(n_heads):
    slopes = 2.0 ** (-8.0 * jnp.arange(1, n_heads + 1, dtype=F32) / n_heads) * LOG2E
    a = jnp.arange(ATT_TK)[None, :, None]
    c = (jnp.arange(ATT_KQ) * ATT_TQ)[:, None, None] + jnp.arange(ATT_TQ)[None, None, :]
    allowed = (a // CHUNK) <= (c // CHUNK)
    diag = slopes[:, None, None, None] * (-2.0 * jnp.maximum(a - c, 0).astype(F32))[None]
    diag = jnp.where(allowed[None], diag, NEG_BIG)
    diag = jnp.concatenate([diag, diag], axis=-1)
    p0 = slopes.astype(BF16)
    p1 = (slopes - p0.astype(F32)).astype(BF16)
    p2 = (slopes - p0.astype(F32) - p1.astype(F32)).astype(BF16)
    parts = jnp.stack([p0, p1, p2], axis=1).astype(F32)
    coef = jnp.concatenate([parts * CHUNK, parts, jnp.zeros((n_heads, V7X_LANES - 2 * ATT_POS_LANES), F32)], axis=1)
    coef = jnp.broadcast_to(coef.astype(BF16)[:, :, None], (n_heads, V7X_LANES, V7X_LANES))
    return slopes, diag, coef


def _diffattn_kernel(slope_ref, q_ref, k_ref, v_ref, db_ref, coef_ref, lam_ref, sub_ref, o_ref,
                     vt_ref, kaug_ref, qbd_ref, s_ref, p_ref, bmax_ref, alpha_ref, m_ref, knorm_ref, acc_ref,
                     fin_ref, *, lam_init):
    h = pl.program_id(1)
    TK = ATT_TK
    n_kv = vt_ref.shape[0]

    @pl.when(jnp.logical_and(pl.program_id(0) == 0, h == 0))
    def _():
        for c in range(n_kv):
            ones_row = lax.broadcasted_iota(jnp.int32, (ATT_SUM_ROWS, TK), 0) == 0
            vt_ref[c, DIFF_V_DIM:DIFF_V_DIM + ATT_SUM_ROWS] = ones_row.astype(F32).astype(BF16)
            pos = c * TK + lax.broadcasted_iota(jnp.int32, (TK, V7X_LANES), 0)
            lane = lax.broadcasted_iota(jnp.int32, (TK, V7X_LANES), 1)
            feat = jnp.where(lane < ATT_POS_LANES, jnp.right_shift(pos, CHUNK_SHIFT),
                             jnp.where(lane < 2 * ATT_POS_LANES, jnp.bitwise_and(pos, CHUNK - 1), 0))
            kaug_ref[c * TK:(c + 1) * TK, 2 * DIFF_HEAD_DIM:] = feat.astype(F32).astype(BF16)

    k1sq = jnp.zeros((1, TK), F32)
    k2sq = jnp.zeros((1, TK), F32)
    for c in range(n_kv):
        vt_ref[c, 0:DIFF_V_DIM] = v_ref[0, 0, c * TK:(c + 1) * TK, :].astype(F32).T.astype(BF16)
        k = k_ref[0, 0, c * TK:(c + 1) * TK, :]
        kaug_ref[c * TK:(c + 1) * TK, 0:2 * DIFF_HEAD_DIM] = k
        ksq_t = jnp.square(k.astype(F32).T)
        k1sq = jnp.maximum(k1sq, jnp.sum(ksq_t[0:DIFF_HEAD_DIM], axis=0, keepdims=True))
        k2sq = jnp.maximum(k2sq, jnp.sum(ksq_t[DIFF_HEAD_DIM:], axis=0, keepdims=True))
    knorm_ref[0:1] = jnp.broadcast_to(jnp.max(k1sq, axis=1, keepdims=True), (1, V7X_LANES))
    knorm_ref[1:2] = jnp.broadcast_to(jnp.max(k2sq, axis=1, keepdims=True), (1, V7X_LANES))

    def group(g, carry):
        _diffattn_group(g, h, slope_ref, q_ref, db_ref, coef_ref, lam_ref, sub_ref, o_ref, vt_ref, kaug_ref,
                        qbd_ref, s_ref, p_ref, bmax_ref, alpha_ref, m_ref, knorm_ref, acc_ref, fin_ref,
                        lam_init=lam_init)
        return carry

    fin_ref[...] = jnp.zeros(fin_ref.shape, F32)
    fin_ref[:, DIFF_V_DIM:DIFF_V_DIM + 1] = jnp.ones((ATT_TILES, 1, fin_ref.shape[2]), F32)
    n_groups = q_ref.shape[2] // (ATT_TILES * ATT_TQ)
    lax.fori_loop(0, n_groups, group, 0)
    _diffattn_finish(n_groups - 1, lam_ref, sub_ref, o_ref, fin_ref, lam_init=lam_init)


def _max_sq_norms(x):
    xsq = jnp.square(x.astype(F32))
    first = lax.broadcasted_iota(jnp.int32, xsq.shape, 1) < DIFF_HEAD_DIM
    n1 = jnp.sum(jnp.where(first, xsq, 0.0), axis=1, keepdims=True)
    n2 = jnp.sum(jnp.where(first, 0.0, xsq), axis=1, keepdims=True)
    return jnp.max(n1, axis=0, keepdims=True), jnp.max(n2, axis=0, keepdims=True)


def _diffattn_group(g, h, slope_ref, q_ref, db_ref, coef_ref, lam_ref, sub_ref, o_ref, vt_ref, kaug_ref,
                    qbd_ref, s_ref, p_ref, bmax_ref, alpha_ref, m_ref, knorm_ref, acc_ref, fin_ref, *, lam_init):
    TQ, TK = ATT_TQ, ATT_TK
    slope = slope_ref[h]
    tiles = range(ATT_TILES)
    tile_idx = [g * ATT_TILES + x for x in tiles]
    diag_blk = [g * (ATT_TILES // ATT_KQ) + x // ATT_KQ for x in tiles]

    def q_rows(x):
        return pl.ds(pl.multiple_of(tile_idx[x] * TQ, TQ), TQ)

    for x in tiles:
        q = q_ref[0, 0, q_rows(x), :]
        q_t = q.astype(F32).T
        row = lax.broadcasted_iota(jnp.int32, q_t.shape, 0)
        qbd_ref[x, 0:2 * DIFF_HEAD_DIM, 0:TQ] = jnp.where(row < DIFF_HEAD_DIM, q_t, 0.0).astype(BF16)
        qbd_ref[x, 0:2 * DIFF_HEAD_DIM, TQ:2 * TQ] = jnp.where(row >= DIFF_HEAD_DIM, q_t, 0.0).astype(BF16)
        qbd_ref[x, 2 * DIFF_HEAD_DIM:, :] = jnp.concatenate([coef_ref[0]] * (2 * TQ // V7X_LANES), axis=1)
        m_ref[x] = jnp.full(m_ref.shape[1:], NEG_BIG, F32)
        acc_ref[x] = jnp.zeros(acc_ref.shape[1:], F32)

    def block_of(x, t):
        return jnp.maximum(diag_blk[x] - t, 0)

    def stage_scores(x, t, slot, diagonal):
        k = kaug_ref[pl.ds(pl.multiple_of(block_of(x, t) * TK, TK), TK), :]
        s = jnp.dot(k, qbd_ref[x], preferred_element_type=F32)
        bmax = None
        for r in range(0, TK, ATT_EXP_ROWS):
            rows = slice(r, r + ATT_EXP_ROWS)
            u = s[rows] + db_ref[0, x % ATT_KQ, rows] if diagonal else s[rows]
            s_ref[x, slot, rows] = u
            cmax = jnp.max(u.reshape(ATT_EXP_ROWS // V7X_SUBLANES, V7X_SUBLANES, 2 * TQ), axis=0)
            bmax = cmax if bmax is None else jnp.maximum(bmax, cmax)
        bmax_ref[x, slot] = jnp.max(bmax, axis=0, keepdims=True)

    def stage_softmax(x, t, slot, diagonal):
        if diagonal:
            shift = 0.0
        else:
            shift = jnp.where(t < n_steps[x], 0.0, NEG_BIG)
        m = m_ref[x]
        m_new = jnp.maximum(m, bmax_ref[x, slot] + shift)
        alpha = jnp.exp2(m - m_new)
        m_ref[x] = m_new
        alpha_ref[x, slot] = alpha
        ref = m_new - shift
        for r in range(0, TK, ATT_EXP_ROWS):
            rows = slice(r, r + ATT_EXP_ROWS)
            p_ref[x, slot, rows] = jnp.exp2(s_ref[x, slot, rows] - ref).astype(BF16)

    def stage_pv(x, t, slot, last=False):
        pv = jnp.dot(vt_ref[block_of(x, t)], p_ref[x, slot], preferred_element_type=F32)
        out_ref = fin_ref if last else acc_ref
        out_ref[x] = alpha_ref[x, slot] * acc_ref[x] + pv

    _diffattn_finish(jnp.maximum(g - 1, 0), lam_ref, sub_ref, o_ref, fin_ref, lam_init=lam_init)
    for x in tiles:
        stage_scores(x, 0, 0, True)
    for x in tiles:
        stage_softmax(x, 0, 0, True)
    for x in tiles:
        stage_scores(x, 1, 1, False)

    n_steps = []
    for x in tiles:
        q1sq, q2sq = _max_sq_norms(q_ref[0, 0, q_rows(x), :])
        bound = jnp.sqrt(jnp.maximum(q1sq * knorm_ref[0:1, 0:1], q2sq * knorm_ref[1:2, 0:1])) * ATT_BOUND_MARGIN
        m_min = jnp.min(m_ref[x], axis=1, keepdims=True)
        blk = diag_blk[x].astype(F32)
        reach = blk + (ATT_EXP2_ZERO + bound - m_min) / (slope * TK)
        n = jnp.minimum(jnp.floor(reach) + 2.0, blk + 1.0)
        n_steps.append(jnp.max(n).astype(jnp.int32))

    def pipeline_tick(t, slot):
        for x in tiles:
            stage_pv(x, t, slot)
        for x in tiles:
            stage_softmax(x, t + 1, 1 - slot, False)
        for x in tiles:
            stage_scores(x, t + 2, slot, False)

    def pipeline_drain(t, slot):
        for x in tiles:
            stage_pv(x, t, slot)
        for x in tiles:
            stage_softmax(x, t + 1, 1 - slot, False)
        for x in tiles:
            stage_pv(x, t + 1, 1 - slot, last=True)

    def by_parity(fn, t):
        for slot in (0, 1):
            @pl.when(jnp.bitwise_and(t, 1) == slot)
            def _():
                fn(t, slot)

    def body(t, carry):
        by_parity(pipeline_tick, t)
        return carry

    n_ticks = jnp.maximum(functools.reduce(jnp.maximum, n_steps), 2)
    lax.fori_loop(0, n_ticks - 2, body, 0)
    by_parity(pipeline_drain, n_ticks - 2)


def _diffattn_finish(g, lam_ref, sub_ref, o_ref, fin_ref, *, lam_init):
    TQ = ATT_TQ
    lp = lam_ref[...]
    lam = (jnp.exp(jnp.sum(lp[0:1] * lp[1:2], axis=-1, keepdims=True))
           - jnp.exp(jnp.sum(lp[2:3] * lp[3:4], axis=-1, keepdims=True)) + lam_init)
    for x in range(ATT_TILES):
        o = fin_ref[x, 0:DIFF_V_DIM] * (1.0 / fin_ref[x, DIFF_V_DIM:DIFF_V_DIM + 1])
        y = (o[:, :TQ] - lam * o[:, TQ:]).T
        ms = jnp.mean(y * y, axis=-1, keepdims=True)
        y = y * lax.rsqrt(ms + NORM_EPS) * sub_ref[...] * (1.0 - lam_init)
        o_ref[0, pl.ds(pl.multiple_of((g * ATT_TILES + x) * TQ, TQ), TQ)] = y.astype(BF16)


def _diffattn(dq, dk, dv, lam_params, subln, layer_idx):
    B, H, S, w = dq.shape
    lam_init = 0.8 - 0.6 * math.exp(-0.3 * layer_idx)
    slopes, diag_bias, coef = _attn_bias_tables(H)
    NT = ATT_TILES
    seq = pl.BlockSpec((1, 1, S, w), lambda b, h, sl: (b, h, 0, 0))
    grid_spec = pltpu.PrefetchScalarGridSpec(
        num_scalar_prefetch=1,
        grid=(B, H),
        in_specs=[
            seq, seq, seq,
            pl.BlockSpec((1, ATT_KQ, ATT_TK, 2 * ATT_TQ), lambda b, h, sl: (h, 0, 0, 0)),
            pl.BlockSpec((1, V7X_LANES, V7X_LANES), lambda b, h, sl: (h, 0, 0)),
            pl.BlockSpec((4, DIFF_HEAD_DIM), lambda b, h, sl: (0, 0)),
            pl.BlockSpec((1, w), lambda b, h, sl: (0, 0)),
        ],
        out_specs=pl.BlockSpec((1, S, w), lambda b, h, sl: (b, 0, h)),
        scratch_shapes=[
            pltpu.VMEM((S // ATT_TK, w + ATT_SUM_ROWS, ATT_TK), BF16),
            pltpu.VMEM((S, 2 * w), BF16),
            pltpu.VMEM((NT, 2 * w, 2 * ATT_TQ), BF16),
            pltpu.VMEM((NT, 2, ATT_TK, 2 * ATT_TQ), F32),
            pltpu.VMEM((NT, 2, ATT_TK, 2 * ATT_TQ), BF16),
            pltpu.VMEM((NT, 2, 1, 2 * ATT_TQ), F32),
            pltpu.VMEM((NT, 2, 1, 2 * ATT_TQ), F32),
            pltpu.VMEM((NT, 1, 2 * ATT_TQ), F32),
            pltpu.VMEM((V7X_SUBLANES, V7X_LANES), F32),
            pltpu.VMEM((NT, w + ATT_SUM_ROWS, 2 * ATT_TQ), F32),
            pltpu.VMEM((NT, w + ATT_SUM_ROWS, 2 * ATT_TQ), F32),
        ],
    )
    return pl.pallas_call(
        functools.partial(_diffattn_kernel, lam_init=lam_init),
        grid_spec=grid_spec,
        out_shape=jax.ShapeDtypeStruct((B, S, H * w), BF16),
        compiler_params=_params("arbitrary", "arbitrary"),
        name="diff_attention",
    )(slopes, dq, dk, dv, diag_bias, coef, lam_params, subln.reshape(1, w))


def _merge_kernel(x_ref, mod_ref, yr_ref, yd_ref, gr_ref, gd_ref, wr_ref, wd_ref, wo_ref, o_ref):
    def sigmoid(g):
        return 0.5 * (1.0 + jnp.tanh(0.5 * g))

    br = jnp.dot(yr_ref[0], wr_ref[0], preferred_element_type=F32)
    bd = jnp.dot(yd_ref[0], wd_ref[0], preferred_element_type=F32)
    merged = sigmoid(gr_ref[0].astype(F32)) * br + sigmoid(gd_ref[0].astype(F32)) * bd
    out = jnp.dot(merged.astype(BF16), wo_ref[0], preferred_element_type=F32)
    o_ref[0] = x_ref[0] + mod_ref[0][2:3] * out


def _merge(x, mod, y_ret, y_diff, g_ret, g_diff, w_rb, w_db, w_o, layer):
    B, S, D = x.shape
    tok = pl.BlockSpec((1, MERGE_TM, D), lambda b, t: (b, t, 0))
    return pl.pallas_call(
        _merge_kernel,
        grid=(B, S // MERGE_TM),
        in_specs=[tok, pl.BlockSpec((1, 3, D), lambda b, t: (b, 0, 0)), tok, tok, tok, tok,
                  _resident_slice(w_rb.shape, (layer,)), _resident_slice(w_db.shape, (layer,)),
                  _resident_slice(w_o.shape, (layer,))],
        out_specs=tok,
        out_shape=jax.ShapeDtypeStruct(x.shape, F32),
        compiler_params=_params("parallel", "parallel"),
        name="mixer_merge",
    )(x, mod, y_ret, y_diff, g_ret, g_diff, w_rb, w_db, w_o)


def kernel(x, c, w_ada, b_ada, norm_w, w_ffn_up, w_ffn_down, w_in, ret_gn, lambda_q1, lambda_k1, lambda_q2,
           lambda_k2, diff_subln, w_ret_branch, w_diff_branch, w_out, final_norm):
    B, S, D = x.shape
    assert D == D_MODEL and S % max(FFN_TM, MERGE_TM, RET_L, ATT_TILES * ATT_TQ) == 0
    assert ATT_TK == ATT_KQ * ATT_TQ and ATT_TILES % ATT_KQ == 0
    mod_all = _adaln(c, w_ada, b_ada, norm_w).reshape(DEPTH, B, N_SUB, 3, D)
    up, down, w_in_b = w_ffn_up.astype(BF16), w_ffn_down.astype(BF16), w_in.astype(BF16)
    w_rb, w_db, w_o = w_ret_branch.astype(BF16), w_diff_branch.astype(BF16), w_out.astype(BF16)
    for l in range(DEPTH):
        mod = [mod_all[l, :, s] for s in range(N_SUB)]
        x = _ffn(x, mod[0], up, down, l, 0)
        rq, rk, rv, rg, dq, dk, dv, g_ret, g_diff = _inproj(x, mod[1], w_in_b, l)
        y_ret = _retention(rq, rk, rv, rg, ret_gn[l])
        lam_params = jnp.stack([lambda_q1[l], lambda_k1[l], lambda_q2[l], lambda_k2[l]])
        y_diff = _diffattn(dq, dk, dv, lam_params, diff_subln[l], l)
        x = _merge(x, mod[1], y_ret, y_diff, g_ret, g_diff, w_rb, w_db, w_o, l)
        x = _ffn(x, mod[2], up, down, l, 1, final_w=final_norm if l == DEPTH - 1 else None)
    return x
```

```python
import functools
import math

import jax
import jax.numpy as jnp
from jax import lax
from jax.experimental import pallas as pl
from jax.experimental.pallas import tpu as pltpu

DEPTH = 4
D_MODEL = 1024
CHUNK = 64
CHUNK_SHIFT = 6
RET_HEADS = 4
RET_QK_DIM = 128
RET_V_DIM = 256
DIFF_HEADS = 8
DIFF_HEAD_DIM = 64
DIFF_V_DIM = 2 * DIFF_HEAD_DIM
D_FF = 2816
N_SUB = 3
NORM_EPS = 1e-6

F32 = jnp.float32
BF16 = jnp.bfloat16

V7X_VMEM_LIMIT_BYTES = 56 * 1024 * 1024
V7X_LANES = 128
V7X_SUBLANES = 8

FFN_TM = 1024
PROJ_TM = 512
MERGE_TM = 1024
FFN_FC = 256
RET_L = 256
ATT_TQ = 256
ATT_KQ = 2
ATT_TK = ATT_KQ * ATT_TQ
ATT_EXP_ROWS = 16
ATT_TILES = 4
ATT_POS_LANES = 3
ATT_SUM_ROWS = 16
ATT_EXP2_ZERO = 160.0
ATT_BOUND_MARGIN = 1.001
NEG_BIG = -1e30
LOG2E = math.log2(math.e)


def _resident(shape):
    nd = len(shape)
    return pl.BlockSpec(shape, lambda *_: (0,) * nd, pipeline_mode=pl.Buffered(1))


def _resident_slice(shape, lead):
    n_lead, nd = len(lead), len(shape)
    index = tuple(lead) + (0,) * (nd - n_lead)
    return pl.BlockSpec((1,) * n_lead + tuple(shape[n_lead:]), lambda *_: index, pipeline_mode=pl.Buffered(1))


def _params(*sem):
    return pltpu.CompilerParams(dimension_semantics=sem, vmem_limit_bytes=V7X_VMEM_LIMIT_BYTES)


def _adaln_kernel(c_ref, w_ref, b_ref, nw_ref, o_ref):
    j = pl.program_id(1)
    c = c_ref[...]
    cond = c / (1.0 + jnp.exp(-c))
    r = jnp.dot(cond, w_ref[0], preferred_element_type=F32,
                precision=lax.Precision.HIGHEST) + b_ref[0]
    kind = j % 3
    sub = j // 3
    r = jnp.where(kind == 1, (1.0 + r) * nw_ref[0, 0], r)
    r = jnp.where(jnp.logical_and(kind == 2, sub != 1), 0.5 * r, r)
    o_ref[0] = r


def _adaln(c, w_ada, b_ada, norm_w):
    B = c.shape[0]
    n_tiles = N_SUB * 3
    return pl.pallas_call(
        _adaln_kernel,
        grid=(DEPTH, n_tiles),
        in_specs=[
            pl.BlockSpec((B, D_MODEL), lambda l, j: (0, 0)),
            pl.BlockSpec((1, D_MODEL, D_MODEL), lambda l, j: (l, 0, j)),
            pl.BlockSpec((1, 1, D_MODEL), lambda l, j: (l, 0, j)),
            pl.BlockSpec((1, 1, 1, D_MODEL), lambda l, j: (l, j // 3, 0, 0)),
        ],
        out_specs=pl.BlockSpec((1, B, D_MODEL), lambda l, j: (l, 0, j)),
        out_shape=jax.ShapeDtypeStruct((DEPTH, B, n_tiles * D_MODEL), F32),
        compiler_params=_params("arbitrary", "arbitrary"),
        name="adaln_mod",
    )(c, w_ada, b_ada.reshape(DEPTH, 1, n_tiles * D_MODEL), norm_w.reshape(DEPTH, N_SUB, 1, D_MODEL))


def _modulated_norm(x, mod):
    ms = jnp.mean(x * x, axis=-1, keepdims=True)
    return x * lax.rsqrt(ms + NORM_EPS) * mod[1:2] + mod[0:1]


def _ffn_kernel(x_ref, mod_ref, wup_ref, wdn_ref, *rest, n_chunks, final):
    if final:
        fw_ref, o_ref = rest
    else:
        (o_ref,) = rest
    x = x_ref[0]
    mod = mod_ref[0]
    h = _modulated_norm(x, mod).astype(BF16)
    acc = jnp.zeros(x.shape, F32)
    for c in range(n_chunks):
        cols = slice(c * FFN_FC, (c + 1) * FFN_FC)
        a = jnp.dot(h, wup_ref[0, 0, :, cols], preferred_element_type=F32)
        b = jnp.dot(h, wup_ref[0, 0, :, D_FF + c * FFN_FC:D_FF + (c + 1) * FFN_FC], preferred_element_type=F32)
        act = (0.5 * a) * (1.0 + jnp.tanh(0.5 * a)) * b
        acc = acc + jnp.dot(act.astype(BF16), wdn_ref[0, 0, cols, :], preferred_element_type=F32)
    y = x + mod[2:3] * acc
    if final:
        ms = jnp.mean(y * y, axis=-1, keepdims=True)
        y = y * lax.rsqrt(ms + NORM_EPS) * fw_ref[...]
    o_ref[0] = y


def _ffn(x, mod, wup, wdn, layer, which, final_w=None):
    B, S, D = x.shape
    n_chunks = D_FF // FFN_FC
    final = final_w is not None
    in_specs = [
        pl.BlockSpec((1, FFN_TM, D), lambda b, t: (b, t, 0)),
        pl.BlockSpec((1, 3, D), lambda b, t: (b, 0, 0)),
        _resident_slice(wup.shape, (layer, which)),
        _resident_slice(wdn.shape, (layer, which)),
    ]
    args = [x, mod, wup, wdn]
    if final:
        in_specs.append(_resident((1, D)))
        args.append(final_w.reshape(1, D))
    return pl.pallas_call(
        functools.partial(_ffn_kernel, n_chunks=n_chunks, final=final),
        grid=(B, S // FFN_TM),
        in_specs=in_specs,
        out_specs=pl.BlockSpec((1, FFN_TM, D), lambda b, t: (b, t, 0)),
        out_shape=jax.ShapeDtypeStruct(x.shape, F32),
        compiler_params=_params("parallel", "parallel"),
        name="ffn_final" if final else "ffn",
    )(*args)


def _inproj_kernel(x_ref, mod_ref, w_ref, rq_ref, rk_ref, rv_ref, rg_ref, dq_ref, dk_ref, dv_ref,
                   gr_ref, gd_ref):
    x = x_ref[0]
    h = _modulated_norm(x, mod_ref[0]).astype(BF16)

    def proj(c):
        return jnp.dot(h, w_ref[0, :, c * D_MODEL:(c + 1) * D_MODEL], preferred_element_type=F32)

    u = proj(0).astype(BF16)
    for hh in range(RET_HEADS):
        rq_ref[0, hh] = u[:, hh * RET_QK_DIM:(hh + 1) * RET_QK_DIM]
        rk_ref[0, hh] = u[:, (RET_HEADS + hh) * RET_QK_DIM:(RET_HEADS + hh + 1) * RET_QK_DIM]
    u = proj(1).astype(BF16)
    for hh in range(RET_HEADS):
        rv_ref[0, hh] = u[:, hh * RET_V_DIM:(hh + 1) * RET_V_DIM]
    rg_ref[0] = proj(2).astype(BF16)
    u = (proj(3) * (DIFF_HEAD_DIM ** -0.5 * LOG2E)).astype(BF16)
    for hh in range(DIFF_HEADS):
        dq_ref[0, hh] = u[:, hh * DIFF_V_DIM:(hh + 1) * DIFF_V_DIM]
    u = proj(4).astype(BF16)
    for hh in range(DIFF_HEADS):
        dk_ref[0, hh] = u[:, hh * DIFF_V_DIM:(hh + 1) * DIFF_V_DIM]
    u = proj(5).astype(BF16)
    for hh in range(DIFF_HEADS):
        dv_ref[0, hh] = u[:, hh * DIFF_V_DIM:(hh + 1) * DIFF_V_DIM]
    gr_ref[0] = proj(6).astype(BF16)
    gd_ref[0] = proj(7).astype(BF16)


def _inproj(x, mod, w_in, layer):
    B, S, D = x.shape
    TM = PROJ_TM
    head_spec = lambda nh, w: pl.BlockSpec((1, nh, TM, w), lambda b, t: (b, 0, t, 0))
    tok_spec = pl.BlockSpec((1, TM, D), lambda b, t: (b, t, 0))
    sds = jax.ShapeDtypeStruct
    return pl.pallas_call(
        _inproj_kernel,
        grid=(B, S // TM),
        in_specs=[tok_spec, pl.BlockSpec((1, 3, D), lambda b, t: (b, 0, 0)),
                  _resident_slice(w_in.shape, (layer,))],
        out_specs=[
            head_spec(RET_HEADS, RET_QK_DIM), head_spec(RET_HEADS, RET_QK_DIM),
            head_spec(RET_HEADS, RET_V_DIM), tok_spec,
            head_spec(DIFF_HEADS, DIFF_V_DIM), head_spec(DIFF_HEADS, DIFF_V_DIM),
            head_spec(DIFF_HEADS, DIFF_V_DIM), tok_spec, tok_spec,
        ],
        out_shape=[
            sds((B, RET_HEADS, S, RET_QK_DIM), BF16), sds((B, RET_HEADS, S, RET_QK_DIM), BF16),
            sds((B, RET_HEADS, S, RET_V_DIM), BF16), sds((B, S, D), BF16),
            sds((B, DIFF_HEADS, S, DIFF_V_DIM), BF16), sds((B, DIFF_HEADS, S, DIFF_V_DIM), BF16),
            sds((B, DIFF_HEADS, S, DIFF_V_DIM), BF16), sds((B, S, D), BF16), sds((B, S, D), BF16),
        ],
        compiler_params=_params("parallel", "parallel"),
        name="mixer_inproj",
    )(x, mod, w_in)


def _retention_consts():
    L = RET_L
    gamma = 1.0 - 2.0 ** (-5.0 - jnp.arange(RET_HEADS, dtype=F32))
    log_g = jnp.log(gamma)
    r = jnp.arange(L, dtype=F32)
    scale = RET_QK_DIM ** -0.5
    allowed = (jnp.arange(L)[None, :] // CHUNK) <= (jnp.arange(L)[:, None] // CHUNK)
    intra = jnp.exp(log_g[:, None, None] * jnp.abs(r[:, None] - r[None, :]))
    intra = jnp.where(allowed[None], intra, 0.0) * scale
    q_dec = jnp.exp(log_g[:, None] * r[None, :]) * scale
    k_dec = jnp.exp(log_g[:, None] * (L - r)[None, :])
    blk_dec = jnp.exp(log_g * L)
    q_dec = jnp.broadcast_to(q_dec[:, :, None], (RET_HEADS, L, RET_QK_DIM))
    k_dec = jnp.broadcast_to(k_dec[:, :, None], (RET_HEADS, L, RET_QK_DIM))
    blk_dec = jnp.broadcast_to(blk_dec[:, None, None], (RET_HEADS, 1, RET_V_DIM))
    return intra, q_dec, k_dec, blk_dec


def _retention_kernel(q_ref, k_ref, v_ref, g_ref, w_ref, qd_ref, kd_ref, bd_ref, gn_ref, o_ref, state_ref):
    t = pl.program_id(1)

    @pl.when(t == 0)
    def _():
        state_ref[...] = jnp.zeros_like(state_ref)

    dv = v_ref.shape[-1]
    for h in range(q_ref.shape[1]):
        q = q_ref[0, h]
        k = k_ref[0, h]
        v = v_ref[0, h]
        s = lax.dot_general(q, k, (((1,), (1,)), ((), ())), preferred_element_type=F32)
        p = (s * w_ref[h]).astype(BF16)
        y = jnp.dot(p, v, preferred_element_type=F32)
        state = state_ref[h]
        qd = (q.astype(F32) * qd_ref[h]).astype(BF16)
        y = y + jnp.dot(qd, state.astype(BF16), preferred_element_type=F32)
        kd_t = (k.astype(F32) * kd_ref[h]).T.astype(BF16)
        state_ref[h] = state * bd_ref[h] + jnp.dot(kd_t, v, preferred_element_type=F32)
        ms = jnp.mean(y * y, axis=-1, keepdims=True)
        y = y * lax.rsqrt(ms + NORM_EPS) * gn_ref[h]
        g = g_ref[0, :, h * dv:(h + 1) * dv].astype(F32)
        o_ref[0, :, h * dv:(h + 1) * dv] = (y * (0.5 * g) * (1.0 + jnp.tanh(0.5 * g))).astype(BF16)


def _retention(rq, rk, rv, rg, ret_gn_l):
    B, H, S, dk = rq.shape
    dv = rv.shape[-1]
    L = RET_L
    intra, q_dec, k_dec, blk_dec = _retention_consts()
    blk = lambda w: pl.BlockSpec((1, H, L, w), lambda b, t: (b, 0, t, 0))
    tok = pl.BlockSpec((1, L, H * dv), lambda b, t: (b, t, 0))
    return pl.pallas_call(
        _retention_kernel,
        grid=(B, S // L),
        in_specs=[
            blk(dk), blk(dk), blk(dv), tok,
            _resident((H, L, L)), _resident((H, L, dk)), _resident((H, L, dk)), _resident((H, 1, dv)),
            _resident((H, 1, dv)),
        ],
        out_specs=tok,
        out_shape=jax.ShapeDtypeStruct((B, S, H * dv), BF16),
        scratch_shapes=[pltpu.VMEM((H, dk, dv), F32)],
        compiler_params=_params("parallel", "arbitrary"),
        name="retention",
    )(rq, rk, rv, rg, intra, q_dec, k_dec, blk_dec, ret_gn_l.reshape(H, 1, dv))


def _attn_bias_tables(n_heads):
    slopes = 2.0 ** (-8.0 * jnp.arange(1, n_heads + 1, dtype=F32) / n_heads) * LOG2E
    a = jnp.arange(ATT_TK)[None, :, None]
    c = (jnp.arange(ATT_KQ) * ATT_TQ)[:, None, None] + jnp.arange(ATT_TQ)[None, None, :]
    allowed = (a // CHUNK) <= (c // CHUNK)
    diag = slopes[:, None, None, None] * (-2.0 * jnp.maximum(a - c, 0).astype(F32))[None]
    diag = jnp.where(allowed[None], diag, NEG_BIG)
    diag = jnp.concatenate([diag, diag], axis=-1)
    p0 = slopes.astype(BF16)
    p1 = (slopes - p0.astype(F32)).astype(BF16)
    p2 = (slopes - p0.astype(F32) - p1.astype(F32)).astype(BF16)
    parts = jnp.stack([p0, p1, p2], axis=1).astype(F32)
    coef = jnp.concatenate([parts * CHUNK, parts, jnp.zeros((n_heads, V7X_LANES - 2 * ATT_POS_LANES), F32)], axis=1)
    coef = jnp.broadcast_to(coef.astype(BF16)[:, :, None], (n_heads, V7X_LANES, V7X_LANES))
    return slopes, diag, coef


def _diffattn_kernel(slope_ref, q_ref, k_ref, v_ref, db_ref, coef_ref, lam_ref, sub_ref, o_ref,
                     vt_ref, kaug_ref, qbd_ref, s_ref, p_ref, bmax_ref, alpha_ref, m_ref, knorm_ref, acc_ref,
                     fin_ref, *, lam_init):
    h = pl.program_id(1)
    TK = ATT_TK
    n_kv = vt_ref.shape[0]

    @pl.when(jnp.logical_and(pl.program_id(0) == 0, h == 0))
    def _():
        for c in range(n_kv):
            ones_row = lax.broadcasted_iota(jnp.int32, (ATT_SUM_ROWS, TK), 0) == 0
            vt_ref[c, DIFF_V_DIM:DIFF_V_DIM + ATT_SUM_ROWS] = ones_row.astype(F32).astype(BF16)
            pos = c * TK + lax.broadcasted_iota(jnp.int32, (TK, V7X_LANES), 0)
            lane = lax.broadcasted_iota(jnp.int32, (TK, V7X_LANES), 1)
            feat = jnp.where(lane < ATT_POS_LANES, jnp.right_shift(pos, CHUNK_SHIFT),
                             jnp.where(lane < 2 * ATT_POS_LANES, jnp.bitwise_and(pos, CHUNK - 1), 0))
            kaug_ref[c * TK:(c + 1) * TK, 2 * DIFF_HEAD_DIM:] = feat.astype(F32).astype(BF16)

    k1sq = jnp.zeros((1, TK), F32)
    k2sq = jnp.zeros((1, TK), F32)
    for c in range(n_kv):
        vt_ref[c, 0:DIFF_V_DIM] = v_ref[0, 0, c * TK:(c + 1) * TK, :].astype(F32).T.astype(BF16)
        k = k_ref[0, 0, c * TK:(c + 1) * TK, :]
        kaug_ref[c * TK:(c + 1) * TK, 0:2 * DIFF_HEAD_DIM] = k
        ksq_t = jnp.square(k.astype(F32).T)
        k1sq = jnp.maximum(k1sq, jnp.sum(ksq_t[0:DIFF_HEAD_DIM], axis=0, keepdims=True))
        k2sq = jnp.maximum(k2sq, jnp.sum(ksq_t[DIFF_HEAD_DIM:], axis=0, keepdims=True))
    knorm_ref[0:1] = jnp.broadcast_to(jnp.max(k1sq, axis=1, keepdims=True), (1, V7X_LANES))
    knorm_ref[1:2] = jnp.broadcast_to(jnp.max(k2sq, axis=1, keepdims=True), (1, V7X_LANES))

    def group(g, carry):
        _diffattn_group(g, h, slope_ref, q_ref, db_ref, coef_ref, lam_ref, sub_ref, o_ref, vt_ref, kaug_ref,
                        qbd_ref, s_ref, p_ref, bmax_ref, alpha_ref, m_ref, knorm_ref, acc_ref, fin_ref,
                        lam_init=lam_init)
        return carry

    fin_ref[...] = jnp.zeros(fin_ref.shape, F32)
    fin_ref[:, DIFF_V_DIM:DIFF_V_DIM + 1] = jnp.ones((ATT_TILES, 1, fin_ref.shape[2]), F32)
    n_groups = q_ref.shape[2] // (ATT_TILES * ATT_TQ)
    lax.fori_loop(0, n_groups, group, 0)
    _diffattn_finish(n_groups - 1, lam_ref, sub_ref, o_ref, fin_ref, lam_init=lam_init)


def _max_sq_norms(x):
    xsq = jnp.square(x.astype(F32))
    first = lax.broadcasted_iota(jnp.int32, xsq.shape, 1) < DIFF_HEAD_DIM
    n1 = jnp.sum(jnp.where(first, xsq, 0.0), axis=1, keepdims=True)
    n2 = jnp.sum(jnp.where(first, 0.0, xsq), axis=1, keepdims=True)
    return jnp.max(n1, axis=0, keepdims=True), jnp.max(n2, axis=0, keepdims=True)


def _diffattn_group(g, h, slope_ref, q_ref, db_ref, coef_ref, lam_ref, sub_ref, o_ref, vt_ref, kaug_ref,
                    qbd_ref, s_ref, p_ref, bmax_ref, alpha_ref, m_ref, knorm_ref, acc_ref, fin_ref, *, lam_init):
    TQ, TK = ATT_TQ, ATT_TK
    slope = slope_ref[h]
    tiles = range(ATT_TILES)
    tile_idx = [g * ATT_TILES + x for x in tiles]
    diag_blk = [g * (ATT_TILES // ATT_KQ) + x // ATT_KQ for x in tiles]

    def q_rows(x):
        return pl.ds(pl.multiple_of(tile_idx[x] * TQ, TQ), TQ)

    for x in tiles:
        q = q_ref[0, 0, q_rows(x), :]
        q_t = q.astype(F32).T
        row = lax.broadcasted_iota(jnp.int32, q_t.shape, 0)
        qbd_ref[x, 0:2 * DIFF_HEAD_DIM, 0:TQ] = jnp.where(row < DIFF_HEAD_DIM, q_t, 0.0).astype(BF16)
        qbd_ref[x, 0:2 * DIFF_HEAD_DIM, TQ:2 * TQ] = jnp.where(row >= DIFF_HEAD_DIM, q_t, 0.0).astype(BF16)
        qbd_ref[x, 2 * DIFF_HEAD_DIM:, :] = jnp.concatenate([coef_ref[0]] * (2 * TQ // V7X_LANES), axis=1)
        m_ref[x] = jnp.full(m_ref.shape[1:], NEG_BIG, F32)
        acc_ref[x] = jnp.zeros(acc_ref.shape[1:], F32)

    def block_of(x, t):
        return jnp.maximum(diag_blk[x] - t, 0)

    def stage_scores(x, t, slot, diagonal):
        k = kaug_ref[pl.ds(pl.multiple_of(block_of(x, t) * TK, TK), TK), :]
        s = jnp.dot(k, qbd_ref[x], preferred_element_type=F32)
        bmax = None
        for r in range(0, TK, ATT_EXP_ROWS):
            rows = slice(r, r + ATT_EXP_ROWS)
            u = s[rows] + db_ref[0, x % ATT_KQ, rows] if diagonal else s[rows]
            s_ref[x, slot, rows] = u
            cmax = jnp.max(u.reshape(ATT_EXP_ROWS // V7X_SUBLANES, V7X_SUBLANES, 2 * TQ), axis=0)
            bmax = cmax if bmax is None else jnp.maximum(bmax, cmax)
        bmax_ref[x, slot] = jnp.max(bmax, axis=0, keepdims=True)

    def stage_softmax(x, t, slot, diagonal):
        if diagonal:
            shift = 0.0
        else:
            shift = jnp.where(t < n_steps[x], 0.0, NEG_BIG)
        m = m_ref[x]
        m_new = jnp.maximum(m, bmax_ref[x, slot] + shift)
        alpha = jnp.exp2(m - m_new)
        m_ref[x] = m_new
        alpha_ref[x, slot] = alpha
        ref = m_new - shift
        for r in range(0, TK, ATT_EXP_ROWS):
            rows = slice(r, r + ATT_EXP_ROWS)
            p_ref[x, slot, rows] = jnp.exp2(s_ref[x, slot, rows] - ref).astype(BF16)

    def stage_pv(x, t, slot, last=False):
        pv = jnp.dot(vt_ref[block_of(x, t)], p_ref[x, slot], preferred_element_type=F32)
        out_ref = fin_ref if last else acc_ref
        out_ref[x] = alpha_ref[x, slot] * acc_ref[x] + pv

    _diffattn_finish(jnp.maximum(g - 1, 0), lam_ref, sub_ref, o_ref, fin_ref, lam_init=lam_init)
    for x in tiles:
        stage_scores(x, 0, 0, True)
    for x in tiles:
        stage_softmax(x, 0, 0, True)
    for x in tiles:
        stage_scores(x, 1, 1, False)

    n_steps = []
    for x in tiles:
        q1sq, q2sq = _max_sq_norms(q_ref[0, 0, q_rows(x), :])
        bound = jnp.sqrt(jnp.maximum(q1sq * knorm_ref[0:1, 0:1], q2sq * knorm_ref[1:2, 0:1])) * ATT_BOUND_MARGIN
        m_min = jnp.min(m_ref[x], axis=1, keepdims=True)
        blk = jnp.asarray(diag_blk[x], F32)
        reach = blk + (ATT_EXP2_ZERO + bound - m_min) / (slope * TK)
        n = jnp.minimum(jnp.floor(reach) + 2.0, blk + 1.0)
        n_steps.append(jnp.max(n).astype(jnp.int32))

    def pipeline_tick(t, slot):
        for x in tiles:
            stage_pv(x, t, slot)
        for x in tiles:
            stage_softmax(x, t + 1, 1 - slot, False)
        for x in tiles:
            stage_scores(x, t + 2, slot, False)

    def pipeline_drain(t, slot):
        for x in tiles:
            stage_pv(x, t, slot)
        for x in tiles:
            stage_softmax(x, t + 1, 1 - slot, False)
        for x in tiles:
            stage_pv(x, t + 1, 1 - slot, last=True)

    def by_parity(fn, t):
        for slot in (0, 1):
            @pl.when(jnp.bitwise_and(t, 1) == slot)
            def _():
                fn(t, slot)

    def body(t, carry):
        by_parity(pipeline_tick, t)
        return carry

    n_ticks = jnp.maximum(functools.reduce(jnp.maximum, n_steps), 2)
    lax.fori_loop(0, n_ticks - 2, body, 0)
    by_parity(pipeline_drain, n_ticks - 2)


def _diffattn_finish(g, lam_ref, sub_ref, o_ref, fin_ref, *, lam_init):
    TQ = ATT_TQ
    lp = lam_ref[...]
    lam = (jnp.exp(jnp.sum(lp[0:1] * lp[1:2], axis=-1, keepdims=True))
           - jnp.exp(jnp.sum(lp[2:3] * lp[3:4], axis=-1, keepdims=True)) + lam_init)
    for x in range(ATT_TILES):
        o = fin_ref[x, 0:DIFF_V_DIM] * (1.0 / fin_ref[x, DIFF_V_DIM:DIFF_V_DIM + 1])
        y = (o[:, :TQ] - lam * o[:, TQ:]).T
        ms = jnp.mean(y * y, axis=-1, keepdims=True)
        y = y * lax.rsqrt(ms + NORM_EPS) * sub_ref[...] * (1.0 - lam_init)
        o_ref[0, pl.ds(pl.multiple_of((g * ATT_TILES + x) * TQ, TQ), TQ)] = y.astype(BF16)


def _diffattn(dq, dk, dv, lam_params, subln, layer_idx):
    B, H, S, w = dq.shape
    lam_init = 0.8 - 0.6 * math.exp(-0.3 * layer_idx)
    slopes, diag_bias, coef = _attn_bias_tables(H)
    NT = ATT_TILES
    seq = pl.BlockSpec((1, 1, S, w), lambda b, h, sl: (b, h, 0, 0))
    grid_spec = pltpu.PrefetchScalarGridSpec(
        num_scalar_prefetch=1,
        grid=(B, H),
        in_specs=[
            seq, seq, seq,
            pl.BlockSpec((1, ATT_KQ, ATT_TK, 2 * ATT_TQ), lambda b, h, sl: (h, 0, 0, 0)),
            pl.BlockSpec((1, V7X_LANES, V7X_LANES), lambda b, h, sl: (h, 0, 0)),
            pl.BlockSpec((4, DIFF_HEAD_DIM), lambda b, h, sl: (0, 0)),
            pl.BlockSpec((1, w), lambda b, h, sl: (0, 0)),
        ],
        out_specs=pl.BlockSpec((1, S, w), lambda b, h, sl: (b, 0, h)),
        scratch_shapes=[
            pltpu.VMEM((S // ATT_TK, w + ATT_SUM_ROWS, ATT_TK), BF16),
            pltpu.VMEM((S, 2 * w), BF16),
            pltpu.VMEM((NT, 2 * w, 2 * ATT_TQ), BF16),
            pltpu.VMEM((NT, 2, ATT_TK, 2 * ATT_TQ), F32),
            pltpu.VMEM((NT, 2, ATT_TK, 2 * ATT_TQ), BF16),
            pltpu.VMEM((NT, 2, 1, 2 * ATT_TQ), F32),
            pltpu.VMEM((NT, 2, 1, 2 * ATT_TQ), F32),
            pltpu.VMEM((NT, 1, 2 * ATT_TQ), F32),
            pltpu.VMEM((V7X_SUBLANES, V7X_LANES), F32),
            pltpu.VMEM((NT, w + ATT_SUM_ROWS, 2 * ATT_TQ), F32),
            pltpu.VMEM((NT, w + ATT_SUM_ROWS, 2 * ATT_TQ), F32),
        ],
    )
    return pl.pallas_call(
        functools.partial(_diffattn_kernel, lam_init=lam_init),
        grid_spec=grid_spec,
        out_shape=jax.ShapeDtypeStruct((B, S, H * w), BF16),
        compiler_params=_params("arbitrary", "arbitrary"),
        name="diff_attention",
    )(slopes, dq, dk, dv, diag_bias, coef, lam_params, subln.reshape(1, w))


def _merge_kernel(x_ref, mod_ref, yr_ref, yd_ref, gr_ref, gd_ref, wr_ref, wd_ref, wo_ref, o_ref):
    def sigmoid(g):
        return 0.5 * (1.0 + jnp.tanh(0.5 * g))

    br = jnp.dot(yr_ref[0], wr_ref[0], preferred_element_type=F32)
    bd = jnp.dot(yd_ref[0], wd_ref[0], preferred_element_type=F32)
    merged = sigmoid(gr_ref[0].astype(F32)) * br + sigmoid(gd_ref[0].astype(F32)) * bd
    out = jnp.dot(merged.astype(BF16), wo_ref[0], preferred_element_type=F32)
    o_ref[0] = x_ref[0] + mod_ref[0][2:3] * out


def _merge(x, mod, y_ret, y_diff, g_ret, g_diff, w_rb, w_db, w_o, layer):
    B, S, D = x.shape
    tok = pl.BlockSpec((1, MERGE_TM, D), lambda b, t: (b, t, 0))
    return pl.pallas_call(
        _merge_kernel,
        grid=(B, S // MERGE_TM),
        in_specs=[tok, pl.BlockSpec((1, 3, D), lambda b, t: (b, 0, 0)), tok, tok, tok, tok,
                  _resident_slice(w_rb.shape, (layer,)), _resident_slice(w_db.shape, (layer,)),
                  _resident_slice(w_o.shape, (layer,))],
        out_specs=tok,
        out_shape=jax.ShapeDtypeStruct(x.shape, F32),
        compiler_params=_params("parallel", "parallel"),
        name="mixer_merge",
    )(x, mod, y_ret, y_diff, g_ret, g_diff, w_rb, w_db, w_o)


def kernel(x, c, w_ada, b_ada, norm_w, w_ffn_up, w_ffn_down, w_in, ret_gn, lambda_q1, lambda_k1, lambda_q2,
           lambda_k2, diff_subln, w_ret_branch, w_diff_branch, w_out, final_norm):
    B, S, D = x.shape
    assert D == D_MODEL and S % max(FFN_TM, MERGE_TM, RET_L, ATT_TILES * ATT_TQ) == 0
    assert ATT_TK == ATT_KQ * ATT_TQ and ATT_TILES % ATT_KQ == 0
    mod_all = _adaln(c, w_ada, b_ada, norm_w).reshape(DEPTH, B, N_SUB, 3, D)
    up, down, w_in_b = w_ffn_up.astype(BF16), w_ffn_down.astype(BF16), w_in.astype(BF16)
    w_rb, w_db, w_o = w_ret_branch.astype(BF16), w_diff_branch.astype(BF16), w_out.astype(BF16)
    for l in range(DEPTH):
        mod = [mod_all[l, :, s] for s in range(N_SUB)]
        x = _ffn(x, mod[0], up, down, l, 0)
        rq, rk, rv, rg, dq, dk, dv, g_ret, g_diff = _inproj(x, mod[1], w_in_b, l)
        y_ret = _retention(rq, rk, rv, rg, ret_gn[l])
        lam_params = jnp.stack([lambda_q1[l], lambda_k1[l], lambda_q2[l], lambda_k2[l]])
        y_diff = _diffattn(dq, dk, dv, lam_params, diff_subln[l], l)
        x = _merge(x, mod[1], y_ret, y_diff, g_ret, g_diff, w_rb, w_db, w_o, l)
        x = _ffn(x, mod[2], up, down, l, 1, final_w=final_norm if l == DEPTH - 1 else None)
    return x
```

```python
import functools
import math

import jax
import jax.numpy as jnp
from jax import lax
from jax.experimental import pallas as pl
from jax.experimental.pallas import tpu as pltpu

DEPTH = 4
D_MODEL = 1024
CHUNK = 64
CHUNK_SHIFT = 6
RET_HEADS = 4
RET_QK_DIM = 128
RET_V_DIM = 256
DIFF_HEADS = 8
DIFF_HEAD_DIM = 64
DIFF_V_DIM = 2 * DIFF_HEAD_DIM
D_FF = 2816
N_SUB = 3
NORM_EPS = 1e-6

F32 = jnp.float32
BF16 = jnp.bfloat16

V7X_VMEM_LIMIT_BYTES = 56 * 1024 * 1024
V7X_LANES = 128
V7X_SUBLANES = 8

FFN_TM = 1024
PROJ_TM = 512
MERGE_TM = 1024
FFN_FC = 256
RET_L = 256
ATT_TQ = 256
ATT_KQ = 2
ATT_TK = ATT_KQ * ATT_TQ
ATT_EXP_ROWS = 16
ATT_TILES = 4
ATT_POS_LANES = 3
ATT_EXP2_ZERO = 160.0
ATT_BOUND_MARGIN = 1.001
NEG_BIG = -1e30
LOG2E = math.log2(math.e)


def _resident(shape):
    nd = len(shape)
    return pl.BlockSpec(shape, lambda *_: (0,) * nd, pipeline_mode=pl.Buffered(1))


def _resident_slice(shape, lead):
    n_lead, nd = len(lead), len(shape)
    index = tuple(lead) + (0,) * (nd - n_lead)
    return pl.BlockSpec((1,) * n_lead + tuple(shape[n_lead:]), lambda *_: index, pipeline_mode=pl.Buffered(1))


def _params(*sem):
    return pltpu.CompilerParams(dimension_semantics=sem, vmem_limit_bytes=V7X_VMEM_LIMIT_BYTES)


def _adaln_kernel(c_ref, w_ref, b_ref, nw_ref, o_ref):
    j = pl.program_id(1)
    c = c_ref[...]
    cond = c / (1.0 + jnp.exp(-c))
    r = jnp.dot(cond, w_ref[0], preferred_element_type=F32,
                precision=lax.Precision.HIGHEST) + b_ref[0]
    kind = j % 3
    sub = j // 3
    r = jnp.where(kind == 1, (1.0 + r) * nw_ref[0, 0], r)
    r = jnp.where(jnp.logical_and(kind == 2, sub != 1), 0.5 * r, r)
    o_ref[0] = r


def _adaln(c, w_ada, b_ada, norm_w):
    B = c.shape[0]
    n_tiles = N_SUB * 3
    return pl.pallas_call(
        _adaln_kernel,
        grid=(DEPTH, n_tiles),
        in_specs=[
            pl.BlockSpec((B, D_MODEL), lambda l, j: (0, 0)),
            pl.BlockSpec((1, D_MODEL, D_MODEL), lambda l, j: (l, 0, j)),
            pl.BlockSpec((1, 1, D_MODEL), lambda l, j: (l, 0, j)),
            pl.BlockSpec((1, 1, 1, D_MODEL), lambda l, j: (l, j // 3, 0, 0)),
        ],
        out_specs=pl.BlockSpec((1, B, D_MODEL), lambda l, j: (l, 0, j)),
        out_shape=jax.ShapeDtypeStruct((DEPTH, B, n_tiles * D_MODEL), F32),
        compiler_params=_params("arbitrary", "arbitrary"),
        name="adaln_mod",
    )(c, w_ada, b_ada.reshape(DEPTH, 1, n_tiles * D_MODEL), norm_w.reshape(DEPTH, N_SUB, 1, D_MODEL))


def _modulated_norm(x, mod):
    ms = jnp.mean(x * x, axis=-1, keepdims=True)
    return x * lax.rsqrt(ms + NORM_EPS) * mod[1:2] + mod[0:1]


def _ffn_kernel(x_ref, mod_ref, wup_ref, wdn_ref, *rest, n_chunks, final):
    if final:
        fw_ref, o_ref = rest
    else:
        (o_ref,) = rest
    x = x_ref[0]
    mod = mod_ref[0]
    h = _modulated_norm(x, mod).astype(BF16)
    acc = jnp.zeros(x.shape, F32)
    for c in range(n_chunks):
        cols = slice(c * FFN_FC, (c + 1) * FFN_FC)
        a = jnp.dot(h, wup_ref[0, 0, :, cols], preferred_element_type=F32)
        b = jnp.dot(h, wup_ref[0, 0, :, D_FF + c * FFN_FC:D_FF + (c + 1) * FFN_FC], preferred_element_type=F32)
        act = (0.5 * a) * (1.0 + jnp.tanh(0.5 * a)) * b
        acc = acc + jnp.dot(act.astype(BF16), wdn_ref[0, 0, cols, :], preferred_element_type=F32)
    y = x + mod[2:3] * acc
    if final:
        ms = jnp.mean(y * y, axis=-1, keepdims=True)
        y = y * lax.rsqrt(ms + NORM_EPS) * fw_ref[...]
    o_ref[0] = y


def _ffn(x, mod, wup, wdn, layer, which, final_w=None):
    B, S, D = x.shape
    n_chunks = D_FF // FFN_FC
    final = final_w is not None
    in_specs = [
        pl.BlockSpec((1, FFN_TM, D), lambda b, t: (b, t, 0)),
        pl.BlockSpec((1, 3, D), lambda b, t: (b, 0, 0)),
        _resident_slice(wup.shape, (layer, which)),
        _resident_slice(wdn.shape, (layer, which)),
    ]
    args = [x, mod, wup, wdn]
    if final:
        in_specs.append(_resident((1, D)))
        args.append(final_w.reshape(1, D))
    return pl.pallas_call(
        functools.partial(_ffn_kernel, n_chunks=n_chunks, final=final),
        grid=(B, S // FFN_TM),
        in_specs=in_specs,
        out_specs=pl.BlockSpec((1, FFN_TM, D), lambda b, t: (b, t, 0)),
        out_shape=jax.ShapeDtypeStruct(x.shape, F32),
        compiler_params=_params("parallel", "parallel"),
        name="ffn_final" if final else "ffn",
    )(*args)


def _inproj_kernel(x_ref, mod_ref, w_ref, rq_ref, rk_ref, rv_ref, rg_ref, dq_ref, dk_ref, dv_ref,
                   gr_ref, gd_ref):
    x = x_ref[0]
    h = _modulated_norm(x, mod_ref[0]).astype(BF16)

    def proj(c):
        return jnp.dot(h, w_ref[0, :, c * D_MODEL:(c + 1) * D_MODEL], preferred_element_type=F32)

    u = proj(0).astype(BF16)
    for hh in range(RET_HEADS):
        rq_ref[0, hh] = u[:, hh * RET_QK_DIM:(hh + 1) * RET_QK_DIM]
        rk_ref[0, hh] = u[:, (RET_HEADS + hh) * RET_QK_DIM:(RET_HEADS + hh + 1) * RET_QK_DIM]
    u = proj(1).astype(BF16)
    for hh in range(RET_HEADS):
        rv_ref[0, hh] = u[:, hh * RET_V_DIM:(hh + 1) * RET_V_DIM]
    rg_ref[0] = proj(2).astype(BF16)
    u = (proj(3) * (DIFF_HEAD_DIM ** -0.5 * LOG2E)).astype(BF16)
    for hh in range(DIFF_HEADS):
        dq_ref[0, hh] = u[:, hh * DIFF_V_DIM:(hh + 1) * DIFF_V_DIM]
    u = proj(4).astype(BF16)
    for hh in range(DIFF_HEADS):
        dk_ref[0, hh] = u[:, hh * DIFF_V_DIM:(hh + 1) * DIFF_V_DIM]
    u = proj(5).astype(BF16)
    for hh in range(DIFF_HEADS):
        dv_ref[0, hh] = u[:, hh * DIFF_V_DIM:(hh + 1) * DIFF_V_DIM]
    gr_ref[0] = proj(6).astype(BF16)
    gd_ref[0] = proj(7).astype(BF16)


def _inproj(x, mod, w_in, layer):
    B, S, D = x.shape
    TM = PROJ_TM
    head_spec = lambda nh, w: pl.BlockSpec((1, nh, TM, w), lambda b, t: (b, 0, t, 0))
    tok_spec = pl.BlockSpec((1, TM, D), lambda b, t: (b, t, 0))
    sds = jax.ShapeDtypeStruct
    return pl.pallas_call(
        _inproj_kernel,
        grid=(B, S // TM),
        in_specs=[tok_spec, pl.BlockSpec((1, 3, D), lambda b, t: (b, 0, 0)),
                  _resident_slice(w_in.shape, (layer,))],
        out_specs=[
            head_spec(RET_HEADS, RET_QK_DIM), head_spec(RET_HEADS, RET_QK_DIM),
            head_spec(RET_HEADS, RET_V_DIM), tok_spec,
            head_spec(DIFF_HEADS, DIFF_V_DIM), head_spec(DIFF_HEADS, DIFF_V_DIM),
            head_spec(DIFF_HEADS, DIFF_V_DIM), tok_spec, tok_spec,
        ],
        out_shape=[
            sds((B, RET_HEADS, S, RET_QK_DIM), BF16), sds((B, RET_HEADS, S, RET_QK_DIM), BF16),
            sds((B, RET_HEADS, S, RET_V_DIM), BF16), sds((B, S, D), BF16),
            sds((B, DIFF_HEADS, S, DIFF_V_DIM), BF16), sds((B, DIFF_HEADS, S, DIFF_V_DIM), BF16),
            sds((B, DIFF_HEADS, S, DIFF_V_DIM), BF16), sds((B, S, D), BF16), sds((B, S, D), BF16),
        ],
        compiler_params=_params("parallel", "parallel"),
        name="mixer_inproj",
    )(x, mod, w_in)


def _retention_consts():
    L = RET_L
    gamma = 1.0 - 2.0 ** (-5.0 - jnp.arange(RET_HEADS, dtype=F32))
    log_g = jnp.log(gamma)
    r = jnp.arange(L, dtype=F32)
    scale = RET_QK_DIM ** -0.5
    allowed = (jnp.arange(L)[None, :] // CHUNK) <= (jnp.arange(L)[:, None] // CHUNK)
    intra = jnp.exp(log_g[:, None, None] * jnp.abs(r[:, None] - r[None, :]))
    intra = jnp.where(allowed[None], intra, 0.0) * scale
    q_dec = jnp.exp(log_g[:, None] * r[None, :]) * scale
    k_dec = jnp.exp(log_g[:, None] * (L - r)[None, :])
    blk_dec = jnp.exp(log_g * L)
    q_dec = jnp.broadcast_to(q_dec[:, :, None], (RET_HEADS, L, RET_QK_DIM))
    k_dec = jnp.broadcast_to(k_dec[:, :, None], (RET_HEADS, L, RET_QK_DIM))
    blk_dec = jnp.broadcast_to(blk_dec[:, None, None], (RET_HEADS, 1, RET_V_DIM))
    return intra, q_dec, k_dec, blk_dec


def _retention_kernel(q_ref, k_ref, v_ref, g_ref, w_ref, qd_ref, kd_ref, bd_ref, gn_ref, o_ref, state_ref):
    t = pl.program_id(1)

    @pl.when(t == 0)
    def _():
        state_ref[...] = jnp.zeros_like(state_ref)

    dv = v_ref.shape[-1]
    for h in range(q_ref.shape[1]):
        q = q_ref[0, h]
        k = k_ref[0, h]
        v = v_ref[0, h]
        s = lax.dot_general(q, k, (((1,), (1,)), ((), ())), preferred_element_type=F32)
        p = (s * w_ref[h]).astype(BF16)
        y = jnp.dot(p, v, preferred_element_type=F32)
        state = state_ref[h]
        qd = (q.astype(F32) * qd_ref[h]).astype(BF16)
        y = y + jnp.dot(qd, state.astype(BF16), preferred_element_type=F32)
        kd_t = (k.astype(F32) * kd_ref[h]).T.astype(BF16)
        state_ref[h] = state * bd_ref[h] + jnp.dot(kd_t, v, preferred_element_type=F32)
        ms = jnp.mean(y * y, axis=-1, keepdims=True)
        y = y * lax.rsqrt(ms + NORM_EPS) * gn_ref[h]
        g = g_ref[0, :, h * dv:(h + 1) * dv].astype(F32)
        o_ref[0, :, h * dv:(h + 1) * dv] = (y * (0.5 * g) * (1.0 + jnp.tanh(0.5 * g))).astype(BF16)


def _retention(rq, rk, rv, rg, ret_gn_l):
    B, H, S, dk = rq.shape
    dv = rv.shape[-1]
    L = RET_L
    intra, q_dec, k_dec, blk_dec = _retention_consts()
    blk = lambda w: pl.BlockSpec((1, H, L, w), lambda b, t: (b, 0, t, 0))
    tok = pl.BlockSpec((1, L, H * dv), lambda b, t: (b, t, 0))
    return pl.pallas_call(
        _retention_kernel,
        grid=(B, S // L),
        in_specs=[
            blk(dk), blk(dk), blk(dv), tok,
            _resident((H, L, L)), _resident((H, L, dk)), _resident((H, L, dk)), _resident((H, 1, dv)),
            _resident((H, 1, dv)),
        ],
        out_specs=tok,
        out_shape=jax.ShapeDtypeStruct((B, S, H * dv), BF16),
        scratch_shapes=[pltpu.VMEM((H, dk, dv), F32)],
        compiler_params=_params("parallel", "arbitrary"),
        name="retention",
    )(rq, rk, rv, rg, intra, q_dec, k_dec, blk_dec, ret_gn_l.reshape(H, 1, dv))


def _attn_bias_tables(n_heads):
    slopes = 2.0 ** (-8.0 * jnp.arange(1, n_heads + 1, dtype=F32) / n_heads) * LOG2E
    a = jnp.arange(ATT_TK)[None, :, None]
    c = (jnp.arange(ATT_KQ) * ATT_TQ)[:, None, None] + jnp.arange(ATT_TQ)[None, None, :]
    allowed = (a // CHUNK) <= (c // CHUNK)
    diag = slopes[:, None, None, None] * (-2.0 * jnp.maximum(a - c, 0).astype(F32))[None]
    diag = jnp.where(allowed[None], diag, NEG_BIG)
    diag = jnp.concatenate([diag, diag], axis=-1)
    p0 = slopes.astype(BF16)
    p1 = (slopes - p0.astype(F32)).astype(BF16)
    p2 = (slopes - p0.astype(F32) - p1.astype(F32)).astype(BF16)
    parts = jnp.stack([p0, p1, p2], axis=1).astype(F32)
    coef = jnp.concatenate([parts * CHUNK, parts, jnp.zeros((n_heads, V7X_LANES - 2 * ATT_POS_LANES), F32)], axis=1)
    coef = jnp.broadcast_to(coef.astype(BF16)[:, :, None], (n_heads, V7X_LANES, V7X_LANES))
    return slopes, diag, coef


def _diffattn_kernel(slope_ref, q_ref, k_ref, v_ref, db_ref, coef_ref, lam_ref, sub_ref, o_ref,
                     vt_ref, kaug_ref, qbd_ref, s_ref, p_ref, bmax_ref, alpha_ref, m_ref, knorm_ref, acc_ref,
                     fin_ref, l_ref, *, lam_init):
    h = pl.program_id(1)
    TK = ATT_TK
    n_kv = vt_ref.shape[0]

    @pl.when(jnp.logical_and(pl.program_id(0) == 0, h == 0))
    def _():
        for c in range(n_kv):
            pos = c * TK + lax.broadcasted_iota(jnp.int32, (TK, V7X_LANES), 0)
            lane = lax.broadcasted_iota(jnp.int32, (TK, V7X_LANES), 1)
            feat = jnp.where(lane < ATT_POS_LANES, jnp.right_shift(pos, CHUNK_SHIFT),
                             jnp.where(lane < 2 * ATT_POS_LANES, jnp.bitwise_and(pos, CHUNK - 1), 0))
            kaug_ref[c * TK:(c + 1) * TK, 2 * DIFF_HEAD_DIM:] = feat.astype(F32).astype(BF16)

    k1sq = jnp.zeros((1, TK), F32)
    k2sq = jnp.zeros((1, TK), F32)
    for c in range(n_kv):
        vt_ref[c, 0:DIFF_V_DIM] = v_ref[0, 0, c * TK:(c + 1) * TK, :].astype(F32).T.astype(BF16)
        k = k_ref[0, 0, c * TK:(c + 1) * TK, :]
        kaug_ref[c * TK:(c + 1) * TK, 0:2 * DIFF_HEAD_DIM] = k
        ksq_t = jnp.square(k.astype(F32).T)
        k1sq = jnp.maximum(k1sq, jnp.sum(ksq_t[0:DIFF_HEAD_DIM], axis=0, keepdims=True))
        k2sq = jnp.maximum(k2sq, jnp.sum(ksq_t[DIFF_HEAD_DIM:], axis=0, keepdims=True))
    knorm_ref[0:1] = jnp.broadcast_to(jnp.max(k1sq, axis=1, keepdims=True), (1, V7X_LANES))
    knorm_ref[1:2] = jnp.broadcast_to(jnp.max(k2sq, axis=1, keepdims=True), (1, V7X_LANES))

    def group(g, carry):
        _diffattn_group(g, h, slope_ref, q_ref, db_ref, coef_ref, lam_ref, sub_ref, o_ref, vt_ref, kaug_ref,
                        qbd_ref, s_ref, p_ref, bmax_ref, alpha_ref, m_ref, knorm_ref, acc_ref, fin_ref, l_ref,
                        lam_init=lam_init)
        return carry

    fin_ref[...] = jnp.zeros(fin_ref.shape, F32)
    l_ref[1] = jnp.ones(l_ref.shape[1:], F32)
    n_groups = q_ref.shape[2] // (ATT_TILES * ATT_TQ)
    lax.fori_loop(0, n_groups, group, 0)
    _diffattn_finish(n_groups - 1, lam_ref, sub_ref, o_ref, fin_ref, l_ref, lam_init=lam_init)


def _max_sq_norms(x):
    xsq = jnp.square(x.astype(F32))
    first = lax.broadcasted_iota(jnp.int32, xsq.shape, 1) < DIFF_HEAD_DIM
    n1 = jnp.sum(jnp.where(first, xsq, 0.0), axis=1, keepdims=True)
    n2 = jnp.sum(jnp.where(first, 0.0, xsq), axis=1, keepdims=True)
    return jnp.max(n1, axis=0, keepdims=True), jnp.max(n2, axis=0, keepdims=True)


def _diffattn_group(g, h, slope_ref, q_ref, db_ref, coef_ref, lam_ref, sub_ref, o_ref, vt_ref, kaug_ref,
                    qbd_ref, s_ref, p_ref, bmax_ref, alpha_ref, m_ref, knorm_ref, acc_ref, fin_ref, l_ref, *,
                    lam_init):
    TQ, TK = ATT_TQ, ATT_TK
    slope = slope_ref[h]
    tiles = range(ATT_TILES)
    tile_idx = [g * ATT_TILES + x for x in tiles]
    diag_blk = [g * (ATT_TILES // ATT_KQ) + x // ATT_KQ for x in tiles]

    def q_rows(x):
        return pl.ds(pl.multiple_of(tile_idx[x] * TQ, TQ), TQ)

    for x in tiles:
        q = q_ref[0, 0, q_rows(x), :]
        q_t = q.astype(F32).T
        row = lax.broadcasted_iota(jnp.int32, q_t.shape, 0)
        qbd_ref[x, 0:2 * DIFF_HEAD_DIM, 0:TQ] = jnp.where(row < DIFF_HEAD_DIM, q_t, 0.0).astype(BF16)
        qbd_ref[x, 0:2 * DIFF_HEAD_DIM, TQ:2 * TQ] = jnp.where(row >= DIFF_HEAD_DIM, q_t, 0.0).astype(BF16)
        qbd_ref[x, 2 * DIFF_HEAD_DIM:, :] = jnp.concatenate([coef_ref[0]] * (2 * TQ // V7X_LANES), axis=1)
        m_ref[x] = jnp.full(m_ref.shape[1:], NEG_BIG, F32)
        l_ref[0, x] = jnp.zeros(l_ref.shape[2:], F32)
        acc_ref[x] = jnp.zeros(acc_ref.shape[1:], F32)

    def block_of(x, t):
        return jnp.maximum(diag_blk[x] - t, 0)

    def stage_scores(x, t, slot, diagonal):
        k = kaug_ref[pl.ds(pl.multiple_of(block_of(x, t) * TK, TK), TK), :]
        s = jnp.dot(k, qbd_ref[x], preferred_element_type=F32)
        bmax = None
        for r in range(0, TK, ATT_EXP_ROWS):
            rows = slice(r, r + ATT_EXP_ROWS)
            u = s[rows] + db_ref[0, x % ATT_KQ, rows] if diagonal else s[rows]
            s_ref[x, slot, rows] = u
            cmax = jnp.max(u.reshape(ATT_EXP_ROWS // V7X_SUBLANES, V7X_SUBLANES, 2 * TQ), axis=0)
            bmax = cmax if bmax is None else jnp.maximum(bmax, cmax)
        bmax_ref[x, slot] = jnp.max(bmax, axis=0, keepdims=True)

    def stage_softmax(x, t, slot, diagonal):
        if diagonal:
            shift = 0.0
        else:
            shift = jnp.where(t < n_steps[x], 0.0, NEG_BIG)
        m = m_ref[x]
        m_new = jnp.maximum(m, bmax_ref[x, slot] + shift)
        alpha = jnp.exp2(m - m_new)
        m_ref[x] = m_new
        alpha_ref[x, slot] = alpha
        ref = m_new - shift
        psum = jnp.zeros((V7X_SUBLANES, 2 * TQ), F32)
        for r in range(0, TK, ATT_EXP_ROWS):
            rows = slice(r, r + ATT_EXP_ROWS)
            p = jnp.exp2(s_ref[x, slot, rows] - ref)
            p_ref[x, slot, rows] = p.astype(BF16)
            psum = psum + jnp.sum(p.reshape(ATT_EXP_ROWS // V7X_SUBLANES, V7X_SUBLANES, 2 * TQ), axis=0)
        l_ref[0, x] = alpha * l_ref[0, x] + jnp.sum(psum, axis=0, keepdims=True)

    def stage_pv(x, t, slot, last=False):
        pv = jnp.dot(vt_ref[block_of(x, t)], p_ref[x, slot], preferred_element_type=F32)
        out_ref = fin_ref if last else acc_ref
        out_ref[x] = alpha_ref[x, slot] * acc_ref[x] + pv
        if last:
            l_ref[1, x] = l_ref[0, x]

    _diffattn_finish(jnp.maximum(g - 1, 0), lam_ref, sub_ref, o_ref, fin_ref, l_ref, lam_init=lam_init)
    for x in tiles:
        stage_scores(x, 0, 0, True)
    for x in tiles:
        stage_softmax(x, 0, 0, True)
    for x in tiles:
        stage_scores(x, 1, 1, False)

    n_steps = []
    for x in tiles:
        q1sq, q2sq = _max_sq_norms(q_ref[0, 0, q_rows(x), :])
        bound = jnp.sqrt(jnp.maximum(q1sq * knorm_ref[0:1, 0:1], q2sq * knorm_ref[1:2, 0:1])) * ATT_BOUND_MARGIN
        m_min = jnp.min(m_ref[x], axis=1, keepdims=True)
        blk = jnp.asarray(diag_blk[x], F32)
        reach = blk + (ATT_EXP2_ZERO + bound - m_min) / (slope * TK)
        n = jnp.minimum(jnp.floor(reach) + 2.0, blk + 1.0)
        n_steps.append(jnp.max(n).astype(jnp.int32))

    def pipeline_tick(t, slot):
        for x in tiles:
            stage_pv(x, t, slot)
        for x in tiles:
            stage_softmax(x, t + 1, 1 - slot, False)
        for x in tiles:
            stage_scores(x, t + 2, slot, False)

    def pipeline_drain(t, slot):
        for x in tiles:
            stage_pv(x, t, slot)
        for x in tiles:
            stage_softmax(x, t + 1, 1 - slot, False)
        for x in tiles:
            stage_pv(x, t + 1, 1 - slot, last=True)

    def by_parity(fn, t):
        for slot in (0, 1):
            @pl.when(jnp.bitwise_and(t, 1) == slot)
            def _():
                fn(t, slot)

    def body(t, carry):
        by_parity(pipeline_tick, t)
        return carry

    n_ticks = jnp.maximum(functools.reduce(jnp.maximum, n_steps), 2)
    lax.fori_loop(0, n_ticks - 2, body, 0)
    by_parity(pipeline_drain, n_ticks - 2)


def _diffattn_finish(g, lam_ref, sub_ref, o_ref, fin_ref, l_ref, *, lam_init):
    TQ = ATT_TQ
    lp = lam_ref[...]
    lam = (jnp.exp(jnp.sum(lp[0:1] * lp[1:2], axis=-1, keepdims=True))
           - jnp.exp(jnp.sum(lp[2:3] * lp[3:4], axis=-1, keepdims=True)) + lam_init)
    for x in range(ATT_TILES):
        o = fin_ref[x] * (1.0 / l_ref[1, x])
        y = (o[:, :TQ] - lam * o[:, TQ:]).T
        ms = jnp.mean(y * y, axis=-1, keepdims=True)
        y = y * lax.rsqrt(ms + NORM_EPS) * sub_ref[...] * (1.0 - lam_init)
        o_ref[0, pl.ds(pl.multiple_of((g * ATT_TILES + x) * TQ, TQ), TQ)] = y.astype(BF16)


def _diffattn(dq, dk, dv, lam_params, subln, layer_idx):
    B, H, S, w = dq.shape
    lam_init = 0.8 - 0.6 * math.exp(-0.3 * layer_idx)
    slopes, diag_bias, coef = _attn_bias_tables(H)
    NT = ATT_TILES
    seq = pl.BlockSpec((1, 1, S, w), lambda b, h, sl: (b, h, 0, 0))
    grid_spec = pltpu.PrefetchScalarGridSpec(
        num_scalar_prefetch=1,
        grid=(B, H),
        in_specs=[
            seq, seq, seq,
            pl.BlockSpec((1, ATT_KQ, ATT_TK, 2 * ATT_TQ), lambda b, h, sl: (h, 0, 0, 0)),
            pl.BlockSpec((1, V7X_LANES, V7X_LANES), lambda b, h, sl: (h, 0, 0)),
            pl.BlockSpec((4, DIFF_HEAD_DIM), lambda b, h, sl: (0, 0)),
            pl.BlockSpec((1, w), lambda b, h, sl: (0, 0)),
        ],
        out_specs=pl.BlockSpec((1, S, w), lambda b, h, sl: (b, 0, h)),
        scratch_shapes=[
            pltpu.VMEM((S // ATT_TK, w, ATT_TK), BF16),
            pltpu.VMEM((S, 2 * w), BF16),
            pltpu.VMEM((NT, 2 * w, 2 * ATT_TQ), BF16),
            pltpu.VMEM((NT, 2, ATT_TK, 2 * ATT_TQ), F32),
            pltpu.VMEM((NT, 2, ATT_TK, 2 * ATT_TQ), BF16),
            pltpu.VMEM((NT, 2, 1, 2 * ATT_TQ), F32),
            pltpu.VMEM((NT, 2, 1, 2 * ATT_TQ), F32),
            pltpu.VMEM((NT, 1, 2 * ATT_TQ), F32),
            pltpu.VMEM((V7X_SUBLANES, V7X_LANES), F32),
            pltpu.VMEM((NT, w, 2 * ATT_TQ), F32),
            pltpu.VMEM((NT, w, 2 * ATT_TQ), F32),
            pltpu.VMEM((2, NT, 1, 2 * ATT_TQ), F32),
        ],
    )
    return pl.pallas_call(
        functools.partial(_diffattn_kernel, lam_init=lam_init),
        grid_spec=grid_spec,
        out_shape=jax.ShapeDtypeStruct((B, S, H * w), BF16),
        compiler_params=_params("arbitrary", "arbitrary"),
        name="diff_attention",
    )(slopes, dq, dk, dv, diag_bias, coef, lam_params, subln.reshape(1, w))


def _merge_kernel(x_ref, mod_ref, yr_ref, yd_ref, gr_ref, gd_ref, wr_ref, wd_ref, wo_ref, o_ref):
    def sigmoid(g):
        return 0.5 * (1.0 + jnp.tanh(0.5 * g))

    br = jnp.dot(yr_ref[0], wr_ref[0], preferred_element_type=F32)
    bd = jnp.dot(yd_ref[0], wd_ref[0], preferred_element_type=F32)
    merged = sigmoid(gr_ref[0].astype(F32)) * br + sigmoid(gd_ref[0].astype(F32)) * bd
    out = jnp.dot(merged.astype(BF16), wo_ref[0], preferred_element_type=F32)
    o_ref[0] = x_ref[0] + mod_ref[0][2:3] * out


def _merge(x, mod, y_ret, y_diff, g_ret, g_diff, w_rb, w_db, w_o, layer):
    B, S, D = x.shape
    tok = pl.BlockSpec((1, MERGE_TM, D), lambda b, t: (b, t, 0))
    return pl.pallas_call(
        _merge_kernel,
        grid=(B, S // MERGE_TM),
        in_specs=[tok, pl.BlockSpec((1, 3, D), lambda b, t: (b, 0, 0)), tok, tok, tok, tok,
                  _resident_slice(w_rb.shape, (layer,)), _resident_slice(w_db.shape, (layer,)),
                  _resident_slice(w_o.shape, (layer,))],
        out_specs=tok,
        out_shape=jax.ShapeDtypeStruct(x.shape, F32),
        compiler_params=_params("parallel", "parallel"),
        name="mixer_merge",
    )(x, mod, y_ret, y_diff, g_ret, g_diff, w_rb, w_db, w_o)


def kernel(x, c, w_ada, b_ada, norm_w, w_ffn_up, w_ffn_down, w_in, ret_gn, lambda_q1, lambda_k1, lambda_q2,
           lambda_k2, diff_subln, w_ret_branch, w_diff_branch, w_out, final_norm):
    B, S, D = x.shape
    assert D == D_MODEL and S % max(FFN_TM, MERGE_TM, RET_L, ATT_TILES * ATT_TQ) == 0
    assert ATT_TK == ATT_KQ * ATT_TQ and ATT_TILES % ATT_KQ == 0
    mod_all = _adaln(c, w_ada, b_ada, norm_w).reshape(DEPTH, B, N_SUB, 3, D)
    up, down, w_in_b = w_ffn_up.astype(BF16), w_ffn_down.astype(BF16), w_in.astype(BF16)
    w_rb, w_db, w_o = w_ret_branch.astype(BF16), w_diff_branch.astype(BF16), w_out.astype(BF16)
    for l in range(DEPTH):
        mod = [mod_all[l, :, s] for s in range(N_SUB)]
        x = _ffn(x, mod[0], up, down, l, 0)
        rq, rk, rv, rg, dq, dk, dv, g_ret, g_diff = _inproj(x, mod[1], w_in_b, l)
        y_ret = _retention(rq, rk, rv, rg, ret_gn[l])
        lam_params = jnp.stack([lambda_q1[l], lambda_k1[l], lambda_q2[l], lambda_k2[l]])
        y_diff = _diffattn(dq, dk, dv, lam_params, diff_subln[l], l)
        x = _merge(x, mod[1], y_ret, y_diff, g_ret, g_diff, w_rb, w_db, w_o, l)
        x = _ffn(x, mod[2], up, down, l, 1, final_w=final_norm if l == DEPTH - 1 else None)
    return x
```
